```python
import jax, jax.numpy as jnp
from jax import lax
import numpy as np

D_MODEL = 2048
BATCH = 2
SEQ = 4096
DEPTH = 1

RET_HEADS = 4
RET_QK_DIM = 256
RET_V_DIM = 256
RET_WIDTH = RET_HEADS * RET_QK_DIM
RET_CHUNK = 128
ROPE_BASE = 10000.0
RWKV_HEAD = 64
RWKV_WIDTH = 1024
RWKV_HEADS = RWKV_WIDTH // RWKV_HEAD
DECAY_LORA = 96
ICLR_LORA = 96
GATE_LORA = 256
FFN_HIDDEN = ((8 * D_MODEL // 3 + 255) // 256) * 256
N_BRANCH = 2
RET_COLS = 4 * RET_WIDTH
SHIFT_COLS = 3 * RWKV_WIDTH + DECAY_LORA + ICLR_LORA + GATE_LORA
GATE_COLS = N_BRANCH * D_MODEL
IN_COLS = RET_COLS + SHIFT_COLS + GATE_COLS
NORM_EPS = 1e-6
GN_EPS_RET = 1e-5
GN_EPS_RWKV = 64e-5

kernel_name = "hybrid_retention_rwkv7_adaln_block"

F32 = jnp.float32


def rms_norm(x, gain):
    xf = x.astype(F32)
    y = xf * lax.rsqrt(jnp.mean(xf * xf, axis=-1, keepdims=True) + NORM_EPS)
    return (y * gain.astype(F32)).astype(x.dtype)


def head_group_norm(y, eps):
    mu = jnp.mean(y, axis=-1, keepdims=True)
    var = jnp.mean(jnp.square(y - mu), axis=-1, keepdims=True)
    return (y - mu) * lax.rsqrt(var + eps)


def rotary(t, pos):
    half = t.shape[-1] // 2
    inv_freq = ROPE_BASE ** (-jnp.arange(half, dtype=F32) / half)
    ang = pos.astype(F32)[..., None] * inv_freq
    cos = jnp.cos(ang)[:, :, None, :]
    sin = jnp.sin(ang)[:, :, None, :]
    t1, t2 = t[..., :half], t[..., half:]
    return jnp.concatenate([t1 * cos - t2 * sin, t1 * sin + t2 * cos], axis=-1)


def retention_chunkwise(q, k, v, pos):
    b, s, h, dk = q.shape
    dv = v.shape[-1]
    n = s // RET_CHUNK
    q = rotary(q, pos)
    k = rotary(k, pos) * (dk ** -0.5)

    def to_chunks(t):
        return t.reshape(b, n, RET_CHUNK, h, t.shape[-1]).transpose(0, 3, 1, 2, 4)

    qc, kc, vc = to_chunks(q), to_chunks(k), to_chunks(v)
    log_gamma = jnp.log(1.0 - 2.0 ** (-5.0 - jnp.arange(h, dtype=F32)))
    idx = jnp.arange(RET_CHUNK, dtype=F32)
    dist = idx[:, None] - idx[None, :]
    decay_intra = jnp.where(dist >= 0,
                            jnp.exp(log_gamma[:, None, None] * jnp.maximum(dist, 0.0)),
                            0.0)
    scores = jnp.einsum('bhncd,bhnsd->bhncs', qc, kc) * decay_intra[None, :, None]
    intra = jnp.einsum('bhncs,bhnsv->bhncv', scores, vc)

    zeta = jnp.exp(log_gamma[:, None] * (RET_CHUNK - 1.0 - idx))
    kv = jnp.einsum('bhncd,hc,bhncv->bhndv', kc, zeta, vc)
    chunk_decay = jnp.exp(log_gamma * RET_CHUNK)[None, :, None, None]

    def step(state, kv_n):
        return state * chunk_decay + kv_n, state

    _, states_before = lax.scan(step, jnp.zeros((b, h, dk, dv), F32), jnp.moveaxis(kv, 2, 0))
    states_before = jnp.moveaxis(states_before, 0, 2)
    xi = jnp.exp(log_gamma[:, None] * (idx + 1.0))
    cross = jnp.einsum('bhncd,bhndv->bhncv', qc, states_before) * xi[None, :, None, :, None]
    out = intra + cross
    return out.transpose(0, 2, 3, 1, 4).reshape(b, s, h, dv)


def rwkv7_recurrence(r, w, k, v, a_vec, b_vec):
    b, s, h, n = r.shape

    def step(state, inp):
        r_t, w_t, k_t, v_t, a_t, b_t = inp
        sa = jnp.einsum('bhvk,bhk->bhv', state, a_t)
        state = (state * w_t[:, :, None, :] + sa[..., None] * b_t[:, :, None, :]
                 + v_t[..., None] * k_t[:, :, None, :])
        return state, jnp.einsum('bhvk,bhk->bhv', state, r_t)

    xs = tuple(jnp.moveaxis(t, 1, 0) for t in (r, w, k, v, a_vec, b_vec))
    _, ys = lax.scan(step, jnp.zeros((b, h, n, n), F32), xs)
    return jnp.moveaxis(ys, 0, 1)


def hybrid_mixer(u, pos, w_in, b_gate, mu_shift, w0, w_decay_up, a0, w_iclr_up, w_gate_up,
                 k_k, k_a, r_k, lnx_w, lnx_b, w_ret_out, w_rwkv_out, w_o):
    bsz, s, _ = u.shape
    z = u @ w_in
    z_ret = z[..., :RET_COLS]
    z_rw = z[..., RET_COLS:RET_COLS + SHIFT_COLS]
    z_gate = z[..., RET_COLS + SHIFT_COLS:]

    q, k, v, g_ret = jnp.split(z_ret.astype(F32), 4, axis=-1)
    q = q.reshape(bsz, s, RET_HEADS, RET_QK_DIM)
    k = k.reshape(bsz, s, RET_HEADS, RET_QK_DIM)
    v = v.reshape(bsz, s, RET_HEADS, RET_V_DIM)
    y_ret = head_group_norm(retention_chunkwise(q, k, v, pos), GN_EPS_RET)
    y_ret = (jax.nn.silu(g_ret) * y_ret.reshape(bsz, s, RET_WIDTH)).astype(u.dtype)

    z_rw = z_rw.astype(F32)
    z_prev = jnp.pad(z_rw[:, :-1], ((0, 0), (1, 0), (0, 0)))
    z_rw = z_rw + (z_prev - z_rw) * mu_shift.astype(F32)
    o1, o2, o3 = RWKV_WIDTH, 2 * RWKV_WIDTH, 3 * RWKV_WIDTH
    o4, o5 = o3 + DECAY_LORA, o3 + DECAY_LORA + ICLR_LORA
    r, kw, vw = z_rw[..., :o1], z_rw[..., o1:o2], z_rw[..., o2:o3]
    zw, za, zg = z_rw[..., o3:o4], z_rw[..., o4:o5], z_rw[..., o5:]
    w_log = -jax.nn.softplus(-(w0.astype(F32) + jnp.tanh(zw) @ w_decay_up.astype(F32))) - 0.5
    decay = jnp.exp(-jnp.exp(w_log))
    iclr = jax.nn.sigmoid(a0.astype(F32) + za @ w_iclr_up.astype(F32))
    g_rw = jax.nn.sigmoid(zg) @ w_gate_up.astype(F32)

    heads = lambda t: t.reshape(bsz, s, RWKV_HEADS, RWKV_HEAD)
    r, kw, vw, decay, iclr = heads(r), heads(kw), heads(vw), heads(decay), heads(iclr)
    kk = kw * k_k.astype(F32).reshape(RWKV_HEADS, RWKV_HEAD)
    kk = kk / jnp.maximum(jnp.linalg.norm(kk, axis=-1, keepdims=True), 1e-12)
    k_mod = kw * (1.0 + (iclr - 1.0) * k_a.astype(F32).reshape(RWKV_HEADS, RWKV_HEAD))
    y_rw = rwkv7_recurrence(r, decay, k_mod, vw, -kk, kk * iclr)
    y_rw = (head_group_norm(y_rw, GN_EPS_RWKV) * lnx_w.astype(F32).reshape(RWKV_HEADS, RWKV_HEAD)
            + lnx_b.astype(F32).reshape(RWKV_HEADS, RWKV_HEAD))
    y_rw = y_rw + jnp.sum(r * k_mod * r_k.astype(F32), axis=-1, keepdims=True) * vw
    y_rw = (y_rw.reshape(bsz, s, RWKV_WIDTH) * g_rw).astype(u.dtype)

    gates = jax.nn.sigmoid(z_gate + b_gate)
    gate_ret, gate_rw = jnp.split(gates, N_BRANCH, axis=-1)
    merged = gate_ret * (y_ret @ w_ret_out) + gate_rw * (y_rw @ w_rwkv_out)
    return merged @ w_o


def swiglu(u, w_ffn_in, w_ffn_out):
    gate, up = jnp.split(u @ w_ffn_in, 2, axis=-1)
    return (jax.nn.silu(gate) * up) @ w_ffn_out


def setup_inputs(seed: int = 0) -> dict:
    key = jax.random.key(seed)
    ks = jax.random.split(key, 32)
    nrm = lambda k, shape, scale: jax.random.normal(k, shape, F32) * scale
    L, D = DEPTH, D_MODEL
    strides = jax.random.randint(ks[2], (BATCH, SEQ), 1, 3, dtype=jnp.int32)
    offset = jax.random.randint(ks[3], (BATCH, 1), 0, 1024, dtype=jnp.int32)
    positions = (offset + jnp.cumsum(strides, axis=1) - 1).astype(jnp.int32)
    w0 = (jnp.broadcast_to(jnp.linspace(-6.0, -1.0, RWKV_WIDTH, dtype=F32), (L, RWKV_WIDTH))
          + nrm(ks[9], (L, RWKV_WIDTH), 0.1))
    return {
        "x": nrm(ks[0], (BATCH, SEQ, D), 1.0),
        "c": nrm(ks[1], (BATCH, D), 1.0),
        "positions": positions,
        "w_ada": nrm(ks[4], (L, D, 6 * D), 0.5 * D ** -0.5),
        "b_ada": nrm(ks[5], (L, 6 * D), 0.01),
        "norm_mix": 1.0 + nrm(ks[6], (L, D), 0.02),
        "norm_ffn": 1.0 + nrm(ks[7], (L, D), 0.02),
        "norm_final": 1.0 + nrm(ks[8], (D,), 0.02),
        "w_in": nrm(ks[10], (L, D, IN_COLS), D ** -0.5),
        "b_gate": nrm(ks[11], (L, GATE_COLS), 0.1),
        "mu_shift": jax.random.uniform(ks[12], (L, SHIFT_COLS), F32),
        "w0": w0,
        "w_decay_up": nrm(ks[13], (L, DECAY_LORA, RWKV_WIDTH), 0.1),
        "a0": nrm(ks[14], (L, RWKV_WIDTH), 0.1),
        "w_iclr_up": nrm(ks[15], (L, ICLR_LORA, RWKV_WIDTH), 0.5 * ICLR_LORA ** -0.5),
        "w_gate_up": nrm(ks[16], (L, GATE_LORA, RWKV_WIDTH), GATE_LORA ** -0.5),
        "k_k": 0.85 + nrm(ks[17], (L, RWKV_WIDTH), 0.02),
        "k_a": 1.0 + nrm(ks[18], (L, RWKV_WIDTH), 0.02),
        "r_k": nrm(ks[19], (L, RWKV_HEADS, RWKV_HEAD), 0.1),
        "lnx_w": 1.0 + nrm(ks[20], (L, RWKV_WIDTH), 0.02),
        "lnx_b": nrm(ks[21], (L, RWKV_WIDTH), 0.01),
        "w_ret_out": nrm(ks[22], (L, RET_WIDTH, D), RET_WIDTH ** -0.5),
        "w_rwkv_out": nrm(ks[23], (L, RWKV_WIDTH, D), RWKV_WIDTH ** -0.5),
        "w_o": nrm(ks[24], (L, D, D), D ** -0.5),
        "w_ffn_in": nrm(ks[25], (L, D, 2 * FFN_HIDDEN), D ** -0.5),
        "w_ffn_out": nrm(ks[26], (L, FFN_HIDDEN, D), FFN_HIDDEN ** -0.5),
    }


def reference(x, c, positions, w_ada, b_ada, norm_mix, norm_ffn, norm_final, w_in, b_gate,
              mu_shift, w0, w_decay_up, a0, w_iclr_up, w_gate_up, k_k, k_a, r_k, lnx_w, lnx_b,
              w_ret_out, w_rwkv_out, w_o, w_ffn_in, w_ffn_out):
    h = x
    c_act = jax.nn.silu(c)
    for l in range(DEPTH):
        mod = c_act @ w_ada[l] + b_ada[l]
        sh1, sc1, g1, sh2, sc2, g2 = [m[:, None, :] for m in jnp.split(mod, 6, axis=-1)]
        u = rms_norm(h, norm_mix[l]) * (1.0 + sc1) + sh1
        h = h + g1 * hybrid_mixer(u, positions, w_in[l], b_gate[l], mu_shift[l], w0[l],
                                  w_decay_up[l], a0[l], w_iclr_up[l], w_gate_up[l], k_k[l],
                                  k_a[l], r_k[l], lnx_w[l], lnx_b[l], w_ret_out[l],
                                  w_rwkv_out[l], w_o[l])
        u = rms_norm(h, norm_ffn[l]) * (1.0 + sc2) + sh2
        h = h + g2 * swiglu(u, w_ffn_in[l], w_ffn_out[l])
    return rms_norm(h, norm_final)
```

```python
import functools

import jax
import jax.numpy as jnp
from jax import lax
from jax.experimental import pallas as pl
from jax.experimental.pallas import tpu as pltpu

F32 = jnp.float32
BF16 = jnp.bfloat16

D_MODEL = 2048
BATCH = 2
SEQ = 4096
TOKENS = BATCH * SEQ

RET_HEADS = 4
RET_DIM = 256
RET_WIDTH = RET_HEADS * RET_DIM
RET_CHUNK = 128
ROPE_BASE = 10000.0

RWKV_HEAD = 64
RWKV_WIDTH = 1024
RWKV_HEADS = RWKV_WIDTH // RWKV_HEAD
DECAY_LORA = 96
ICLR_LORA = 96
GATE_LORA = 256
RWKV_CHUNK = 64
GROUP_LANES = 256
GROUP_HEADS = GROUP_LANES // RWKV_HEAD
RWKV_GROUPS = RWKV_WIDTH // GROUP_LANES
LORA_PAD = 128

FFN_HIDDEN = ((8 * D_MODEL // 3 + 255) // 256) * 256
RET_COLS = 4 * RET_WIDTH
SHIFT_COLS = 3 * RWKV_WIDTH + DECAY_LORA + ICLR_LORA + GATE_LORA
GATE_COLS = 2 * D_MODEL
SEC = 4096
Z_COLS = 3 * SEC
NORM_EPS = 1e-6
GN_EPS_RET = 1e-5
GN_EPS_RWKV = 64e-5

VMEM_LIMIT = 56 * 1024 * 1024


def _params(*sem, vmem=VMEM_LIMIT):
    return pltpu.CompilerParams(dimension_semantics=sem, vmem_limit_bytes=vmem)


def _bdot(a, b):
    return jnp.dot(a.astype(BF16), b.astype(BF16), preferred_element_type=F32)


def _bdot_nt(a, b):
    return lax.dot_general(a.astype(BF16), b.astype(BF16), (((1,), (1,)), ((), ())),
                           preferred_element_type=F32)


def _bdot_tn(a, b):
    return lax.dot_general(a.astype(BF16), b.astype(BF16), (((0,), (0,)), ((), ())),
                           preferred_element_type=F32)


def _silu(x):
    return x * jax.nn.sigmoid(x)


ADA_TN = 1024


def _ada_kernel(c_ref, w_ref, b_ref, o_ref):
    o_ref[...] = _bdot(_silu(c_ref[...]), w_ref[...]) + b_ref[...]


def _ada(c_pad, w_ada, b_ada):
    n = w_ada.shape[1]
    return pl.pallas_call(
        _ada_kernel,
        grid=(n // ADA_TN,),
        in_specs=[pl.BlockSpec((8, D_MODEL), lambda j: (0, 0)),
                  pl.BlockSpec((D_MODEL, ADA_TN), lambda j: (0, j)),
                  pl.BlockSpec((1, ADA_TN), lambda j: (0, j))],
        out_specs=pl.BlockSpec((8, ADA_TN), lambda j: (0, j)),
        out_shape=jax.ShapeDtypeStruct((8, n), F32),
        compiler_params=_params("arbitrary"),
        name="ada",
    )(c_pad, w_ada, b_ada)


IN_TM = 1024
IN_TN = 1024


def _modulated_norm(x, gain, scale, shift):
    y = x * lax.rsqrt(jnp.mean(x * x, axis=-1, keepdims=True) + NORM_EPS)
    return y * gain * (1.0 + scale) + shift


def _inproj_kernel(x_ref, mod_ref, gain_ref, w_ref, o_ref, u_ref):
    @pl.when(pl.program_id(1) == 0)
    def _():
        mod = mod_ref[0]
        u_ref[...] = _modulated_norm(x_ref[...], gain_ref[...], mod[1:2], mod[0:1]).astype(BF16)

    o_ref[...] = jnp.dot(u_ref[...], w_ref[...], preferred_element_type=F32)


def _inproj(x2, mod3, gain, w_p):
    blocks_per_batch = SEQ // IN_TM
    return pl.pallas_call(
        _inproj_kernel,
        grid=(TOKENS // IN_TM, Z_COLS // IN_TN),
        in_specs=[pl.BlockSpec((IN_TM, D_MODEL), lambda i, j: (i, 0)),
                  pl.BlockSpec((1, 6, D_MODEL), lambda i, j: (i // blocks_per_batch, 0, 0)),
                  pl.BlockSpec((1, D_MODEL), lambda i, j: (0, 0)),
                  pl.BlockSpec((D_MODEL, IN_TN), lambda i, j: (0, j))],
        out_specs=pl.BlockSpec((IN_TM, IN_TN), lambda i, j: (i, j)),
        out_shape=jax.ShapeDtypeStruct((TOKENS, Z_COLS), F32),
        scratch_shapes=[pltpu.VMEM((IN_TM, D_MODEL), BF16)],
        compiler_params=_params("arbitrary", "arbitrary"),
        name="inproj",
    )(x2, mod3, gain, w_p)


def _ret_kernel(pos_ref, z_ref, invf_ref, dintra_ref, zeta_ref, xi_ref, cdec_ref, o_ref, state_ref):
    @pl.when(pl.program_id(1) == 0)
    def _():
        state_ref[...] = jnp.zeros_like(state_ref)

    ang = pos_ref[...].astype(F32) * invf_ref[...]
    cos = jnp.cos(ang)
    sin = jnp.sin(ang)
    half = RET_DIM // 2

    def rot(t):
        t1, t2 = t[:, :half], t[:, half:]
        return jnp.concatenate([t1 * cos - t2 * sin, t1 * sin + t2 * cos], axis=-1)

    for h in range(RET_HEADS):
        lo, hi = h * RET_DIM, (h + 1) * RET_DIM
        q = rot(z_ref[:, lo:hi])
        k = rot(z_ref[:, RET_WIDTH + lo:RET_WIDTH + hi]) * (RET_DIM ** -0.5)
        v = z_ref[:, 2 * RET_WIDTH + lo:2 * RET_WIDTH + hi].astype(BF16)
        g = z_ref[:, 3 * RET_WIDTH + lo:3 * RET_WIDTH + hi]
        qb = q.astype(BF16)
        scores = _bdot_nt(qb, k) * dintra_ref[h]
        state = state_ref[h]
        out = _bdot(scores, v) + _bdot(qb, state) * xi_ref[h]
        state_ref[h] = state * cdec_ref[h] + _bdot_tn(k * zeta_ref[h], v)
        mu = jnp.mean(out, axis=-1, keepdims=True)
        d = out - mu
        var = jnp.mean(d * d, axis=-1, keepdims=True)
        o_ref[:, lo:hi] = (_silu(g) * (d * lax.rsqrt(var + GN_EPS_RET))).astype(BF16)


def _retention_tables():
    h = RET_HEADS
    half = RET_DIM // 2
    inv_freq = ROPE_BASE ** (-jnp.arange(half, dtype=F32) / half)
    log_gamma = jnp.log(1.0 - 2.0 ** (-5.0 - jnp.arange(h, dtype=F32)))
    idx = jnp.arange(RET_CHUNK, dtype=F32)
    dist = idx[:, None] - idx[None, :]
    decay_intra = jnp.where(dist >= 0, jnp.exp(log_gamma[:, None, None] * jnp.maximum(dist, 0.0)), 0.0)
    zeta = jnp.exp(log_gamma[:, None] * (RET_CHUNK - 1.0 - idx))
    xi = jnp.exp(log_gamma[:, None] * (idx + 1.0))
    chunk_decay = jnp.exp(log_gamma * RET_CHUNK)
    wide = lambda t: jnp.broadcast_to(t[:, :, None], (h, RET_CHUNK, RET_DIM))
    cdec = jnp.broadcast_to(chunk_decay[:, None, None], (h, 1, RET_DIM))
    return inv_freq[None, :], decay_intra, wide(zeta), wide(xi), cdec


def _retention(pos_col, z):
    invf, dintra, zeta, xi, cdec = _retention_tables()
    nch = SEQ // RET_CHUNK
    full = lambda a: pl.BlockSpec(a.shape, lambda b, n: (0,) * a.ndim)
    return pl.pallas_call(
        _ret_kernel,
        grid=(BATCH, nch),
        in_specs=[pl.BlockSpec((RET_CHUNK, 1), lambda b, n: (b * nch + n, 0)),
                  pl.BlockSpec((RET_CHUNK, SEC), lambda b, n: (b * nch + n, 0)),
                  full(invf), full(dintra), full(zeta), full(xi), full(cdec)],
        out_specs=pl.BlockSpec((RET_CHUNK, RET_WIDTH), lambda b, n: (b * nch + n, 0)),
        out_shape=jax.ShapeDtypeStruct((TOKENS, RET_WIDTH), BF16),
        scratch_shapes=[pltpu.VMEM((RET_HEADS, RET_DIM, RET_DIM), F32)],
        compiler_params=_params("arbitrary", "arbitrary"),
        name="retention",
    )(pos_col, z, invf, dintra, zeta, xi, cdec)


def _split2(x):
    hi = x.astype(BF16)
    return hi, (x - hi.astype(F32)).astype(BF16)


def _head_sums(xs, ones_ref):
    c = RWKV_CHUNK
    pieces = []
    for x in xs:
        for p in _split2(x):
            for g in range(RWKV_GROUPS):
                pieces.append(p[:, g * GROUP_LANES:(g + 1) * GROUP_LANES])
    res = jnp.dot(jnp.concatenate(pieces, axis=0), ones_ref[...], preferred_element_type=F32)
    outs = []
    for i in range(len(xs)):
        base = i * 2 * RWKV_GROUPS * c
        cols = []
        for g in range(RWKV_GROUPS):
            hi = res[base + g * c:base + (g + 1) * c]
            lo = res[base + (RWKV_GROUPS + g) * c:base + (RWKV_GROUPS + g + 1) * c]
            cols.append(hi + lo)
        outs.append(jnp.concatenate(cols, axis=-1))
    return outs


def _rwkv_kernel(z_ref, mu_ref, par_ref, wd_ref, wa_ref, wg_ref, ltri_ref, ones_ref, mstrict_ref, mincl_ref,
                 o_ref, s_ref, carry_ref):
    c = RWKV_CHUNK
    w = RWKV_WIDTH

    @pl.when(pl.program_id(1) == 0)
    def _():
        s_ref[...] = jnp.zeros_like(s_ref)
        carry_ref[...] = jnp.zeros_like(carry_ref)

    z = z_ref[...]
    row = lax.broadcasted_iota(jnp.int32, z.shape, 0)
    prev = jnp.where(row == 0, carry_ref[...], pltpu.roll(z, 1, 0))
    carry_ref[...] = z_ref[c - 1:c, :]
    zs = z + (prev - z) * mu_ref[...]

    r, kw, vw = zs[:, :w], zs[:, w:2 * w], zs[:, 2 * w:3 * w]
    zw = zs[:, 3 * w:3 * w + LORA_PAD]
    za = zs[:, 3 * w + LORA_PAD:3 * w + 2 * LORA_PAD]
    zg = zs[:, 3 * w + 2 * LORA_PAD:3 * w + 2 * LORA_PAD + GATE_LORA]
    par = par_ref[...]
    w0, a0, k_k, k_a, r_k, lnx_w, lnx_b = (par[i:i + 1] for i in range(7))

    dec_pre = w0 + _bdot(jnp.tanh(zw), wd_ref[...])
    softplus = jnp.maximum(-dec_pre, 0.0) + jnp.log1p(jnp.exp(-jnp.abs(dec_pre)))
    logdec = -jnp.exp(-softplus - 0.5)
    iclr = jax.nn.sigmoid(a0 + _bdot(za, wa_ref[...]))
    g_rw = _bdot(jax.nn.sigmoid(zg), wg_ref[...])

    kk = kw * k_k
    k_mod = kw * (1.0 + (iclr - 1.0) * k_a)
    kk_sq, rk_sum = _head_sums([kk * kk, r * k_mod * r_k], ones_ref)
    kk = kk * (1.0 / jnp.maximum(jnp.sqrt(kk_sq), 1e-12))
    a = -kk
    b = kk * iclr
    bonus = rk_sum * vw

    ld_hi = logdec.astype(BF16)
    ld_mid, ld_lo = _split2(logdec - ld_hi.astype(F32))
    ltri = ltri_ref[...]
    cum = (jnp.dot(ltri, ld_hi, preferred_element_type=F32) + jnp.dot(ltri, ld_mid, preferred_element_type=F32)
           + jnp.dot(ltri, ld_lo, preferred_element_type=F32))
    cum_end = cum[c - 1:c, :]
    p_in = jnp.exp(cum)
    p_inv = jnp.exp(-cum)
    p_end = jnp.exp(cum_end - cum)
    a_t = a * jnp.exp(cum - logdec)
    b_t = (b * p_inv).astype(BF16)
    k_t = (k_mod * p_inv).astype(BF16)
    r_t = r * p_in
    b_e = b * p_end
    k_e = k_mod * p_end
    decay_end = jnp.exp(cum_end)

    lane_head = lax.shift_right_logical(lax.broadcasted_iota(jnp.int32, (c, GROUP_LANES), 1), 6)
    head_masks = [lane_head == h for h in range(GROUP_HEADS)]

    def stack(x):
        return jnp.concatenate([jnp.where(m, x, 0.0).astype(BF16) for m in head_masks], axis=0)

    def rep(x):
        return jnp.concatenate([x] * GROUP_HEADS, axis=0)

    m_strict = mstrict_ref[...]
    m_incl = mincl_ref[...]
    eye = (lax.broadcasted_iota(jnp.int32, (GROUP_LANES, GROUP_LANES), 0)
           == lax.broadcasted_iota(jnp.int32, (GROUP_LANES, GROUP_LANES), 1)).astype(F32)
    n4 = GROUP_HEADS * c
    ys = []
    for g in range(RWKV_GROUPS):
        sl = slice(g * GROUP_LANES, (g + 1) * GROUP_LANES)
        a_s, r_s, v_s, be_s, ke_s = stack(a_t[:, sl]), stack(r_t[:, sl]), stack(vw[:, sl]), stack(b_e[:, sl]), stack(k_e[:, sl])
        blk = _bdot_nt(jnp.concatenate([a_s, r_s], axis=0),
                       jnp.concatenate([rep(b_t[:, sl]), rep(k_t[:, sl])], axis=0))
        a_ab = blk[:n4, :n4] * m_strict
        a_ak = blk[:n4, n4:] * m_strict
        a_rb = blk[n4:, :n4] * m_incl
        a_rk = blk[n4:, n4:] * m_incl
        t_inv = eye + a_ab
        a_pow = a_ab
        for _ in range(5):
            a_pow = _bdot(a_pow, a_pow)
            t_inv = t_inv + _bdot(t_inv, a_pow)
        a_eff = _bdot(t_inv, a_s)
        u_const = _bdot(t_inv, _bdot(a_ak, v_s))
        state = s_ref[g]
        sb = state.astype(BF16)
        u = _bdot_nt(a_eff, sb) + u_const
        y = _bdot_nt(r_s, sb) + _bdot(a_rb, u) + _bdot(a_rk, v_s)
        s_ref[g] = (state * decay_end[:, sl]
                    + _bdot_tn(jnp.concatenate([u.astype(BF16), v_s], axis=0), jnp.concatenate([be_s, ke_s], axis=0)))
        ys.append(y[:c] + y[c:2 * c] + y[2 * c:3 * c] + y[3 * c:])
    y = jnp.concatenate(ys, axis=-1)

    (y_sum,) = _head_sums([y], ones_ref)
    d = y - y_sum * (1.0 / RWKV_HEAD)
    (d_sq,) = _head_sums([d * d], ones_ref)
    y = d * lax.rsqrt(d_sq * (1.0 / RWKV_HEAD) + GN_EPS_RWKV) * lnx_w + lnx_b + bonus
    o_ref[...] = (y * g_rw).astype(BF16)


def _rwkv_tables():
    c = RWKV_CHUNK
    ltri = (jnp.arange(c)[:, None] >= jnp.arange(c)[None, :]).astype(BF16)
    row = jnp.arange(GROUP_LANES)[:, None]
    col = jnp.arange(GROUP_LANES)[None, :]
    same_head = (row // RWKV_HEAD) == (col // RWKV_HEAD)
    ones = same_head.astype(BF16)
    m_strict = (same_head & (row > col)).astype(F32)
    m_incl = (same_head & (row >= col)).astype(F32)
    return ltri, ones, m_strict, m_incl


def _rwkv(z, mu_p, par, wd, wa, wg):
    assert RWKV_CHUNK == RWKV_HEAD
    ltri, ones, m_strict, m_incl = _rwkv_tables()
    nch = SEQ // RWKV_CHUNK
    full = lambda a: pl.BlockSpec(a.shape, lambda b, n: (0,) * a.ndim)
    return pl.pallas_call(
        _rwkv_kernel,
        grid=(BATCH, nch),
        in_specs=[pl.BlockSpec((RWKV_CHUNK, SEC), lambda b, n: (b * nch + n, 2)),
                  full(mu_p), full(par), full(wd), full(wa), full(wg),
                  full(ltri), full(ones), full(m_strict), full(m_incl)],
        out_specs=pl.BlockSpec((RWKV_CHUNK, RWKV_WIDTH), lambda b, n: (b * nch + n, 0)),
        out_shape=jax.ShapeDtypeStruct((TOKENS, RWKV_WIDTH), BF16),
        scratch_shapes=[pltpu.VMEM((RWKV_GROUPS, GROUP_LANES, GROUP_LANES), F32),
                        pltpu.VMEM((1, SEC), F32)],
        compiler_params=_params("arbitrary", "arbitrary"),
        name="rwkv",
    )(z, mu_p, par, wd, wa, wg, ltri, ones, m_strict, m_incl)


MG_TM = 1024
MG_TN = 1024


def _merge_kernel(yr_ref, yw_ref, wr_ref, ww_ref, gr_ref, gw_ref, br_ref, bw_ref, o_ref):
    pr = jnp.dot(yr_ref[...], wr_ref[...], preferred_element_type=F32)
    pw = jnp.dot(yw_ref[...], ww_ref[...], preferred_element_type=F32)
    o = jax.nn.sigmoid(gr_ref[...] + br_ref[...]) * pr + jax.nn.sigmoid(gw_ref[...] + bw_ref[...]) * pw
    o_ref[...] = o.astype(BF16)


def _merge(y_ret, y_rw, w_ret, w_rw, z, b_gate):
    g0 = SEC // MG_TN
    nd = D_MODEL // MG_TN
    return pl.pallas_call(
        _merge_kernel,
        grid=(TOKENS // MG_TM, nd),
        in_specs=[pl.BlockSpec((MG_TM, RET_WIDTH), lambda i, j: (i, 0)),
                  pl.BlockSpec((MG_TM, RWKV_WIDTH), lambda i, j: (i, 0)),
                  pl.BlockSpec((RET_WIDTH, MG_TN), lambda i, j: (0, j)),
                  pl.BlockSpec((RWKV_WIDTH, MG_TN), lambda i, j: (0, j)),
                  pl.BlockSpec((MG_TM, MG_TN), lambda i, j: (i, g0 + j)),
                  pl.BlockSpec((MG_TM, MG_TN), lambda i, j: (i, g0 + nd + j)),
                  pl.BlockSpec((1, MG_TN), lambda i, j: (0, j)),
                  pl.BlockSpec((1, MG_TN), lambda i, j: (0, nd + j))],
        out_specs=pl.BlockSpec((MG_TM, MG_TN), lambda i, j: (i, j)),
        out_shape=jax.ShapeDtypeStruct((TOKENS, D_MODEL), BF16),
        compiler_params=_params("arbitrary", "arbitrary"),
        name="merge",
    )(y_ret, y_rw, w_ret, w_rw, z, z, b_gate, b_gate)


OP_TM = 512


def _oproj_kernel(m_ref, w_ref, x_ref, mod_ref, gain_ref, h_ref, u_ref):
    mod = mod_ref[0]
    h = x_ref[...] + mod[2:3] * jnp.dot(m_ref[...], w_ref[...], preferred_element_type=F32)
    h_ref[...] = h
    u_ref[...] = _modulated_norm(h, gain_ref[...], mod[4:5], mod[3:4]).astype(BF16)


def _oproj(merged, w_o, x2, mod3, gain):
    blocks_per_batch = SEQ // OP_TM
    return pl.pallas_call(
        _oproj_kernel,
        grid=(TOKENS // OP_TM,),
        in_specs=[pl.BlockSpec((OP_TM, D_MODEL), lambda i: (i, 0)),
                  pl.BlockSpec((D_MODEL, D_MODEL), lambda i: (0, 0)),
                  pl.BlockSpec((OP_TM, D_MODEL), lambda i: (i, 0)),
                  pl.BlockSpec((1, 6, D_MODEL), lambda i: (i // blocks_per_batch, 0, 0)),
                  pl.BlockSpec((1, D_MODEL), lambda i: (0, 0))],
        out_specs=[pl.BlockSpec((OP_TM, D_MODEL), lambda i: (i, 0)),
                   pl.BlockSpec((OP_TM, D_MODEL), lambda i: (i, 0))],
        out_shape=[jax.ShapeDtypeStruct((TOKENS, D_MODEL), F32),
                   jax.ShapeDtypeStruct((TOKENS, D_MODEL), BF16)],
        compiler_params=_params("arbitrary"),
        name="oproj",
    )(merged, w_o, x2, mod3, gain)


FI_TM = 1024
FI_TN = 512


def _ffn_in_kernel(u_ref, wg_ref, wu_ref, o_ref):
    u = u_ref[...]
    gate = jnp.dot(u, wg_ref[...], preferred_element_type=F32)
    up = jnp.dot(u, wu_ref[...], preferred_element_type=F32)
    o_ref[...] = (_silu(gate) * up).astype(BF16)


def _ffn_in(u2, w_in):
    nt = FFN_HIDDEN // FI_TN
    return pl.pallas_call(
        _ffn_in_kernel,
        grid=(TOKENS // FI_TM, nt),
        in_specs=[pl.BlockSpec((FI_TM, D_MODEL), lambda i, j: (i, 0)),
                  pl.BlockSpec((D_MODEL, FI_TN), lambda i, j: (0, j)),
                  pl.BlockSpec((D_MODEL, FI_TN), lambda i, j: (0, nt + j))],
        out_specs=pl.BlockSpec((FI_TM, FI_TN), lambda i, j: (i, j)),
        out_shape=jax.ShapeDtypeStruct((TOKENS, FFN_HIDDEN), BF16),
        compiler_params=_params("arbitrary", "arbitrary"),
        name="ffn_in",
    )(u2, w_in, w_in)


FO_TM = 512
FO_TK = 512


def _ffn_out_kernel(a_ref, w_ref, h_ref, mod_ref, gain_ref, o_ref, acc_ref):
    k = pl.program_id(1)

    @pl.when(k == 0)
    def _():
        acc_ref[...] = jnp.zeros_like(acc_ref)

    acc_ref[...] += jnp.dot(a_ref[...], w_ref[...], preferred_element_type=F32)

    @pl.when(k == pl.num_programs(1) - 1)
    def _():
        h = h_ref[...] + mod_ref[0][5:6] * acc_ref[...]
        o_ref[...] = h * lax.rsqrt(jnp.mean(h * h, axis=-1, keepdims=True) + NORM_EPS) * gain_ref[...]


def _ffn_out(act, w_out, h1, mod3, gain):
    blocks_per_batch = SEQ // FO_TM
    return pl.pallas_call(
        _ffn_out_kernel,
        grid=(TOKENS // FO_TM, FFN_HIDDEN // FO_TK),
        in_specs=[pl.BlockSpec((FO_TM, FO_TK), lambda i, k: (i, k)),
                  pl.BlockSpec((FO_TK, D_MODEL), lambda i, k: (k, 0)),
                  pl.BlockSpec((FO_TM, D_MODEL), lambda i, k: (i, 0)),
                  pl.BlockSpec((1, 6, D_MODEL), lambda i, k: (i // blocks_per_batch, 0, 0)),
                  pl.BlockSpec((1, D_MODEL), lambda i, k: (0, 0))],
        out_specs=pl.BlockSpec((FO_TM, D_MODEL), lambda i, k: (i, 0)),
        out_shape=jax.ShapeDtypeStruct((TOKENS, D_MODEL), F32),
        scratch_shapes=[pltpu.VMEM((FO_TM, D_MODEL), F32)],
        compiler_params=_params("arbitrary", "arbitrary"),
        name="ffn_out",
    )(act, w_out, h1, mod3, gain)


def _pad_cols(t, width):
    return jnp.pad(t, ((0, 0), (0, width - t.shape[1])))


def _pack_w_in(w_in):
    o3 = RET_COLS + 3 * RWKV_WIDTH
    o4 = o3 + DECAY_LORA
    o5 = o4 + ICLR_LORA
    o6 = RET_COLS + SHIFT_COLS
    rw = jnp.concatenate([w_in[:, RET_COLS:o3], _pad_cols(w_in[:, o3:o4], LORA_PAD),
                          _pad_cols(w_in[:, o4:o5], LORA_PAD), w_in[:, o5:o6]], axis=1)
    return jnp.concatenate([w_in[:, :RET_COLS], w_in[:, o6:], _pad_cols(rw, SEC)], axis=1).astype(BF16)


def _pack_mu(mu):
    o3 = 3 * RWKV_WIDTH
    o4 = o3 + DECAY_LORA
    o5 = o4 + ICLR_LORA
    mu = mu[None, :]
    rw = jnp.concatenate([mu[:, :o3], _pad_cols(mu[:, o3:o4], LORA_PAD), _pad_cols(mu[:, o4:o5], LORA_PAD),
                          mu[:, o5:]], axis=1)
    return _pad_cols(rw, SEC)


def _pad_rows(t, rows):
    return jnp.pad(t, ((0, rows - t.shape[0]), (0, 0)))


def kernel(x, c, positions, w_ada, b_ada, norm_mix, norm_ffn, norm_final, w_in, b_gate, mu_shift, w0, w_decay_up, a0, w_iclr_up, w_gate_up, k_k, k_a, r_k, lnx_w, lnx_b, w_ret_out, w_rwkv_out, w_o, w_ffn_in, w_ffn_out):
    assert x.shape == (BATCH, SEQ, D_MODEL) and w_ada.shape[0] == 1
    x2 = x.reshape(TOKENS, D_MODEL)
    pos_col = positions.reshape(TOKENS, 1)

    mod = _ada(_pad_rows(c, 8), w_ada[0], b_ada)
    mod3 = mod[:BATCH].reshape(BATCH, 6, D_MODEL)

    z = _inproj(x2, mod3, norm_mix, _pack_w_in(w_in[0]))
    y_ret = _retention(pos_col, z)

    par = jnp.concatenate([w0, a0, k_k, k_a, r_k.reshape(1, RWKV_WIDTH), lnx_w, lnx_b,
                           jnp.zeros((1, RWKV_WIDTH), F32)], axis=0)
    y_rw = _rwkv(z, _pack_mu(mu_shift[0]), par,
                 _pad_rows(w_decay_up[0], LORA_PAD).astype(BF16), _pad_rows(w_iclr_up[0], LORA_PAD).astype(BF16),
                 w_gate_up[0].astype(BF16))

    merged = _merge(y_ret, y_rw, w_ret_out[0].astype(BF16), w_rwkv_out[0].astype(BF16), z, b_gate)
    h1, u2 = _oproj(merged, w_o[0].astype(BF16), x2, mod3, norm_ffn)
    act = _ffn_in(u2, w_ffn_in[0].astype(BF16))
    out = _ffn_out(act, w_ffn_out[0].astype(BF16), h1, mod3, norm_final[None, :])
    return out.reshape(BATCH, SEQ, D_MODEL)
```

```python
import functools

import jax
import jax.numpy as jnp
from jax import lax
from jax.experimental import pallas as pl
from jax.experimental.pallas import tpu as pltpu

F32 = jnp.float32
BF16 = jnp.bfloat16

D_MODEL = 2048
BATCH = 2
SEQ = 4096
TOKENS = BATCH * SEQ

RET_HEADS = 4
RET_DIM = 256
RET_WIDTH = RET_HEADS * RET_DIM
RET_CHUNK = 128
ROPE_BASE = 10000.0

RWKV_HEAD = 64
RWKV_WIDTH = 1024
RWKV_HEADS = RWKV_WIDTH // RWKV_HEAD
DECAY_LORA = 96
ICLR_LORA = 96
GATE_LORA = 256
RWKV_CHUNK = 64
GROUP_LANES = 256
GROUP_HEADS = GROUP_LANES // RWKV_HEAD
RWKV_GROUPS = RWKV_WIDTH // GROUP_LANES
LORA_PAD = 128

FFN_HIDDEN = ((8 * D_MODEL // 3 + 255) // 256) * 256
RET_COLS = 4 * RET_WIDTH
SHIFT_COLS = 3 * RWKV_WIDTH + DECAY_LORA + ICLR_LORA + GATE_LORA
GATE_COLS = 2 * D_MODEL
SEC = 4096
Z_COLS = 3 * SEC
NORM_EPS = 1e-6
GN_EPS_RET = 1e-5
GN_EPS_RWKV = 64e-5

VMEM_LIMIT = 56 * 1024 * 1024


def _params(*sem, vmem=VMEM_LIMIT):
    return pltpu.CompilerParams(dimension_semantics=sem, vmem_limit_bytes=vmem)


def _bdot(a, b):
    return jnp.dot(a.astype(BF16), b.astype(BF16), preferred_element_type=F32)


def _bdot_nt(a, b):
    return lax.dot_general(a.astype(BF16), b.astype(BF16), (((1,), (1,)), ((), ())),
                           preferred_element_type=F32)


def _bdot_tn(a, b):
    return lax.dot_general(a.astype(BF16), b.astype(BF16), (((0,), (0,)), ((), ())),
                           preferred_element_type=F32)


def _silu(x):
    return x * jax.nn.sigmoid(x)


ADA_TN = 1024


def _ada_kernel(c_ref, w_ref, b_ref, o_ref):
    o_ref[...] = _bdot(_silu(c_ref[...]), w_ref[...]) + b_ref[...]


def _ada(c_pad, w_ada, b_ada):
    n = w_ada.shape[1]
    return pl.pallas_call(
        _ada_kernel,
        grid=(n // ADA_TN,),
        in_specs=[pl.BlockSpec((8, D_MODEL), lambda j: (0, 0)),
                  pl.BlockSpec((D_MODEL, ADA_TN), lambda j: (0, j)),
                  pl.BlockSpec((1, ADA_TN), lambda j: (0, j))],
        out_specs=pl.BlockSpec((8, ADA_TN), lambda j: (0, j)),
        out_shape=jax.ShapeDtypeStruct((8, n), F32),
        compiler_params=_params("arbitrary"),
        name="ada",
    )(c_pad, w_ada, b_ada)


IN_TM = 1024
IN_TN = 1024


def _modulated_norm(x, gain, scale, shift):
    y = x * lax.rsqrt(jnp.mean(x * x, axis=-1, keepdims=True) + NORM_EPS)
    return y * gain * (1.0 + scale) + shift


def _inproj_kernel(x_ref, mod_ref, gain_ref, w_ref, o_ref, u_ref):
    @pl.when(pl.program_id(1) == 0)
    def _():
        mod = mod_ref[0]
        u_ref[...] = _modulated_norm(x_ref[...], gain_ref[...], mod[1:2], mod[0:1]).astype(BF16)

    o_ref[...] = jnp.dot(u_ref[...], w_ref[...], preferred_element_type=F32)


def _inproj(x2, mod3, gain, w_p):
    blocks_per_batch = SEQ // IN_TM
    return pl.pallas_call(
        _inproj_kernel,
        grid=(TOKENS // IN_TM, Z_COLS // IN_TN),
        in_specs=[pl.BlockSpec((IN_TM, D_MODEL), lambda i, j: (i, 0)),
                  pl.BlockSpec((1, 6, D_MODEL), lambda i, j: (i // blocks_per_batch, 0, 0)),
                  pl.BlockSpec((1, D_MODEL), lambda i, j: (0, 0)),
                  pl.BlockSpec((D_MODEL, IN_TN), lambda i, j: (0, j))],
        out_specs=pl.BlockSpec((IN_TM, IN_TN), lambda i, j: (i, j)),
        out_shape=jax.ShapeDtypeStruct((TOKENS, Z_COLS), F32),
        scratch_shapes=[pltpu.VMEM((IN_TM, D_MODEL), BF16)],
        compiler_params=_params("arbitrary", "arbitrary"),
        name="inproj",
    )(x2, mod3, gain, w_p)


def _ret_kernel(pos_ref, z_ref, invf_ref, dintra_ref, zeta_ref, xi_ref, cdec_ref, o_ref, state_ref):
    @pl.when(pl.program_id(1) == 0)
    def _():
        state_ref[...] = jnp.zeros_like(state_ref)

    ang = pos_ref[...].astype(F32) * invf_ref[...]
    cos = jnp.cos(ang)
    sin = jnp.sin(ang)
    half = RET_DIM // 2

    def rot(t):
        t1, t2 = t[:, :half], t[:, half:]
        return jnp.concatenate([t1 * cos - t2 * sin, t1 * sin + t2 * cos], axis=-1)

    for h in range(RET_HEADS):
        lo, hi = h * RET_DIM, (h + 1) * RET_DIM
        q = rot(z_ref[:, lo:hi])
        k = rot(z_ref[:, RET_WIDTH + lo:RET_WIDTH + hi]) * (RET_DIM ** -0.5)
        v = z_ref[:, 2 * RET_WIDTH + lo:2 * RET_WIDTH + hi].astype(BF16)
        g = z_ref[:, 3 * RET_WIDTH + lo:3 * RET_WIDTH + hi]
        qb = q.astype(BF16)
        scores = _bdot_nt(qb, k) * dintra_ref[h]
        state = state_ref[h]
        out = _bdot(scores, v) + _bdot(qb, state) * xi_ref[h]
        state_ref[h] = state * cdec_ref[h] + _bdot_tn(k * zeta_ref[h], v)
        mu = jnp.mean(out, axis=-1, keepdims=True)
        d = out - mu
        var = jnp.mean(d * d, axis=-1, keepdims=True)
        o_ref[:, lo:hi] = (_silu(g) * (d * lax.rsqrt(var + GN_EPS_RET))).astype(BF16)


def _retention_tables():
    h = RET_HEADS
    half = RET_DIM // 2
    inv_freq = ROPE_BASE ** (-jnp.arange(half, dtype=F32) / half)
    log_gamma = jnp.log(1.0 - 2.0 ** (-5.0 - jnp.arange(h, dtype=F32)))
    idx = jnp.arange(RET_CHUNK, dtype=F32)
    dist = idx[:, None] - idx[None, :]
    decay_intra = jnp.where(dist >= 0, jnp.exp(log_gamma[:, None, None] * jnp.maximum(dist, 0.0)), 0.0)
    zeta = jnp.exp(log_gamma[:, None] * (RET_CHUNK - 1.0 - idx))
    xi = jnp.exp(log_gamma[:, None] * (idx + 1.0))
    chunk_decay = jnp.exp(log_gamma * RET_CHUNK)
    wide = lambda t: jnp.broadcast_to(t[:, :, None], (h, RET_CHUNK, RET_DIM))
    cdec = jnp.broadcast_to(chunk_decay[:, None, None], (h, 1, RET_DIM))
    return inv_freq[None, :], decay_intra, wide(zeta), wide(xi), cdec


def _retention(pos_col, z):
    invf, dintra, zeta, xi, cdec = _retention_tables()
    nch = SEQ // RET_CHUNK
    full = lambda a: pl.BlockSpec(a.shape, lambda b, n: (0,) * a.ndim)
    return pl.pallas_call(
        _ret_kernel,
        grid=(BATCH, nch),
        in_specs=[pl.BlockSpec((RET_CHUNK, 1), lambda b, n: (b * nch + n, 0)),
                  pl.BlockSpec((RET_CHUNK, SEC), lambda b, n: (b * nch + n, 0)),
                  full(invf), full(dintra), full(zeta), full(xi), full(cdec)],
        out_specs=pl.BlockSpec((RET_CHUNK, RET_WIDTH), lambda b, n: (b * nch + n, 0)),
        out_shape=jax.ShapeDtypeStruct((TOKENS, RET_WIDTH), BF16),
        scratch_shapes=[pltpu.VMEM((RET_HEADS, RET_DIM, RET_DIM), F32)],
        compiler_params=_params("arbitrary", "arbitrary"),
        name="retention",
    )(pos_col, z, invf, dintra, zeta, xi, cdec)


def _split2(x):
    hi = x.astype(BF16)
    return hi, (x - hi.astype(F32)).astype(BF16)


def _head_sums(xs, ones_ref):
    c = RWKV_CHUNK
    pieces = []
    for x in xs:
        for p in _split2(x):
            for g in range(RWKV_GROUPS):
                pieces.append(p[:, g * GROUP_LANES:(g + 1) * GROUP_LANES])
    res = jnp.dot(jnp.concatenate(pieces, axis=0), ones_ref[...], preferred_element_type=F32)
    outs = []
    for i in range(len(xs)):
        base = i * 2 * RWKV_GROUPS * c
        cols = []
        for g in range(RWKV_GROUPS):
            hi = res[base + g * c:base + (g + 1) * c]
            lo = res[base + (RWKV_GROUPS + g) * c:base + (RWKV_GROUPS + g + 1) * c]
            cols.append(hi + lo)
        outs.append(jnp.concatenate(cols, axis=-1))
    return outs


def _rwkv_kernel(z_ref, mu_ref, par_ref, wd_ref, wa_ref, wg_ref, ltri_ref, ones_ref, mstrict_ref, mincl_ref,
                 o_ref, s_ref, carry_ref):
    c = RWKV_CHUNK
    w = RWKV_WIDTH

    @pl.when(pl.program_id(1) == 0)
    def _():
        s_ref[...] = jnp.zeros_like(s_ref)
        carry_ref[...] = jnp.zeros_like(carry_ref)

    z = z_ref[...]
    row = lax.broadcasted_iota(jnp.int32, z.shape, 0)
    prev = jnp.where(row == 0, carry_ref[...], pltpu.roll(z, 1, 0))
    carry_ref[...] = z_ref[c - 1:c, :]
    zs = z + (prev - z) * mu_ref[...]

    r, kw, vw = zs[:, :w], zs[:, w:2 * w], zs[:, 2 * w:3 * w]
    zw = zs[:, 3 * w:3 * w + LORA_PAD]
    za = zs[:, 3 * w + LORA_PAD:3 * w + 2 * LORA_PAD]
    zg = zs[:, 3 * w + 2 * LORA_PAD:3 * w + 2 * LORA_PAD + GATE_LORA]
    par = par_ref[...]
    w0, a0, k_k, k_a, r_k, lnx_w, lnx_b = (par[i:i + 1] for i in range(7))

    dec_pre = w0 + _bdot(jnp.tanh(zw), wd_ref[...])
    softplus = jnp.maximum(-dec_pre, 0.0) + jnp.log1p(jnp.exp(-jnp.abs(dec_pre)))
    logdec = -jnp.exp(-softplus - 0.5)
    iclr = jax.nn.sigmoid(a0 + _bdot(za, wa_ref[...]))
    g_rw = _bdot(jax.nn.sigmoid(zg), wg_ref[...])

    kk = kw * k_k
    k_mod = kw * (1.0 + (iclr - 1.0) * k_a)
    kk_sq, rk_sum = _head_sums([kk * kk, r * k_mod * r_k], ones_ref)
    kk = kk * (1.0 / jnp.maximum(jnp.sqrt(kk_sq), 1e-12))
    a = -kk
    b = kk * iclr
    bonus = rk_sum * vw

    ld_hi = logdec.astype(BF16)
    ld_mid, ld_lo = _split2(logdec - ld_hi.astype(F32))
    ltri = ltri_ref[...]
    cum = (jnp.dot(ltri, ld_hi, preferred_element_type=F32) + jnp.dot(ltri, ld_mid, preferred_element_type=F32)
           + jnp.dot(ltri, ld_lo, preferred_element_type=F32))
    cum_end = cum[c - 1:c, :]
    p_in = jnp.exp(cum)
    p_inv = jnp.exp(-cum)
    p_end = jnp.exp(cum_end - cum)
    a_t = a * jnp.exp(cum - logdec)
    b_t = (b * p_inv).astype(BF16)
    k_t = (k_mod * p_inv).astype(BF16)
    r_t = r * p_in
    b_e = b * p_end
    k_e = k_mod * p_end
    decay_end = jnp.exp(cum_end)

    lane_head = lax.shift_right_logical(lax.broadcasted_iota(jnp.int32, (c, GROUP_LANES), 1), 6)
    head_masks = [lane_head == h for h in range(GROUP_HEADS)]

    def stack(x):
        return jnp.concatenate([jnp.where(m, x, 0.0).astype(BF16) for m in head_masks], axis=0)

    def rep(x):
        return jnp.concatenate([x] * GROUP_HEADS, axis=0)

    m_strict = mstrict_ref[...]
    m_incl = mincl_ref[...]
    eye = (lax.broadcasted_iota(jnp.int32, (GROUP_LANES, GROUP_LANES), 0)
           == lax.broadcasted_iota(jnp.int32, (GROUP_LANES, GROUP_LANES), 1)).astype(F32)
    n4 = GROUP_HEADS * c
    groups = range(RWKV_GROUPS)
    sls = [slice(g * GROUP_LANES, (g + 1) * GROUP_LANES) for g in groups]
    a_s = [stack(a_t[:, sl]) for sl in sls]
    r_s = [stack(r_t[:, sl]) for sl in sls]
    v_s = [stack(vw[:, sl]) for sl in sls]
    be_s = [stack(b_e[:, sl]) for sl in sls]
    ke_s = [stack(k_e[:, sl]) for sl in sls]
    blk = [_bdot_nt(jnp.concatenate([a_s[g], r_s[g]], axis=0),
                    jnp.concatenate([rep(b_t[:, sls[g]]), rep(k_t[:, sls[g]])], axis=0)) for g in groups]
    a_ab = [blk[g][:n4, :n4] * m_strict for g in groups]
    a_ak = [blk[g][:n4, n4:] * m_strict for g in groups]
    a_rb = [blk[g][n4:, :n4] * m_incl for g in groups]
    a_rk = [blk[g][n4:, n4:] * m_incl for g in groups]
    t_inv = [eye + a_ab[g] for g in groups]
    a_pow = a_ab
    for _ in range(5):
        a_pow = [_bdot(a_pow[g], a_pow[g]) for g in groups]
        t_inv = [t_inv[g] + _bdot(t_inv[g], a_pow[g]) for g in groups]
    a_eff = [_bdot(t_inv[g], a_s[g]) for g in groups]
    av = [_bdot(a_ak[g], v_s[g]) for g in groups]
    u_const = [_bdot(t_inv[g], av[g]) for g in groups]
    state = [s_ref[g] for g in groups]
    sb = [state[g].astype(BF16) for g in groups]
    u = [_bdot_nt(a_eff[g], sb[g]) + u_const[g] for g in groups]
    y_st = [_bdot_nt(r_s[g], sb[g]) + _bdot(a_rb[g], u[g]) + _bdot(a_rk[g], v_s[g]) for g in groups]
    for g in groups:
        s_ref[g] = (state[g] * decay_end[:, sls[g]]
                    + _bdot_tn(jnp.concatenate([u[g].astype(BF16), v_s[g]], axis=0),
                               jnp.concatenate([be_s[g], ke_s[g]], axis=0)))
    y = jnp.concatenate([t[:c] + t[c:2 * c] + t[2 * c:3 * c] + t[3 * c:] for t in y_st], axis=-1)

    (y_sum,) = _head_sums([y], ones_ref)
    d = y - y_sum * (1.0 / RWKV_HEAD)
    (d_sq,) = _head_sums([d * d], ones_ref)
    y = d * lax.rsqrt(d_sq * (1.0 / RWKV_HEAD) + GN_EPS_RWKV) * lnx_w + lnx_b + bonus
    o_ref[...] = (y * g_rw).astype(BF16)


def _rwkv_tables():
    c = RWKV_CHUNK
    ltri = (jnp.arange(c)[:, None] >= jnp.arange(c)[None, :]).astype(BF16)
    row = jnp.arange(GROUP_LANES)[:, None]
    col = jnp.arange(GROUP_LANES)[None, :]
    same_head = (row // RWKV_HEAD) == (col // RWKV_HEAD)
    ones = same_head.astype(BF16)
    m_strict = (same_head & (row > col)).astype(F32)
    m_incl = (same_head & (row >= col)).astype(F32)
    return ltri, ones, m_strict, m_incl


def _rwkv(z, mu_p, par, wd, wa, wg):
    assert RWKV_CHUNK == RWKV_HEAD
    ltri, ones, m_strict, m_incl = _rwkv_tables()
    nch = SEQ // RWKV_CHUNK
    full = lambda a: pl.BlockSpec(a.shape, lambda b, n: (0,) * a.ndim)
    return pl.pallas_call(
        _rwkv_kernel,
        grid=(BATCH, nch),
        in_specs=[pl.BlockSpec((RWKV_CHUNK, SEC), lambda b, n: (b * nch + n, 2)),
                  full(mu_p), full(par), full(wd), full(wa), full(wg),
                  full(ltri), full(ones), full(m_strict), full(m_incl)],
        out_specs=pl.BlockSpec((RWKV_CHUNK, RWKV_WIDTH), lambda b, n: (b * nch + n, 0)),
        out_shape=jax.ShapeDtypeStruct((TOKENS, RWKV_WIDTH), BF16),
        scratch_shapes=[pltpu.VMEM((RWKV_GROUPS, GROUP_LANES, GROUP_LANES), F32),
                        pltpu.VMEM((1, SEC), F32)],
        compiler_params=_params("arbitrary", "arbitrary"),
        name="rwkv",
    )(z, mu_p, par, wd, wa, wg, ltri, ones, m_strict, m_incl)


MG_TM = 1024
MG_TN = 1024


def _merge_kernel(yr_ref, yw_ref, wr_ref, ww_ref, gr_ref, gw_ref, br_ref, bw_ref, o_ref):
    pr = jnp.dot(yr_ref[...], wr_ref[...], preferred_element_type=F32)
    pw = jnp.dot(yw_ref[...], ww_ref[...], preferred_element_type=F32)
    o = jax.nn.sigmoid(gr_ref[...] + br_ref[...]) * pr + jax.nn.sigmoid(gw_ref[...] + bw_ref[...]) * pw
    o_ref[...] = o.astype(BF16)


def _merge(y_ret, y_rw, w_ret, w_rw, z, b_gate):
    g0 = SEC // MG_TN
    nd = D_MODEL // MG_TN
    return pl.pallas_call(
        _merge_kernel,
        grid=(TOKENS // MG_TM, nd),
        in_specs=[pl.BlockSpec((MG_TM, RET_WIDTH), lambda i, j: (i, 0)),
                  pl.BlockSpec((MG_TM, RWKV_WIDTH), lambda i, j: (i, 0)),
                  pl.BlockSpec((RET_WIDTH, MG_TN), lambda i, j: (0, j)),
                  pl.BlockSpec((RWKV_WIDTH, MG_TN), lambda i, j: (0, j)),
                  pl.BlockSpec((MG_TM, MG_TN), lambda i, j: (i, g0 + j)),
                  pl.BlockSpec((MG_TM, MG_TN), lambda i, j: (i, g0 + nd + j)),
                  pl.BlockSpec((1, MG_TN), lambda i, j: (0, j)),
                  pl.BlockSpec((1, MG_TN), lambda i, j: (0, nd + j))],
        out_specs=pl.BlockSpec((MG_TM, MG_TN), lambda i, j: (i, j)),
        out_shape=jax.ShapeDtypeStruct((TOKENS, D_MODEL), BF16),
        compiler_params=_params("arbitrary", "arbitrary"),
        name="merge",
    )(y_ret, y_rw, w_ret, w_rw, z, z, b_gate, b_gate)


OP_TM = 512


def _oproj_kernel(m_ref, w_ref, x_ref, mod_ref, gain_ref, h_ref, u_ref):
    mod = mod_ref[0]
    h = x_ref[...] + mod[2:3] * jnp.dot(m_ref[...], w_ref[...], preferred_element_type=F32)
    h_ref[...] = h
    u_ref[...] = _modulated_norm(h, gain_ref[...], mod[4:5], mod[3:4]).astype(BF16)


def _oproj(merged, w_o, x2, mod3, gain):
    blocks_per_batch = SEQ // OP_TM
    return pl.pallas_call(
        _oproj_kernel,
        grid=(TOKENS // OP_TM,),
        in_specs=[pl.BlockSpec((OP_TM, D_MODEL), lambda i: (i, 0)),
                  pl.BlockSpec((D_MODEL, D_MODEL), lambda i: (0, 0)),
                  pl.BlockSpec((OP_TM, D_MODEL), lambda i: (i, 0)),
                  pl.BlockSpec((1, 6, D_MODEL), lambda i: (i // blocks_per_batch, 0, 0)),
                  pl.BlockSpec((1, D_MODEL), lambda i: (0, 0))],
        out_specs=[pl.BlockSpec((OP_TM, D_MODEL), lambda i: (i, 0)),
                   pl.BlockSpec((OP_TM, D_MODEL), lambda i: (i, 0))],
        out_shape=[jax.ShapeDtypeStruct((TOKENS, D_MODEL), F32),
                   jax.ShapeDtypeStruct((TOKENS, D_MODEL), BF16)],
        compiler_params=_params("arbitrary"),
        name="oproj",
    )(merged, w_o, x2, mod3, gain)


FI_TM = 1024
FI_TN = 512


def _ffn_in_kernel(u_ref, wg_ref, wu_ref, o_ref):
    u = u_ref[...]
    gate = jnp.dot(u, wg_ref[...], preferred_element_type=F32)
    up = jnp.dot(u, wu_ref[...], preferred_element_type=F32)
    o_ref[...] = (_silu(gate) * up).astype(BF16)


def _ffn_in(u2, w_in):
    nt = FFN_HIDDEN // FI_TN
    return pl.pallas_call(
        _ffn_in_kernel,
        grid=(TOKENS // FI_TM, nt),
        in_specs=[pl.BlockSpec((FI_TM, D_MODEL), lambda i, j: (i, 0)),
                  pl.BlockSpec((D_MODEL, FI_TN), lambda i, j: (0, j)),
                  pl.BlockSpec((D_MODEL, FI_TN), lambda i, j: (0, nt + j))],
        out_specs=pl.BlockSpec((FI_TM, FI_TN), lambda i, j: (i, j)),
        out_shape=jax.ShapeDtypeStruct((TOKENS, FFN_HIDDEN), BF16),
        compiler_params=_params("arbitrary", "arbitrary"),
        name="ffn_in",
    )(u2, w_in, w_in)


FO_TM = 512
FO_TK = 512


def _ffn_out_kernel(a_ref, w_ref, h_ref, mod_ref, gain_ref, o_ref, acc_ref):
    k = pl.program_id(1)

    @pl.when(k == 0)
    def _():
        acc_ref[...] = jnp.zeros_like(acc_ref)

    acc_ref[...] += jnp.dot(a_ref[...], w_ref[...], preferred_element_type=F32)

    @pl.when(k == pl.num_programs(1) - 1)
    def _():
        h = h_ref[...] + mod_ref[0][5:6] * acc_ref[...]
        o_ref[...] = h * lax.rsqrt(jnp.mean(h * h, axis=-1, keepdims=True) + NORM_EPS) * gain_ref[...]


def _ffn_out(act, w_out, h1, mod3, gain):
    blocks_per_batch = SEQ // FO_TM
    return pl.pallas_call(
        _ffn_out_kernel,
        grid=(TOKENS // FO_TM, FFN_HIDDEN // FO_TK),
        in_specs=[pl.BlockSpec((FO_TM, FO_TK), lambda i, k: (i, k)),
                  pl.BlockSpec((FO_TK, D_MODEL), lambda i, k: (k, 0)),
                  pl.BlockSpec((FO_TM, D_MODEL), lambda i, k: (i, 0)),
                  pl.BlockSpec((1, 6, D_MODEL), lambda i, k: (i // blocks_per_batch, 0, 0)),
                  pl.BlockSpec((1, D_MODEL), lambda i, k: (0, 0))],
        out_specs=pl.BlockSpec((FO_TM, D_MODEL), lambda i, k: (i, 0)),
        out_shape=jax.ShapeDtypeStruct((TOKENS, D_MODEL), F32),
        scratch_shapes=[pltpu.VMEM((FO_TM, D_MODEL), F32)],
        compiler_params=_params("arbitrary", "arbitrary"),
        name="ffn_out",
    )(act, w_out, h1, mod3, gain)


def _pad_cols(t, width):
    return jnp.pad(t, ((0, 0), (0, width - t.shape[1])))


def _pack_w_in(w_in):
    o3 = RET_COLS + 3 * RWKV_WIDTH
    o4 = o3 + DECAY_LORA
    o5 = o4 + ICLR_LORA
    o6 = RET_COLS + SHIFT_COLS
    rw = jnp.concatenate([w_in[:, RET_COLS:o3], _pad_cols(w_in[:, o3:o4], LORA_PAD),
                          _pad_cols(w_in[:, o4:o5], LORA_PAD), w_in[:, o5:o6]], axis=1)
    return jnp.concatenate([w_in[:, :RET_COLS], w_in[:, o6:], _pad_cols(rw, SEC)], axis=1).astype(BF16)


def _pack_mu(mu):
    o3 = 3 * RWKV_WIDTH
    o4 = o3 + DECAY_LORA
    o5 = o4 + ICLR_LORA
    mu = mu[None, :]
    rw = jnp.concatenate([mu[:, :o3], _pad_cols(mu[:, o3:o4], LORA_PAD), _pad_cols(mu[:, o4:o5], LORA_PAD),
                          mu[:, o5:]], axis=1)
    return _pad_cols(rw, SEC)


def _pad_rows(t, rows):
    return jnp.pad(t, ((0, rows - t.shape[0]), (0, 0)))


def kernel(x, c, positions, w_ada, b_ada, norm_mix, norm_ffn, norm_final, w_in, b_gate, mu_shift, w0, w_decay_up, a0, w_iclr_up, w_gate_up, k_k, k_a, r_k, lnx_w, lnx_b, w_ret_out, w_rwkv_out, w_o, w_ffn_in, w_ffn_out):
    assert x.shape == (BATCH, SEQ, D_MODEL) and w_ada.shape[0] == 1
    x2 = x.reshape(TOKENS, D_MODEL)
    pos_col = positions.reshape(TOKENS, 1)

    mod = _ada(_pad_rows(c, 8), w_ada[0], b_ada)
    mod3 = mod[:BATCH].reshape(BATCH, 6, D_MODEL)

    z = _inproj(x2, mod3, norm_mix, _pack_w_in(w_in[0]))
    y_ret = _retention(pos_col, z)

    par = jnp.concatenate([w0, a0, k_k, k_a, r_k.reshape(1, RWKV_WIDTH), lnx_w, lnx_b,
                           jnp.zeros((1, RWKV_WIDTH), F32)], axis=0)
    y_rw = _rwkv(z, _pack_mu(mu_shift[0]), par,
                 _pad_rows(w_decay_up[0], LORA_PAD).astype(BF16), _pad_rows(w_iclr_up[0], LORA_PAD).astype(BF16),
                 w_gate_up[0].astype(BF16))

    merged = _merge(y_ret, y_rw, w_ret_out[0].astype(BF16), w_rwkv_out[0].astype(BF16), z, b_gate)
    h1, u2 = _oproj(merged, w_o[0].astype(BF16), x2, mod3, norm_ffn)
    act = _ffn_in(u2, w_ffn_in[0].astype(BF16))
    out = _ffn_out(act, w_ffn_out[0].astype(BF16), h1, mod3, norm_final[None, :])
    return out.reshape(BATCH, SEQ, D_MODEL)
```

```python
import functools

import jax
import jax.numpy as jnp
from jax import lax
from jax.experimental import pallas as pl
from jax.experimental.pallas import tpu as pltpu

F32 = jnp.float32
BF16 = jnp.bfloat16

D_MODEL = 2048
BATCH = 2
SEQ = 4096
TOKENS = BATCH * SEQ

RET_HEADS = 4
RET_DIM = 256
RET_WIDTH = RET_HEADS * RET_DIM
RET_CHUNK = 128
ROPE_BASE = 10000.0

RWKV_HEAD = 64
RWKV_WIDTH = 1024
RWKV_HEADS = RWKV_WIDTH // RWKV_HEAD
DECAY_LORA = 96
ICLR_LORA = 96
GATE_LORA = 256
RWKV_CHUNK = 64
GROUP_LANES = 256
GROUP_HEADS = GROUP_LANES // RWKV_HEAD
RWKV_GROUPS = RWKV_WIDTH // GROUP_LANES
LORA_PAD = 128

FFN_HIDDEN = ((8 * D_MODEL // 3 + 255) // 256) * 256
RET_COLS = 4 * RET_WIDTH
SHIFT_COLS = 3 * RWKV_WIDTH + DECAY_LORA + ICLR_LORA + GATE_LORA
GATE_COLS = 2 * D_MODEL
SEC = 4096
Z_COLS = 3 * SEC
HEAD_COLS = RET_COLS + 3 * RWKV_WIDTH
TAIL_COLS = Z_COLS - HEAD_COLS
NORM_EPS = 1e-6
GN_EPS_RET = 1e-5
GN_EPS_RWKV = 64e-5

VMEM_LIMIT = 56 * 1024 * 1024


def _params(*sem, vmem=VMEM_LIMIT):
    return pltpu.CompilerParams(dimension_semantics=sem, vmem_limit_bytes=vmem)


def _bdot(a, b):
    return jnp.dot(a.astype(BF16), b.astype(BF16), preferred_element_type=F32)


def _bdot_nt(a, b):
    return lax.dot_general(a.astype(BF16), b.astype(BF16), (((1,), (1,)), ((), ())),
                           preferred_element_type=F32)


def _bdot_tn(a, b):
    return lax.dot_general(a.astype(BF16), b.astype(BF16), (((0,), (0,)), ((), ())),
                           preferred_element_type=F32)


def _silu(x):
    return x * jax.nn.sigmoid(x)


ADA_TN = 1024


def _ada_kernel(c_ref, w_ref, b_ref, o_ref):
    o_ref[...] = _bdot(_silu(c_ref[...]), w_ref[...]) + b_ref[...]


def _ada(c_pad, w_ada, b_ada):
    n = w_ada.shape[1]
    return pl.pallas_call(
        _ada_kernel,
        grid=(n // ADA_TN,),
        in_specs=[pl.BlockSpec((8, D_MODEL), lambda j: (0, 0)),
                  pl.BlockSpec((D_MODEL, ADA_TN), lambda j: (0, j)),
                  pl.BlockSpec((1, ADA_TN), lambda j: (0, j))],
        out_specs=pl.BlockSpec((8, ADA_TN), lambda j: (0, j)),
        out_shape=jax.ShapeDtypeStruct((8, n), F32),
        compiler_params=_params("arbitrary"),
        name="ada",
    )(c_pad, w_ada, b_ada)


IN_TM = 1024
IN_TN = 512
IN_HEAD_BLOCKS = HEAD_COLS // IN_TN


def _modulated_norm(x, gain, scale, shift):
    y = x * lax.rsqrt(jnp.mean(x * x, axis=-1, keepdims=True) + NORM_EPS)
    return y * gain * (1.0 + scale) + shift


def _inproj_kernel(x_ref, mod_ref, gain_ref, w_ref, wt_ref, o_ref, u_ref):
    j = pl.program_id(1)

    @pl.when(j == 0)
    def _():
        mod = mod_ref[0]
        u_ref[...] = _modulated_norm(x_ref[...], gain_ref[...], mod[1:2], mod[0:1]).astype(BF16)

    @pl.when(j < IN_HEAD_BLOCKS)
    def _():
        o_ref[...] = jnp.dot(u_ref[...], w_ref[...].astype(BF16), preferred_element_type=F32)

    @pl.when(j >= IN_HEAD_BLOCKS)
    def _():
        o_ref[...] = jnp.dot(u_ref[...], wt_ref[...], preferred_element_type=F32)


def _inproj(x2, mod3, gain, w_in, w_tail):
    blocks_per_batch = SEQ // IN_TM
    return pl.pallas_call(
        _inproj_kernel,
        grid=(TOKENS // IN_TM, Z_COLS // IN_TN),
        in_specs=[pl.BlockSpec((IN_TM, D_MODEL), lambda i, j: (i, 0)),
                  pl.BlockSpec((1, 6, D_MODEL), lambda i, j: (i // blocks_per_batch, 0, 0)),
                  pl.BlockSpec((1, D_MODEL), lambda i, j: (0, 0)),
                  pl.BlockSpec((D_MODEL, IN_TN), lambda i, j: (0, jnp.minimum(j, IN_HEAD_BLOCKS - 1))),
                  pl.BlockSpec((D_MODEL, IN_TN), lambda i, j: (0, jnp.maximum(j - IN_HEAD_BLOCKS, 0)))],
        out_specs=pl.BlockSpec((IN_TM, IN_TN), lambda i, j: (i, j)),
        out_shape=jax.ShapeDtypeStruct((TOKENS, Z_COLS), F32),
        scratch_shapes=[pltpu.VMEM((IN_TM, D_MODEL), BF16)],
        compiler_params=_params("arbitrary", "arbitrary"),
        name="inproj",
    )(x2, mod3, gain, w_in, w_tail)


def _ret_kernel(pos_ref, z_ref, invf_ref, dintra_ref, zeta_ref, xi_ref, cdec_ref, o_ref, state_ref):
    @pl.when(pl.program_id(1) == 0)
    def _():
        state_ref[...] = jnp.zeros_like(state_ref)

    ang = pos_ref[...].astype(F32) * invf_ref[...]
    cos = jnp.cos(ang)
    sin = jnp.sin(ang)
    half = RET_DIM // 2

    def rot(t):
        t1, t2 = t[:, :half], t[:, half:]
        return jnp.concatenate([t1 * cos - t2 * sin, t1 * sin + t2 * cos], axis=-1)

    for h in range(RET_HEADS):
        lo, hi = h * RET_DIM, (h + 1) * RET_DIM
        q = rot(z_ref[:, lo:hi])
        k = rot(z_ref[:, RET_WIDTH + lo:RET_WIDTH + hi]) * (RET_DIM ** -0.5)
        v = z_ref[:, 2 * RET_WIDTH + lo:2 * RET_WIDTH + hi].astype(BF16)
        g = z_ref[:, 3 * RET_WIDTH + lo:3 * RET_WIDTH + hi]
        qb = q.astype(BF16)
        scores = _bdot_nt(qb, k) * dintra_ref[h]
        state = state_ref[h]
        out = _bdot(scores, v) + _bdot(qb, state) * xi_ref[h]
        state_ref[h] = state * cdec_ref[h] + _bdot_tn(k * zeta_ref[h], v)
        mu = jnp.mean(out, axis=-1, keepdims=True)
        d = out - mu
        var = jnp.mean(d * d, axis=-1, keepdims=True)
        o_ref[:, lo:hi] = (_silu(g) * (d * lax.rsqrt(var + GN_EPS_RET))).astype(BF16)


def _retention_tables():
    h = RET_HEADS
    half = RET_DIM // 2
    inv_freq = ROPE_BASE ** (-jnp.arange(half, dtype=F32) / half)
    log_gamma = jnp.log(1.0 - 2.0 ** (-5.0 - jnp.arange(h, dtype=F32)))
    idx = jnp.arange(RET_CHUNK, dtype=F32)
    dist = idx[:, None] - idx[None, :]
    decay_intra = jnp.where(dist >= 0, jnp.exp(log_gamma[:, None, None] * jnp.maximum(dist, 0.0)), 0.0)
    zeta = jnp.exp(log_gamma[:, None] * (RET_CHUNK - 1.0 - idx))
    xi = jnp.exp(log_gamma[:, None] * (idx + 1.0))
    chunk_decay = jnp.exp(log_gamma * RET_CHUNK)
    wide = lambda t: jnp.broadcast_to(t[:, :, None], (h, RET_CHUNK, RET_DIM))
    cdec = jnp.broadcast_to(chunk_decay[:, None, None], (h, 1, RET_DIM))
    return inv_freq[None, :], decay_intra, wide(zeta), wide(xi), cdec


def _retention(pos_col, z):
    invf, dintra, zeta, xi, cdec = _retention_tables()
    nch = SEQ // RET_CHUNK
    full = lambda a: pl.BlockSpec(a.shape, lambda b, n: (0,) * a.ndim)
    return pl.pallas_call(
        _ret_kernel,
        grid=(BATCH, nch),
        in_specs=[pl.BlockSpec((RET_CHUNK, 1), lambda b, n: (b * nch + n, 0)),
                  pl.BlockSpec((RET_CHUNK, SEC), lambda b, n: (b * nch + n, 0)),
                  full(invf), full(dintra), full(zeta), full(xi), full(cdec)],
        out_specs=pl.BlockSpec((RET_CHUNK, RET_WIDTH), lambda b, n: (b * nch + n, 0)),
        out_shape=jax.ShapeDtypeStruct((TOKENS, RET_WIDTH), BF16),
        scratch_shapes=[pltpu.VMEM((RET_HEADS, RET_DIM, RET_DIM), F32)],
        compiler_params=_params("arbitrary", "arbitrary"),
        name="retention",
    )(pos_col, z, invf, dintra, zeta, xi, cdec)


def _split2(x):
    hi = x.astype(BF16)
    return hi, (x - hi.astype(F32)).astype(BF16)


def _head_sums(xs, ones_ref):
    c = RWKV_CHUNK
    pieces = []
    for x in xs:
        for p in _split2(x):
            for g in range(RWKV_GROUPS):
                pieces.append(p[:, g * GROUP_LANES:(g + 1) * GROUP_LANES])
    res = jnp.dot(jnp.concatenate(pieces, axis=0), ones_ref[...], preferred_element_type=F32)
    outs = []
    for i in range(len(xs)):
        base = i * 2 * RWKV_GROUPS * c
        cols = []
        for g in range(RWKV_GROUPS):
            hi = res[base + g * c:base + (g + 1) * c]
            lo = res[base + (RWKV_GROUPS + g) * c:base + (RWKV_GROUPS + g + 1) * c]
            cols.append(hi + lo)
        outs.append(jnp.concatenate(cols, axis=-1))
    return outs


def _rwkv_kernel(z_ref, mu_ref, par_ref, wd_ref, wa_ref, wg_ref, ltri_ref, ones_ref, mstrict_ref, mincl_ref,
                 o_ref, s_ref, carry_ref):
    c = RWKV_CHUNK
    w = RWKV_WIDTH

    @pl.when(pl.program_id(1) == 0)
    def _():
        s_ref[...] = jnp.zeros_like(s_ref)
        carry_ref[...] = jnp.zeros_like(carry_ref)

    z = z_ref[...]
    row = lax.broadcasted_iota(jnp.int32, z.shape, 0)
    prev = jnp.where(row == 0, carry_ref[...], pltpu.roll(z, 1, 0))
    carry_ref[...] = z_ref[c - 1:c, :]
    zs = z + (prev - z) * mu_ref[...]

    r, kw, vw = zs[:, :w], zs[:, w:2 * w], zs[:, 2 * w:3 * w]
    zw = zs[:, 3 * w:3 * w + LORA_PAD]
    za = zs[:, 3 * w + LORA_PAD:3 * w + 2 * LORA_PAD]
    zg = zs[:, 3 * w + 2 * LORA_PAD:3 * w + 2 * LORA_PAD + GATE_LORA]
    par = par_ref[...]
    w0, a0, k_k, k_a, r_k, lnx_w, lnx_b = (par[i:i + 1] for i in range(7))

    dec_pre = w0 + _bdot(jnp.tanh(zw), wd_ref[...])
    softplus = jnp.maximum(-dec_pre, 0.0) + jnp.log1p(jnp.exp(-jnp.abs(dec_pre)))
    logdec = -jnp.exp(-softplus - 0.5)
    iclr = jax.nn.sigmoid(a0 + _bdot(za, wa_ref[...]))
    g_rw = _bdot(jax.nn.sigmoid(zg), wg_ref[...])

    kk = kw * k_k
    k_mod = kw * (1.0 + (iclr - 1.0) * k_a)
    kk_sq, rk_sum = _head_sums([kk * kk, r * k_mod * r_k], ones_ref)
    kk = kk * (1.0 / jnp.maximum(jnp.sqrt(kk_sq), 1e-12))
    a = -kk
    b = kk * iclr
    bonus = rk_sum * vw

    ld_hi = logdec.astype(BF16)
    ld_mid, ld_lo = _split2(logdec - ld_hi.astype(F32))
    ltri = ltri_ref[...]
    cum = (jnp.dot(ltri, ld_hi, preferred_element_type=F32) + jnp.dot(ltri, ld_mid, preferred_element_type=F32)
           + jnp.dot(ltri, ld_lo, preferred_element_type=F32))
    cum_end = cum[c - 1:c, :]
    p_in = jnp.exp(cum)
    p_inv = jnp.exp(-cum)
    p_end = jnp.exp(cum_end - cum)
    a_t = a * jnp.exp(cum - logdec)
    b_t = (b * p_inv).astype(BF16)
    k_t = (k_mod * p_inv).astype(BF16)
    r_t = r * p_in
    b_e = b * p_end
    k_e = k_mod * p_end
    decay_end = jnp.exp(cum_end)

    lane_head = lax.shift_right_logical(lax.broadcasted_iota(jnp.int32, (c, GROUP_LANES), 1), 6)
    head_masks = [lane_head == h for h in range(GROUP_HEADS)]

    def stack(x):
        return jnp.concatenate([jnp.where(m, x, 0.0).astype(BF16) for m in head_masks], axis=0)

    def rep(x):
        return jnp.concatenate([x] * GROUP_HEADS, axis=0)

    m_strict = mstrict_ref[...]
    m_incl = mincl_ref[...]
    eye = (lax.broadcasted_iota(jnp.int32, (GROUP_LANES, GROUP_LANES), 0)
           == lax.broadcasted_iota(jnp.int32, (GROUP_LANES, GROUP_LANES), 1)).astype(F32)
    n4 = GROUP_HEADS * c
    groups = range(RWKV_GROUPS)
    sls = [slice(g * GROUP_LANES, (g + 1) * GROUP_LANES) for g in groups]
    a_s = [stack(a_t[:, sl]) for sl in sls]
    r_s = [stack(r_t[:, sl]) for sl in sls]
    v_s = [stack(vw[:, sl]) for sl in sls]
    be_s = [stack(b_e[:, sl]) for sl in sls]
    ke_s = [stack(k_e[:, sl]) for sl in sls]
    blk = [_bdot_nt(jnp.concatenate([a_s[g], r_s[g]], axis=0),
                    jnp.concatenate([rep(b_t[:, sls[g]]), rep(k_t[:, sls[g]])], axis=0)) for g in groups]
    a_ab = [blk[g][:n4, :n4] * m_strict for g in groups]
    a_ak = [blk[g][:n4, n4:] * m_strict for g in groups]
    a_rb = [blk[g][n4:, :n4] * m_incl for g in groups]
    a_rk = [blk[g][n4:, n4:] * m_incl for g in groups]
    t_inv = [eye + a_ab[g] for g in groups]
    a_pow = a_ab
    for _ in range(5):
        a_pow = [_bdot(a_pow[g], a_pow[g]) for g in groups]
        t_inv = [t_inv[g] + _bdot(t_inv[g], a_pow[g]) for g in groups]
    a_eff = [_bdot(t_inv[g], a_s[g]) for g in groups]
    av = [_bdot(a_ak[g], v_s[g]) for g in groups]
    u_const = [_bdot(t_inv[g], av[g]) for g in groups]
    state = [s_ref[g] for g in groups]
    sb = [state[g].astype(BF16) for g in groups]
    u = [_bdot_nt(a_eff[g], sb[g]) + u_const[g] for g in groups]
    y_st = [_bdot_nt(r_s[g], sb[g]) + _bdot(a_rb[g], u[g]) + _bdot(a_rk[g], v_s[g]) for g in groups]
    for g in groups:
        s_ref[g] = (state[g] * decay_end[:, sls[g]]
                    + _bdot_tn(jnp.concatenate([u[g].astype(BF16), v_s[g]], axis=0),
                               jnp.concatenate([be_s[g], ke_s[g]], axis=0)))
    y = jnp.concatenate([t[:c] + t[c:2 * c] + t[2 * c:3 * c] + t[3 * c:] for t in y_st], axis=-1)

    (y_sum,) = _head_sums([y], ones_ref)
    d = y - y_sum * (1.0 / RWKV_HEAD)
    (d_sq,) = _head_sums([d * d], ones_ref)
    y = d * lax.rsqrt(d_sq * (1.0 / RWKV_HEAD) + GN_EPS_RWKV) * lnx_w + lnx_b + bonus
    o_ref[...] = (y * g_rw).astype(BF16)


def _rwkv_tables():
    c = RWKV_CHUNK
    ltri = (jnp.arange(c)[:, None] >= jnp.arange(c)[None, :]).astype(BF16)
    row = jnp.arange(GROUP_LANES)[:, None]
    col = jnp.arange(GROUP_LANES)[None, :]
    same_head = (row // RWKV_HEAD) == (col // RWKV_HEAD)
    ones = same_head.astype(BF16)
    m_strict = (same_head & (row > col)).astype(F32)
    m_incl = (same_head & (row >= col)).astype(F32)
    return ltri, ones, m_strict, m_incl


def _rwkv(z, mu_p, par, wd, wa, wg):
    assert RWKV_CHUNK == RWKV_HEAD
    ltri, ones, m_strict, m_incl = _rwkv_tables()
    nch = SEQ // RWKV_CHUNK
    full = lambda a: pl.BlockSpec(a.shape, lambda b, n: (0,) * a.ndim)
    return pl.pallas_call(
        _rwkv_kernel,
        grid=(BATCH, nch),
        in_specs=[pl.BlockSpec((RWKV_CHUNK, SEC), lambda b, n: (b * nch + n, 1)),
                  full(mu_p), full(par), full(wd), full(wa), full(wg),
                  full(ltri), full(ones), full(m_strict), full(m_incl)],
        out_specs=pl.BlockSpec((RWKV_CHUNK, RWKV_WIDTH), lambda b, n: (b * nch + n, 0)),
        out_shape=jax.ShapeDtypeStruct((TOKENS, RWKV_WIDTH), BF16),
        scratch_shapes=[pltpu.VMEM((RWKV_GROUPS, GROUP_LANES, GROUP_LANES), F32),
                        pltpu.VMEM((1, SEC), F32)],
        compiler_params=_params("arbitrary", "arbitrary"),
        name="rwkv",
    )(z, mu_p, par, wd, wa, wg, ltri, ones, m_strict, m_incl)


MG_TM = 512
MG_TN = 1024


def _merge_kernel(yr_ref, yw_ref, wr_ref, ww_ref, gr_ref, gw_ref, br_ref, bw_ref, o_ref, wrb_ref, wwb_ref):
    @pl.when(pl.program_id(1) == 0)
    def _():
        wrb_ref[...] = wr_ref[...].astype(BF16)
        wwb_ref[...] = ww_ref[...].astype(BF16)

    pr = jnp.dot(yr_ref[...], wrb_ref[...], preferred_element_type=F32)
    pw = jnp.dot(yw_ref[...], wwb_ref[...], preferred_element_type=F32)
    o = jax.nn.sigmoid(gr_ref[...] + br_ref[...]) * pr + jax.nn.sigmoid(gw_ref[...] + bw_ref[...]) * pw
    o_ref[...] = o.astype(BF16)


def _merge(y_ret, y_rw, w_ret, w_rw, z, b_gate):
    g0 = 2 * SEC // MG_TN
    nd = D_MODEL // MG_TN
    return pl.pallas_call(
        _merge_kernel,
        grid=(nd, TOKENS // MG_TM),
        in_specs=[pl.BlockSpec((MG_TM, RET_WIDTH), lambda j, i: (i, 0)),
                  pl.BlockSpec((MG_TM, RWKV_WIDTH), lambda j, i: (i, 0)),
                  pl.BlockSpec((RET_WIDTH, MG_TN), lambda j, i: (0, j)),
                  pl.BlockSpec((RWKV_WIDTH, MG_TN), lambda j, i: (0, j)),
                  pl.BlockSpec((MG_TM, MG_TN), lambda j, i: (i, g0 + j)),
                  pl.BlockSpec((MG_TM, MG_TN), lambda j, i: (i, g0 + nd + j)),
                  pl.BlockSpec((1, MG_TN), lambda j, i: (0, j)),
                  pl.BlockSpec((1, MG_TN), lambda j, i: (0, nd + j))],
        out_specs=pl.BlockSpec((MG_TM, MG_TN), lambda j, i: (i, j)),
        out_shape=jax.ShapeDtypeStruct((TOKENS, D_MODEL), BF16),
        scratch_shapes=[pltpu.VMEM((RET_WIDTH, MG_TN), BF16), pltpu.VMEM((RWKV_WIDTH, MG_TN), BF16)],
        compiler_params=_params("arbitrary", "arbitrary"),
        name="merge",
    )(y_ret, y_rw, w_ret, w_rw, z, z, b_gate, b_gate)


OP_TM = 512


def _oproj_kernel(m_ref, w_ref, x_ref, mod_ref, gain_ref, h_ref, u_ref, wb_ref):
    @pl.when(pl.program_id(0) == 0)
    def _():
        wb_ref[...] = w_ref[...].astype(BF16)

    mod = mod_ref[0]
    h = x_ref[...] + mod[2:3] * jnp.dot(m_ref[...], wb_ref[...], preferred_element_type=F32)
    h_ref[...] = h
    u_ref[...] = _modulated_norm(h, gain_ref[...], mod[4:5], mod[3:4]).astype(BF16)


def _oproj(merged, w_o, x2, mod3, gain):
    blocks_per_batch = SEQ // OP_TM
    return pl.pallas_call(
        _oproj_kernel,
        grid=(TOKENS // OP_TM,),
        in_specs=[pl.BlockSpec((OP_TM, D_MODEL), lambda i: (i, 0)),
                  pl.BlockSpec((D_MODEL, D_MODEL), lambda i: (0, 0), pipeline_mode=pl.Buffered(1)),
                  pl.BlockSpec((OP_TM, D_MODEL), lambda i: (i, 0)),
                  pl.BlockSpec((1, 6, D_MODEL), lambda i: (i // blocks_per_batch, 0, 0)),
                  pl.BlockSpec((1, D_MODEL), lambda i: (0, 0))],
        out_specs=[pl.BlockSpec((OP_TM, D_MODEL), lambda i: (i, 0)),
                   pl.BlockSpec((OP_TM, D_MODEL), lambda i: (i, 0))],
        out_shape=[jax.ShapeDtypeStruct((TOKENS, D_MODEL), F32),
                   jax.ShapeDtypeStruct((TOKENS, D_MODEL), BF16)],
        scratch_shapes=[pltpu.VMEM((D_MODEL, D_MODEL), BF16)],
        compiler_params=_params("arbitrary"),
        name="oproj",
    )(merged, w_o, x2, mod3, gain)


FI_TM = 1024
FI_TN = 512


def _ffn_in_kernel(u_ref, wg_ref, wu_ref, o_ref, wgb_ref, wub_ref):
    @pl.when(pl.program_id(1) == 0)
    def _():
        wgb_ref[...] = wg_ref[...].astype(BF16)
        wub_ref[...] = wu_ref[...].astype(BF16)

    u = u_ref[...]
    gate = jnp.dot(u, wgb_ref[...], preferred_element_type=F32)
    up = jnp.dot(u, wub_ref[...], preferred_element_type=F32)
    o_ref[...] = (_silu(gate) * up).astype(BF16)


def _ffn_in(u2, w_in):
    nt = FFN_HIDDEN // FI_TN
    return pl.pallas_call(
        _ffn_in_kernel,
        grid=(nt, TOKENS // FI_TM),
        in_specs=[pl.BlockSpec((FI_TM, D_MODEL), lambda j, i: (i, 0)),
                  pl.BlockSpec((D_MODEL, FI_TN), lambda j, i: (0, j)),
                  pl.BlockSpec((D_MODEL, FI_TN), lambda j, i: (0, nt + j))],
        out_specs=pl.BlockSpec((FI_TM, FI_TN), lambda j, i: (i, j)),
        out_shape=jax.ShapeDtypeStruct((TOKENS, FFN_HIDDEN), BF16),
        scratch_shapes=[pltpu.VMEM((D_MODEL, FI_TN), BF16), pltpu.VMEM((D_MODEL, FI_TN), BF16)],
        compiler_params=_params("arbitrary", "arbitrary"),
        name="ffn_in",
    )(u2, w_in, w_in)


FO_TM = 512
FO_TK = 512


def _ffn_out_kernel(a_ref, w_ref, h_ref, mod_ref, gain_ref, o_ref):
    k = pl.program_id(1)
    part = jnp.dot(a_ref[...], w_ref[...], preferred_element_type=F32)

    @pl.when(k == 0)
    def _():
        o_ref[...] = part

    @pl.when(k > 0)
    def _():
        o_ref[...] += part

    @pl.when(k == pl.num_programs(1) - 1)
    def _():
        h = h_ref[...] + mod_ref[0][5:6] * o_ref[...]
        o_ref[...] = h * lax.rsqrt(jnp.mean(h * h, axis=-1, keepdims=True) + NORM_EPS) * gain_ref[...]


def _ffn_out(act, w_out, h1, mod3, gain):
    blocks_per_batch = SEQ // FO_TM
    return pl.pallas_call(
        _ffn_out_kernel,
        grid=(TOKENS // FO_TM, FFN_HIDDEN // FO_TK),
        in_specs=[pl.BlockSpec((FO_TM, FO_TK), lambda i, k: (i, k)),
                  pl.BlockSpec((FO_TK, D_MODEL), lambda i, k: (k, 0)),
                  pl.BlockSpec((FO_TM, D_MODEL), lambda i, k: (i, 0)),
                  pl.BlockSpec((1, 6, D_MODEL), lambda i, k: (i // blocks_per_batch, 0, 0)),
                  pl.BlockSpec((1, D_MODEL), lambda i, k: (0, 0))],
        out_specs=pl.BlockSpec((FO_TM, D_MODEL), lambda i, k: (i, 0)),
        out_shape=jax.ShapeDtypeStruct((TOKENS, D_MODEL), F32),
        compiler_params=_params("arbitrary", "arbitrary"),
        name="ffn_out",
    )(act, w_out, h1, mod3, gain)


def _pad_cols(t, width):
    return jnp.pad(t, ((0, 0), (0, width - t.shape[1])))


def _pack_w_tail(w_in):
    o3 = HEAD_COLS
    o4 = o3 + DECAY_LORA
    o5 = o4 + ICLR_LORA
    o6 = RET_COLS + SHIFT_COLS
    lora = jnp.concatenate([_pad_cols(w_in[:, o3:o4], LORA_PAD), _pad_cols(w_in[:, o4:o5], LORA_PAD),
                            w_in[:, o5:o6]], axis=1)
    return jnp.concatenate([_pad_cols(lora, 2 * SEC - HEAD_COLS), w_in[:, o6:]], axis=1).astype(BF16)


def _pack_mu(mu):
    o3 = 3 * RWKV_WIDTH
    o4 = o3 + DECAY_LORA
    o5 = o4 + ICLR_LORA
    mu = mu[None, :]
    rw = jnp.concatenate([mu[:, :o3], _pad_cols(mu[:, o3:o4], LORA_PAD), _pad_cols(mu[:, o4:o5], LORA_PAD),
                          mu[:, o5:]], axis=1)
    return _pad_cols(rw, SEC)


def _pad_rows(t, rows):
    return jnp.pad(t, ((0, rows - t.shape[0]), (0, 0)))


def kernel(x, c, positions, w_ada, b_ada, norm_mix, norm_ffn, norm_final, w_in, b_gate, mu_shift, w0, w_decay_up, a0, w_iclr_up, w_gate_up, k_k, k_a, r_k, lnx_w, lnx_b, w_ret_out, w_rwkv_out, w_o, w_ffn_in, w_ffn_out):
    assert x.shape == (BATCH, SEQ, D_MODEL) and w_ada.shape[0] == 1
    x2 = x.reshape(TOKENS, D_MODEL)
    pos_col = positions.reshape(TOKENS, 1)

    mod = _ada(_pad_rows(c, 8), w_ada[0], b_ada)
    mod3 = mod[:BATCH].reshape(BATCH, 6, D_MODEL)

    w_in2 = w_in.reshape(D_MODEL, RET_COLS + SHIFT_COLS + GATE_COLS)
    z = _inproj(x2, mod3, norm_mix, w_in2, _pack_w_tail(w_in2))
    y_ret = _retention(pos_col, z)

    par = jnp.concatenate([w0, a0, k_k, k_a, r_k.reshape(1, RWKV_WIDTH), lnx_w, lnx_b,
                           jnp.zeros((1, RWKV_WIDTH), F32)], axis=0)
    y_rw = _rwkv(z, _pack_mu(mu_shift[0]), par,
                 _pad_rows(w_decay_up[0], LORA_PAD).astype(BF16), _pad_rows(w_iclr_up[0], LORA_PAD).astype(BF16),
                 w_gate_up[0].astype(BF16))

    merged = _merge(y_ret, y_rw, w_ret_out[0], w_rwkv_out[0], z, b_gate)
    h1, u2 = _oproj(merged, w_o[0], x2, mod3, norm_ffn)
    act = _ffn_in(u2, w_ffn_in[0])
    out = _ffn_out(act, w_ffn_out[0].astype(BF16), h1, mod3, norm_final[None, :])
    return out.reshape(BATCH, SEQ, D_MODEL)
```

```python
import functools

import jax
import jax.numpy as jnp
from jax import lax
from jax.experimental import pallas as pl
from jax.experimental.pallas import tpu as pltpu

F32 = jnp.float32
BF16 = jnp.bfloat16

D_MODEL = 2048
BATCH = 2
SEQ = 4096
TOKENS = BATCH * SEQ

RET_HEADS = 4
RET_DIM = 256
RET_WIDTH = RET_HEADS * RET_DIM
RET_CHUNK = 128
ROPE_BASE = 10000.0

RWKV_HEAD = 64
RWKV_WIDTH = 1024
RWKV_HEADS = RWKV_WIDTH // RWKV_HEAD
DECAY_LORA = 96
ICLR_LORA = 96
GATE_LORA = 256
RWKV_CHUNK = 64
GROUP_LANES = 256
GROUP_HEADS = GROUP_LANES // RWKV_HEAD
RWKV_GROUPS = RWKV_WIDTH // GROUP_LANES
LORA_PAD = 128

FFN_HIDDEN = ((8 * D_MODEL // 3 + 255) // 256) * 256
RET_COLS = 4 * RET_WIDTH
SHIFT_COLS = 3 * RWKV_WIDTH + DECAY_LORA + ICLR_LORA + GATE_LORA
GATE_COLS = 2 * D_MODEL
SEC = 4096
Z_COLS = 3 * SEC
HEAD_COLS = RET_COLS + 3 * RWKV_WIDTH
TAIL_COLS = Z_COLS - HEAD_COLS
NORM_EPS = 1e-6
GN_EPS_RET = 1e-5
GN_EPS_RWKV = 64e-5

VMEM_LIMIT = 56 * 1024 * 1024


def _params(*sem, vmem=VMEM_LIMIT):
    return pltpu.CompilerParams(dimension_semantics=sem, vmem_limit_bytes=vmem)


def _bdot(a, b):
    return jnp.dot(a.astype(BF16), b.astype(BF16), preferred_element_type=F32)


def _bdot_nt(a, b):
    return lax.dot_general(a.astype(BF16), b.astype(BF16), (((1,), (1,)), ((), ())),
                           preferred_element_type=F32)


def _bdot_tn(a, b):
    return lax.dot_general(a.astype(BF16), b.astype(BF16), (((0,), (0,)), ((), ())),
                           preferred_element_type=F32)


def _silu(x):
    return x * jax.nn.sigmoid(x)


ADA_TN = 1024


def _ada_kernel(c_ref, w_ref, b_ref, o_ref):
    o_ref[...] = _bdot(_silu(c_ref[...]), w_ref[...]) + b_ref[...]


def _ada(c_pad, w_ada, b_ada):
    n = w_ada.shape[1]
    return pl.pallas_call(
        _ada_kernel,
        grid=(n // ADA_TN,),
        in_specs=[pl.BlockSpec((8, D_MODEL), lambda j: (0, 0)),
                  pl.BlockSpec((D_MODEL, ADA_TN), lambda j: (0, j)),
                  pl.BlockSpec((1, ADA_TN), lambda j: (0, j))],
        out_specs=pl.BlockSpec((8, ADA_TN), lambda j: (0, j)),
        out_shape=jax.ShapeDtypeStruct((8, n), F32),
        compiler_params=_params("arbitrary"),
        name="ada",
    )(c_pad, w_ada, b_ada)


IN_TM = 1024
IN_TN = 1024


def _modulated_norm(x, gain, scale, shift):
    y = x * lax.rsqrt(jnp.mean(x * x, axis=-1, keepdims=True) + NORM_EPS)
    return y * gain * (1.0 + scale) + shift


def _inproj_kernel(x_ref, mod_ref, gain_ref, wt_ref, o_ref, u_ref):
    @pl.when(pl.program_id(1) == 0)
    def _():
        mod = mod_ref[0]
        u_ref[...] = _modulated_norm(x_ref[...], gain_ref[...], mod[1:2], mod[0:1]).astype(BF16)

    o_ref[...] = _bdot_nt(u_ref[...], wt_ref[...])


def _inproj(x2, mod3, gain, w_t):
    blocks_per_batch = SEQ // IN_TM
    return pl.pallas_call(
        _inproj_kernel,
        grid=(TOKENS // IN_TM, Z_COLS // IN_TN),
        in_specs=[pl.BlockSpec((IN_TM, D_MODEL), lambda i, j: (i, 0)),
                  pl.BlockSpec((1, 6, D_MODEL), lambda i, j: (i // blocks_per_batch, 0, 0)),
                  pl.BlockSpec((1, D_MODEL), lambda i, j: (0, 0)),
                  pl.BlockSpec((IN_TN, D_MODEL), lambda i, j: (j, 0))],
        out_specs=pl.BlockSpec((IN_TM, IN_TN), lambda i, j: (i, j)),
        out_shape=jax.ShapeDtypeStruct((TOKENS, Z_COLS), F32),
        scratch_shapes=[pltpu.VMEM((IN_TM, D_MODEL), BF16)],
        compiler_params=_params("arbitrary", "arbitrary"),
        name="inproj",
    )(x2, mod3, gain, w_t)


def _ret_kernel(pos_ref, z_ref, invf_ref, dintra_ref, zeta_ref, xi_ref, cdec_ref, o_ref, state_ref):
    @pl.when(pl.program_id(1) == 0)
    def _():
        state_ref[...] = jnp.zeros_like(state_ref)

    ang = pos_ref[...].astype(F32) * invf_ref[...]
    cos = jnp.cos(ang)
    sin = jnp.sin(ang)
    half = RET_DIM // 2

    def rot(t):
        t1, t2 = t[:, :half], t[:, half:]
        return jnp.concatenate([t1 * cos - t2 * sin, t1 * sin + t2 * cos], axis=-1)

    for h in range(RET_HEADS):
        lo, hi = h * RET_DIM, (h + 1) * RET_DIM
        q = rot(z_ref[:, lo:hi])
        k = rot(z_ref[:, RET_WIDTH + lo:RET_WIDTH + hi]) * (RET_DIM ** -0.5)
        v = z_ref[:, 2 * RET_WIDTH + lo:2 * RET_WIDTH + hi].astype(BF16)
        g = z_ref[:, 3 * RET_WIDTH + lo:3 * RET_WIDTH + hi]
        qb = q.astype(BF16)
        scores = _bdot_nt(qb, k) * dintra_ref[h]
        state = state_ref[h]
        out = _bdot(scores, v) + _bdot(qb, state) * xi_ref[h]
        state_ref[h] = state * cdec_ref[h] + _bdot_tn(k * zeta_ref[h], v)
        mu = jnp.mean(out, axis=-1, keepdims=True)
        d = out - mu
        var = jnp.mean(d * d, axis=-1, keepdims=True)
        o_ref[:, lo:hi] = (_silu(g) * (d * lax.rsqrt(var + GN_EPS_RET))).astype(BF16)


def _retention_tables():
    h = RET_HEADS
    half = RET_DIM // 2
    inv_freq = ROPE_BASE ** (-jnp.arange(half, dtype=F32) / half)
    log_gamma = jnp.log(1.0 - 2.0 ** (-5.0 - jnp.arange(h, dtype=F32)))
    idx = jnp.arange(RET_CHUNK, dtype=F32)
    dist = idx[:, None] - idx[None, :]
    decay_intra = jnp.where(dist >= 0, jnp.exp(log_gamma[:, None, None] * jnp.maximum(dist, 0.0)), 0.0)
    zeta = jnp.exp(log_gamma[:, None] * (RET_CHUNK - 1.0 - idx))
    xi = jnp.exp(log_gamma[:, None] * (idx + 1.0))
    chunk_decay = jnp.exp(log_gamma * RET_CHUNK)
    wide = lambda t: jnp.broadcast_to(t[:, :, None], (h, RET_CHUNK, RET_DIM))
    cdec = jnp.broadcast_to(chunk_decay[:, None, None], (h, 1, RET_DIM))
    return inv_freq[None, :], decay_intra, wide(zeta), wide(xi), cdec


def _retention(pos_col, z):
    invf, dintra, zeta, xi, cdec = _retention_tables()
    nch = SEQ // RET_CHUNK
    full = lambda a: pl.BlockSpec(a.shape, lambda b, n: (0,) * a.ndim)
    return pl.pallas_call(
        _ret_kernel,
        grid=(BATCH, nch),
        in_specs=[pl.BlockSpec((RET_CHUNK, 1), lambda b, n: (b * nch + n, 0)),
                  pl.BlockSpec((RET_CHUNK, SEC), lambda b, n: (b * nch + n, 0)),
                  full(invf), full(dintra), full(zeta), full(xi), full(cdec)],
        out_specs=pl.BlockSpec((RET_CHUNK, RET_WIDTH), lambda b, n: (b * nch + n, 0)),
        out_shape=jax.ShapeDtypeStruct((TOKENS, RET_WIDTH), BF16),
        scratch_shapes=[pltpu.VMEM((RET_HEADS, RET_DIM, RET_DIM), F32)],
        compiler_params=_params("arbitrary", "arbitrary"),
        name="retention",
    )(pos_col, z, invf, dintra, zeta, xi, cdec)


def _split2(x):
    hi = x.astype(BF16)
    return hi, (x - hi.astype(F32)).astype(BF16)


def _head_sums(xs, ones_ref):
    c = RWKV_CHUNK
    pieces = []
    for x in xs:
        for p in _split2(x):
            for g in range(RWKV_GROUPS):
                pieces.append(p[:, g * GROUP_LANES:(g + 1) * GROUP_LANES])
    res = jnp.dot(jnp.concatenate(pieces, axis=0), ones_ref[...], preferred_element_type=F32)
    outs = []
    for i in range(len(xs)):
        base = i * 2 * RWKV_GROUPS * c
        cols = []
        for g in range(RWKV_GROUPS):
            hi = res[base + g * c:base + (g + 1) * c]
            lo = res[base + (RWKV_GROUPS + g) * c:base + (RWKV_GROUPS + g + 1) * c]
            cols.append(hi + lo)
        outs.append(jnp.concatenate(cols, axis=-1))
    return outs


def _rwkv_kernel(z_ref, mu_ref, par_ref, wd_ref, wa_ref, wg_ref, ltri_ref, ones_ref, mstrict_ref, mincl_ref,
                 o_ref, s_ref, carry_ref):
    c = RWKV_CHUNK
    w = RWKV_WIDTH

    @pl.when(pl.program_id(1) == 0)
    def _():
        s_ref[...] = jnp.zeros_like(s_ref)
        carry_ref[...] = jnp.zeros_like(carry_ref)

    z = z_ref[...]
    row = lax.broadcasted_iota(jnp.int32, z.shape, 0)
    prev = jnp.where(row == 0, carry_ref[...], pltpu.roll(z, 1, 0))
    carry_ref[...] = z_ref[c - 1:c, :]
    zs = z + (prev - z) * mu_ref[...]

    r, kw, vw = zs[:, :w], zs[:, w:2 * w], zs[:, 2 * w:3 * w]
    zw = zs[:, 3 * w:3 * w + LORA_PAD]
    za = zs[:, 3 * w + LORA_PAD:3 * w + 2 * LORA_PAD]
    zg = zs[:, 3 * w + 2 * LORA_PAD:3 * w + 2 * LORA_PAD + GATE_LORA]
    par = par_ref[...]
    w0, a0, k_k, k_a, r_k, lnx_w, lnx_b = (par[i:i + 1] for i in range(7))

    dec_pre = w0 + _bdot(jnp.tanh(zw), wd_ref[...])
    softplus = jnp.maximum(-dec_pre, 0.0) + jnp.log1p(jnp.exp(-jnp.abs(dec_pre)))
    logdec = -jnp.exp(-softplus - 0.5)
    iclr = jax.nn.sigmoid(a0 + _bdot(za, wa_ref[...]))
    g_rw = _bdot(jax.nn.sigmoid(zg), wg_ref[...])

    kk = kw * k_k
    k_mod = kw * (1.0 + (iclr - 1.0) * k_a)
    kk_sq, rk_sum = _head_sums([kk * kk, r * k_mod * r_k], ones_ref)
    kk = kk * (1.0 / jnp.maximum(jnp.sqrt(kk_sq), 1e-12))
    a = -kk
    b = kk * iclr
    bonus = rk_sum * vw

    ld_hi = logdec.astype(BF16)
    ld_mid, ld_lo = _split2(logdec - ld_hi.astype(F32))
    ltri = ltri_ref[...]
    cum = (jnp.dot(ltri, ld_hi, preferred_element_type=F32) + jnp.dot(ltri, ld_mid, preferred_element_type=F32)
           + jnp.dot(ltri, ld_lo, preferred_element_type=F32))
    cum_end = cum[c - 1:c, :]
    p_in = jnp.exp(cum)
    p_inv = jnp.exp(-cum)
    p_end = jnp.exp(cum_end - cum)
    a_t = a * jnp.exp(cum - logdec)
    b_t = (b * p_inv).astype(BF16)
    k_t = (k_mod * p_inv).astype(BF16)
    r_t = r * p_in
    b_e = b * p_end
    k_e = k_mod * p_end
    decay_end = jnp.exp(cum_end)

    lane_head = lax.shift_right_logical(lax.broadcasted_iota(jnp.int32, (c, GROUP_LANES), 1), 6)
    head_masks = [lane_head == h for h in range(GROUP_HEADS)]

    def stack(x):
        return jnp.concatenate([jnp.where(m, x, 0.0).astype(BF16) for m in head_masks], axis=0)

    def rep(x):
        return jnp.concatenate([x] * GROUP_HEADS, axis=0)

    m_strict = mstrict_ref[...]
    m_incl = mincl_ref[...]
    eye = (lax.broadcasted_iota(jnp.int32, (GROUP_LANES, GROUP_LANES), 0)
           == lax.broadcasted_iota(jnp.int32, (GROUP_LANES, GROUP_LANES), 1)).astype(F32)
    n4 = GROUP_HEADS * c
    groups = range(RWKV_GROUPS)
    sls = [slice(g * GROUP_LANES, (g + 1) * GROUP_LANES) for g in groups]
    a_s = [stack(a_t[:, sl]) for sl in sls]
    r_s = [stack(r_t[:, sl]) for sl in sls]
    v_s = [stack(vw[:, sl]) for sl in sls]
    be_s = [stack(b_e[:, sl]) for sl in sls]
    ke_s = [stack(k_e[:, sl]) for sl in sls]
    blk = [_bdot_nt(jnp.concatenate([a_s[g], r_s[g]], axis=0),
                    jnp.concatenate([rep(b_t[:, sls[g]]), rep(k_t[:, sls[g]])], axis=0)) for g in groups]
    a_ab = [blk[g][:n4, :n4] * m_strict for g in groups]
    a_ak = [blk[g][:n4, n4:] * m_strict for g in groups]
    a_rb = [blk[g][n4:, :n4] * m_incl for g in groups]
    a_rk = [blk[g][n4:, n4:] * m_incl for g in groups]
    t_inv = [eye + a_ab[g] for g in groups]
    a_pow = a_ab
    for _ in range(5):
        a_pow = [_bdot(a_pow[g], a_pow[g]) for g in groups]
        t_inv = [t_inv[g] + _bdot(t_inv[g], a_pow[g]) for g in groups]
    a_eff = [_bdot(t_inv[g], a_s[g]) for g in groups]
    av = [_bdot(a_ak[g], v_s[g]) for g in groups]
    u_const = [_bdot(t_inv[g], av[g]) for g in groups]
    state = [s_ref[g] for g in groups]
    sb = [state[g].astype(BF16) for g in groups]
    u = [_bdot_nt(a_eff[g], sb[g]) + u_const[g] for g in groups]
    y_st = [_bdot_nt(r_s[g], sb[g]) + _bdot(a_rb[g], u[g]) + _bdot(a_rk[g], v_s[g]) for g in groups]
    for g in groups:
        s_ref[g] = (state[g] * decay_end[:, sls[g]]
                    + _bdot_tn(jnp.concatenate([u[g].astype(BF16), v_s[g]], axis=0),
                               jnp.concatenate([be_s[g], ke_s[g]], axis=0)))
    y = jnp.concatenate([t[:c] + t[c:2 * c] + t[2 * c:3 * c] + t[3 * c:] for t in y_st], axis=-1)

    (y_sum,) = _head_sums([y], ones_ref)
    d = y - y_sum * (1.0 / RWKV_HEAD)
    (d_sq,) = _head_sums([d * d], ones_ref)
    y = d * lax.rsqrt(d_sq * (1.0 / RWKV_HEAD) + GN_EPS_RWKV) * lnx_w + lnx_b + bonus
    o_ref[...] = (y * g_rw).astype(BF16)


def _rwkv_tables():
    c = RWKV_CHUNK
    ltri = (jnp.arange(c)[:, None] >= jnp.arange(c)[None, :]).astype(BF16)
    row = jnp.arange(GROUP_LANES)[:, None]
    col = jnp.arange(GROUP_LANES)[None, :]
    same_head = (row // RWKV_HEAD) == (col // RWKV_HEAD)
    ones = same_head.astype(BF16)
    m_strict = (same_head & (row > col)).astype(F32)
    m_incl = (same_head & (row >= col)).astype(F32)
    return ltri, ones, m_strict, m_incl


def _rwkv(z, mu_p, par, wd, wa, wg):
    assert RWKV_CHUNK == RWKV_HEAD
    ltri, ones, m_strict, m_incl = _rwkv_tables()
    nch = SEQ // RWKV_CHUNK
    full = lambda a: pl.BlockSpec(a.shape, lambda b, n: (0,) * a.ndim)
    return pl.pallas_call(
        _rwkv_kernel,
        grid=(BATCH, nch),
        in_specs=[pl.BlockSpec((RWKV_CHUNK, SEC), lambda b, n: (b * nch + n, 1)),
                  full(mu_p), full(par), full(wd), full(wa), full(wg),
                  full(ltri), full(ones), full(m_strict), full(m_incl)],
        out_specs=pl.BlockSpec((RWKV_CHUNK, RWKV_WIDTH), lambda b, n: (b * nch + n, 0)),
        out_shape=jax.ShapeDtypeStruct((TOKENS, RWKV_WIDTH), BF16),
        scratch_shapes=[pltpu.VMEM((RWKV_GROUPS, GROUP_LANES, GROUP_LANES), F32),
                        pltpu.VMEM((1, SEC), F32)],
        compiler_params=_params("arbitrary", "arbitrary"),
        name="rwkv",
    )(z, mu_p, par, wd, wa, wg, ltri, ones, m_strict, m_incl)


MG_TM = 512
MG_TN = 1024


def _merge_kernel(yr_ref, yw_ref, wr_ref, ww_ref, gr_ref, gw_ref, br_ref, bw_ref, o_ref, wrb_ref, wwb_ref):
    @pl.when(pl.program_id(1) == 0)
    def _():
        wrb_ref[...] = wr_ref[...].astype(BF16)
        wwb_ref[...] = ww_ref[...].astype(BF16)

    pr = jnp.dot(yr_ref[...], wrb_ref[...], preferred_element_type=F32)
    pw = jnp.dot(yw_ref[...], wwb_ref[...], preferred_element_type=F32)
    o = jax.nn.sigmoid(gr_ref[...] + br_ref[...]) * pr + jax.nn.sigmoid(gw_ref[...] + bw_ref[...]) * pw
    o_ref[...] = o.astype(BF16)


def _merge(y_ret, y_rw, w_ret, w_rw, z, b_gate):
    g0 = 2 * SEC // MG_TN
    nd = D_MODEL // MG_TN
    return pl.pallas_call(
        _merge_kernel,
        grid=(nd, TOKENS // MG_TM),
        in_specs=[pl.BlockSpec((MG_TM, RET_WIDTH), lambda j, i: (i, 0)),
                  pl.BlockSpec((MG_TM, RWKV_WIDTH), lambda j, i: (i, 0)),
                  pl.BlockSpec((RET_WIDTH, MG_TN), lambda j, i: (0, j)),
                  pl.BlockSpec((RWKV_WIDTH, MG_TN), lambda j, i: (0, j)),
                  pl.BlockSpec((MG_TM, MG_TN), lambda j, i: (i, g0 + j)),
                  pl.BlockSpec((MG_TM, MG_TN), lambda j, i: (i, g0 + nd + j)),
                  pl.BlockSpec((1, MG_TN), lambda j, i: (0, j)),
                  pl.BlockSpec((1, MG_TN), lambda j, i: (0, nd + j))],
        out_specs=pl.BlockSpec((MG_TM, MG_TN), lambda j, i: (i, j)),
        out_shape=jax.ShapeDtypeStruct((TOKENS, D_MODEL), BF16),
        scratch_shapes=[pltpu.VMEM((RET_WIDTH, MG_TN), BF16), pltpu.VMEM((RWKV_WIDTH, MG_TN), BF16)],
        compiler_params=_params("arbitrary", "arbitrary"),
        name="merge",
    )(y_ret, y_rw, w_ret, w_rw, z, z, b_gate, b_gate)


OP_TM = 512


def _oproj_kernel(m_ref, w_ref, x_ref, mod_ref, gain_ref, h_ref, u_ref, wb_ref):
    @pl.when(pl.program_id(0) == 0)
    def _():
        wb_ref[...] = w_ref[...].astype(BF16)

    mod = mod_ref[0]
    h = x_ref[...] + mod[2:3] * jnp.dot(m_ref[...], wb_ref[...], preferred_element_type=F32)
    h_ref[...] = h
    u_ref[...] = _modulated_norm(h, gain_ref[...], mod[4:5], mod[3:4]).astype(BF16)


def _oproj(merged, w_o, x2, mod3, gain):
    blocks_per_batch = SEQ // OP_TM
    return pl.pallas_call(
        _oproj_kernel,
        grid=(TOKENS // OP_TM,),
        in_specs=[pl.BlockSpec((OP_TM, D_MODEL), lambda i: (i, 0)),
                  pl.BlockSpec((D_MODEL, D_MODEL), lambda i: (0, 0), pipeline_mode=pl.Buffered(1)),
                  pl.BlockSpec((OP_TM, D_MODEL), lambda i: (i, 0)),
                  pl.BlockSpec((1, 6, D_MODEL), lambda i: (i // blocks_per_batch, 0, 0)),
                  pl.BlockSpec((1, D_MODEL), lambda i: (0, 0))],
        out_specs=[pl.BlockSpec((OP_TM, D_MODEL), lambda i: (i, 0)),
                   pl.BlockSpec((OP_TM, D_MODEL), lambda i: (i, 0))],
        out_shape=[jax.ShapeDtypeStruct((TOKENS, D_MODEL), F32),
                   jax.ShapeDtypeStruct((TOKENS, D_MODEL), BF16)],
        scratch_shapes=[pltpu.VMEM((D_MODEL, D_MODEL), BF16)],
        compiler_params=_params("arbitrary"),
        name="oproj",
    )(merged, w_o, x2, mod3, gain)


FI_TM = 1024
FI_TN = 512


def _ffn_in_kernel(u_ref, wg_ref, wu_ref, o_ref, wgb_ref, wub_ref):
    @pl.when(pl.program_id(1) == 0)
    def _():
        wgb_ref[...] = wg_ref[...].astype(BF16)
        wub_ref[...] = wu_ref[...].astype(BF16)

    u = u_ref[...]
    gate = jnp.dot(u, wgb_ref[...], preferred_element_type=F32)
    up = jnp.dot(u, wub_ref[...], preferred_element_type=F32)
    o_ref[...] = (_silu(gate) * up).astype(BF16)


def _ffn_in(u2, w_in):
    nt = FFN_HIDDEN // FI_TN
    return pl.pallas_call(
        _ffn_in_kernel,
        grid=(nt, TOKENS // FI_TM),
        in_specs=[pl.BlockSpec((FI_TM, D_MODEL), lambda j, i: (i, 0)),
                  pl.BlockSpec((D_MODEL, FI_TN), lambda j, i: (0, j)),
                  pl.BlockSpec((D_MODEL, FI_TN), lambda j, i: (0, nt + j))],
        out_specs=pl.BlockSpec((FI_TM, FI_TN), lambda j, i: (i, j)),
        out_shape=jax.ShapeDtypeStruct((TOKENS, FFN_HIDDEN), BF16),
        scratch_shapes=[pltpu.VMEM((D_MODEL, FI_TN), BF16), pltpu.VMEM((D_MODEL, FI_TN), BF16)],
        compiler_params=_params("arbitrary", "arbitrary"),
        name="ffn_in",
    )(u2, w_in, w_in)


FO_TM = 1024
FO_TK = 512


def _ffn_out_kernel(a_ref, w_ref, h_ref, mod_ref, gain_ref, o_ref, acc_ref):
    k = pl.program_id(1)

    @pl.when(k == 0)
    def _():
        acc_ref[...] = jnp.zeros_like(acc_ref)

    acc_ref[...] += jnp.dot(a_ref[...], w_ref[...], preferred_element_type=F32)

    @pl.when(k == pl.num_programs(1) - 1)
    def _():
        h = h_ref[...] + mod_ref[0][5:6] * acc_ref[...]
        o_ref[...] = h * lax.rsqrt(jnp.mean(h * h, axis=-1, keepdims=True) + NORM_EPS) * gain_ref[...]


def _ffn_out(act, w_out, h1, mod3, gain):
    blocks_per_batch = SEQ // FO_TM
    return pl.pallas_call(
        _ffn_out_kernel,
        grid=(TOKENS // FO_TM, FFN_HIDDEN // FO_TK),
        in_specs=[pl.BlockSpec((FO_TM, FO_TK), lambda i, k: (i, k)),
                  pl.BlockSpec((FO_TK, D_MODEL), lambda i, k: (k, 0)),
                  pl.BlockSpec((FO_TM, D_MODEL), lambda i, k: (i, 0)),
                  pl.BlockSpec((1, 6, D_MODEL), lambda i, k: (i // blocks_per_batch, 0, 0)),
                  pl.BlockSpec((1, D_MODEL), lambda i, k: (0, 0))],
        out_specs=pl.BlockSpec((FO_TM, D_MODEL), lambda i, k: (i, 0)),
        out_shape=jax.ShapeDtypeStruct((TOKENS, D_MODEL), F32),
        scratch_shapes=[pltpu.VMEM((FO_TM, D_MODEL), F32)],
        compiler_params=_params("arbitrary", "arbitrary"),
        name="ffn_out",
    )(act, w_out, h1, mod3, gain)


def _pad_cols(t, width):
    return jnp.pad(t, ((0, 0), (0, width - t.shape[1])))


def _pack_w_in_t(w_t):
    o3 = HEAD_COLS
    o4 = o3 + DECAY_LORA
    o5 = o4 + ICLR_LORA
    o6 = RET_COLS + SHIFT_COLS
    lora = jnp.concatenate([_pad_rows(w_t[o3:o4], LORA_PAD), _pad_rows(w_t[o4:o5], LORA_PAD), w_t[o5:o6]], axis=0)
    return jnp.concatenate([w_t[:o3], _pad_rows(lora, 2 * SEC - HEAD_COLS), w_t[o6:]], axis=0).astype(BF16)


def _pack_mu(mu):
    o3 = 3 * RWKV_WIDTH
    o4 = o3 + DECAY_LORA
    o5 = o4 + ICLR_LORA
    mu = mu[None, :]
    rw = jnp.concatenate([mu[:, :o3], _pad_cols(mu[:, o3:o4], LORA_PAD), _pad_cols(mu[:, o4:o5], LORA_PAD),
                          mu[:, o5:]], axis=1)
    return _pad_cols(rw, SEC)


def _pad_rows(t, rows):
    return jnp.pad(t, ((0, rows - t.shape[0]), (0, 0)))


def kernel(x, c, positions, w_ada, b_ada, norm_mix, norm_ffn, norm_final, w_in, b_gate, mu_shift, w0, w_decay_up, a0, w_iclr_up, w_gate_up, k_k, k_a, r_k, lnx_w, lnx_b, w_ret_out, w_rwkv_out, w_o, w_ffn_in, w_ffn_out):
    assert x.shape == (BATCH, SEQ, D_MODEL) and w_ada.shape[0] == 1
    x2 = x.reshape(TOKENS, D_MODEL)
    pos_col = positions.reshape(TOKENS, 1)

    mod = _ada(_pad_rows(c, 8), w_ada[0], b_ada)
    mod3 = mod[:BATCH].reshape(BATCH, 6, D_MODEL)

    z = _inproj(x2, mod3, norm_mix, _pack_w_in_t(jnp.swapaxes(w_in, 1, 2)[0]))
    y_ret = _retention(pos_col, z)

    par = jnp.concatenate([w0, a0, k_k, k_a, r_k.reshape(1, RWKV_WIDTH), lnx_w, lnx_b,
                           jnp.zeros((1, RWKV_WIDTH), F32)], axis=0)
    y_rw = _rwkv(z, _pack_mu(mu_shift[0]), par,
                 _pad_rows(w_decay_up[0], LORA_PAD).astype(BF16), _pad_rows(w_iclr_up[0], LORA_PAD).astype(BF16),
                 w_gate_up[0].astype(BF16))

    merged = _merge(y_ret, y_rw, w_ret_out[0], w_rwkv_out[0], z, b_gate)
    h1, u2 = _oproj(merged, w_o[0], x2, mod3, norm_ffn)
    act = _ffn_in(u2, w_ffn_in[0])
    out = _ffn_out(act, w_ffn_out[0].astype(BF16), h1, mod3, norm_final[None, :])
    return out.reshape(BATCH, SEQ, D_MODEL)
```

```python
import functools

import jax
import jax.numpy as jnp
from jax import lax
from jax.experimental import pallas as pl
from jax.experimental.pallas import tpu as pltpu

F32 = jnp.float32
BF16 = jnp.bfloat16

D_MODEL = 2048
BATCH = 2
SEQ = 4096
TOKENS = BATCH * SEQ

RET_HEADS = 4
RET_DIM = 256
RET_WIDTH = RET_HEADS * RET_DIM
RET_CHUNK = 128
ROPE_BASE = 10000.0

RWKV_HEAD = 64
RWKV_WIDTH = 1024
RWKV_HEADS = RWKV_WIDTH // RWKV_HEAD
DECAY_LORA = 96
ICLR_LORA = 96
GATE_LORA = 256
RWKV_CHUNK = 64
GROUP_LANES = 256
GROUP_HEADS = GROUP_LANES // RWKV_HEAD
RWKV_GROUPS = RWKV_WIDTH // GROUP_LANES
LORA_PAD = 128

FFN_HIDDEN = ((8 * D_MODEL // 3 + 255) // 256) * 256
RET_COLS = 4 * RET_WIDTH
SHIFT_COLS = 3 * RWKV_WIDTH + DECAY_LORA + ICLR_LORA + GATE_LORA
GATE_COLS = 2 * D_MODEL
SEC = 4096
Z_COLS = 3 * SEC
HEAD_COLS = RET_COLS + 3 * RWKV_WIDTH
TAIL_COLS = Z_COLS - HEAD_COLS
NORM_EPS = 1e-6
GN_EPS_RET = 1e-5
GN_EPS_RWKV = 64e-5

VMEM_LIMIT = 56 * 1024 * 1024
SUBLANES = 8


def _params(*sem, vmem=VMEM_LIMIT):
    return pltpu.CompilerParams(dimension_semantics=sem, vmem_limit_bytes=vmem)


def _bdot(a, b):
    return jnp.dot(a.astype(BF16), b.astype(BF16), preferred_element_type=F32)


def _bdot_nt(a, b):
    return lax.dot_general(a.astype(BF16), b.astype(BF16), (((1,), (1,)), ((), ())),
                           preferred_element_type=F32)


def _bdot_tn(a, b):
    return lax.dot_general(a.astype(BF16), b.astype(BF16), (((0,), (0,)), ((), ())),
                           preferred_element_type=F32)


def _silu(x):
    return x * jax.nn.sigmoid(x)


ADA_TN = 1024


def _ada_kernel(c_ref, w_ref, b_ref, o_ref):
    o_ref[...] = _bdot(_silu(c_ref[...]), w_ref[...]) + b_ref[...]


def _ada(c_pad, w_ada, b_ada):
    n = w_ada.shape[1]
    return pl.pallas_call(
        _ada_kernel,
        grid=(n // ADA_TN,),
        in_specs=[pl.BlockSpec((8, D_MODEL), lambda j: (0, 0)),
                  pl.BlockSpec((D_MODEL, ADA_TN), lambda j: (0, j)),
                  pl.BlockSpec((1, ADA_TN), lambda j: (0, j))],
        out_specs=pl.BlockSpec((8, ADA_TN), lambda j: (0, j)),
        out_shape=jax.ShapeDtypeStruct((8, n), F32),
        compiler_params=_params("arbitrary"),
        name="ada",
    )(c_pad, w_ada, b_ada)


IN_TM = 1024
IN_TN = 1024


def _modulated_norm(x, gain, scale, shift):
    y = x * lax.rsqrt(jnp.mean(x * x, axis=-1, keepdims=True) + NORM_EPS)
    return y * gain * (1.0 + scale) + shift


def _norm1_kernel(x_ref, mod_ref, gain_ref, u_ref):
    mod = mod_ref[0]
    u_ref[...] = _modulated_norm(x_ref[...], gain_ref[...], mod[1:2], mod[0:1]).astype(BF16)


def _norm1(x2, mod3, gain):
    blocks_per_batch = SEQ // IN_TM
    return pl.pallas_call(
        _norm1_kernel,
        grid=(TOKENS // IN_TM,),
        in_specs=[pl.BlockSpec((IN_TM, D_MODEL), lambda i: (i, 0)),
                  pl.BlockSpec((1, 6, D_MODEL), lambda i: (i // blocks_per_batch, 0, 0)),
                  pl.BlockSpec((1, D_MODEL), lambda i: (0, 0))],
        out_specs=pl.BlockSpec((IN_TM, D_MODEL), lambda i: (i, 0)),
        out_shape=jax.ShapeDtypeStruct((TOKENS, D_MODEL), BF16),
        compiler_params=_params("arbitrary"),
        name="norm1",
    )(x2, mod3, gain)


IN_LORA_BLOCK = HEAD_COLS // IN_TN
GATE_ROW0 = RET_COLS + SHIFT_COLS


def _inproj_kernel(u_ref, w_ref, wl_ref, o_ref, wb_ref):
    j = pl.program_id(0)
    first = pl.program_id(1) == 0

    @pl.when(first & (j != IN_LORA_BLOCK))
    def _():
        wb_ref[...] = w_ref[...].astype(BF16)

    @pl.when(first & (j == IN_LORA_BLOCK))
    def _():
        wb_ref[...] = wl_ref[...].astype(BF16)

    o_ref[...] = _bdot_nt(u_ref[...], wb_ref[...])


def _inproj(u, w_t, w_lora):
    assert HEAD_COLS % IN_TN == 0 and 2 * SEC - HEAD_COLS == IN_TN

    def w_row(j, i):
        tn, g0 = IN_TN // SUBLANES, GATE_ROW0 // SUBLANES
        head = jnp.minimum(j, IN_LORA_BLOCK - 1) * tn
        return (jnp.where(j > IN_LORA_BLOCK, g0 + (j - IN_LORA_BLOCK - 1) * tn, head) * SUBLANES, 0)

    return pl.pallas_call(
        _inproj_kernel,
        grid=(Z_COLS // IN_TN, TOKENS // IN_TM),
        in_specs=[pl.BlockSpec((IN_TM, D_MODEL), lambda j, i: (i, 0)),
                  pl.BlockSpec((pl.Element(IN_TN), pl.Element(D_MODEL)), w_row),
                  pl.BlockSpec((IN_TN, D_MODEL), lambda j, i: (0, 0), pipeline_mode=pl.Buffered(1))],
        out_specs=pl.BlockSpec((IN_TM, IN_TN), lambda j, i: (i, j)),
        out_shape=jax.ShapeDtypeStruct((TOKENS, Z_COLS), F32),
        scratch_shapes=[pltpu.VMEM((IN_TN, D_MODEL), BF16)],
        compiler_params=_params("arbitrary", "arbitrary"),
        name="inproj",
    )(u, w_t, w_lora)


def _ret_kernel(pos_ref, z_ref, invf_ref, dintra_ref, zeta_ref, xi_ref, cdec_ref, o_ref, state_ref):
    @pl.when(pl.program_id(1) == 0)
    def _():
        state_ref[...] = jnp.zeros_like(state_ref)

    ang = pos_ref[...].astype(F32) * invf_ref[...]
    cos = jnp.cos(ang)
    sin = jnp.sin(ang)
    half = RET_DIM // 2

    def rot(t):
        t1, t2 = t[:, :half], t[:, half:]
        return jnp.concatenate([t1 * cos - t2 * sin, t1 * sin + t2 * cos], axis=-1)

    for h in range(RET_HEADS):
        lo, hi = h * RET_DIM, (h + 1) * RET_DIM
        q = rot(z_ref[:, lo:hi])
        k = rot(z_ref[:, RET_WIDTH + lo:RET_WIDTH + hi]) * (RET_DIM ** -0.5)
        v = z_ref[:, 2 * RET_WIDTH + lo:2 * RET_WIDTH + hi].astype(BF16)
        g = z_ref[:, 3 * RET_WIDTH + lo:3 * RET_WIDTH + hi]
        qb = q.astype(BF16)
        scores = _bdot_nt(qb, k) * dintra_ref[h]
        state = state_ref[h]
        out = _bdot(scores, v) + _bdot(qb, state) * xi_ref[h]
        state_ref[h] = state * cdec_ref[h] + _bdot_tn(k * zeta_ref[h], v)
        mu = jnp.mean(out, axis=-1, keepdims=True)
        d = out - mu
        var = jnp.mean(d * d, axis=-1, keepdims=True)
        o_ref[:, lo:hi] = (_silu(g) * (d * lax.rsqrt(var + GN_EPS_RET))).astype(BF16)


def _retention_tables():
    h = RET_HEADS
    half = RET_DIM // 2
    inv_freq = ROPE_BASE ** (-jnp.arange(half, dtype=F32) / half)
    log_gamma = jnp.log(1.0 - 2.0 ** (-5.0 - jnp.arange(h, dtype=F32)))
    idx = jnp.arange(RET_CHUNK, dtype=F32)
    dist = idx[:, None] - idx[None, :]
    decay_intra = jnp.where(dist >= 0, jnp.exp(log_gamma[:, None, None] * jnp.maximum(dist, 0.0)), 0.0)
    zeta = jnp.exp(log_gamma[:, None] * (RET_CHUNK - 1.0 - idx))
    xi = jnp.exp(log_gamma[:, None] * (idx + 1.0))
    chunk_decay = jnp.exp(log_gamma * RET_CHUNK)
    wide = lambda t: jnp.broadcast_to(t[:, :, None], (h, RET_CHUNK, RET_DIM))
    cdec = jnp.broadcast_to(chunk_decay[:, None, None], (h, 1, RET_DIM))
    return inv_freq[None, :], decay_intra, wide(zeta), wide(xi), cdec


def _retention(pos_col, z):
    invf, dintra, zeta, xi, cdec = _retention_tables()
    nch = SEQ // RET_CHUNK
    full = lambda a: pl.BlockSpec(a.shape, lambda b, n: (0,) * a.ndim)
    return pl.pallas_call(
        _ret_kernel,
        grid=(BATCH, nch),
        in_specs=[pl.BlockSpec((RET_CHUNK, 1), lambda b, n: (b * nch + n, 0)),
                  pl.BlockSpec((RET_CHUNK, SEC), lambda b, n: (b * nch + n, 0)),
                  full(invf), full(dintra), full(zeta), full(xi), full(cdec)],
        out_specs=pl.BlockSpec((RET_CHUNK, RET_WIDTH), lambda b, n: (b * nch + n, 0)),
        out_shape=jax.ShapeDtypeStruct((TOKENS, RET_WIDTH), BF16),
        scratch_shapes=[pltpu.VMEM((RET_HEADS, RET_DIM, RET_DIM), F32)],
        compiler_params=_params("arbitrary", "arbitrary"),
        name="retention",
    )(pos_col, z, invf, dintra, zeta, xi, cdec)


def _split2(x):
    hi = x.astype(BF16)
    return hi, (x - hi.astype(F32)).astype(BF16)


def _head_sums(xs, ones_ref):
    c = RWKV_CHUNK
    pieces = []
    for x in xs:
        for p in _split2(x):
            for g in range(RWKV_GROUPS):
                pieces.append(p[:, g * GROUP_LANES:(g + 1) * GROUP_LANES])
    res = jnp.dot(jnp.concatenate(pieces, axis=0), ones_ref[...], preferred_element_type=F32)
    outs = []
    for i in range(len(xs)):
        base = i * 2 * RWKV_GROUPS * c
        cols = []
        for g in range(RWKV_GROUPS):
            hi = res[base + g * c:base + (g + 1) * c]
            lo = res[base + (RWKV_GROUPS + g) * c:base + (RWKV_GROUPS + g + 1) * c]
            cols.append(hi + lo)
        outs.append(jnp.concatenate(cols, axis=-1))
    return outs


def _rwkv_kernel(z_ref, mu_ref, par_ref, wd_ref, wa_ref, wg_ref, ltri_ref, ones_ref, mstrict_ref, mincl_ref,
                 o_ref, s_ref, carry_ref):
    c = RWKV_CHUNK
    w = RWKV_WIDTH

    @pl.when(pl.program_id(1) == 0)
    def _():
        s_ref[...] = jnp.zeros_like(s_ref)
        carry_ref[...] = jnp.zeros_like(carry_ref)

    z = z_ref[...]
    row = lax.broadcasted_iota(jnp.int32, z.shape, 0)
    prev = jnp.where(row == 0, carry_ref[...], pltpu.roll(z, 1, 0))
    carry_ref[...] = z_ref[c - 1:c, :]
    zs = z + (prev - z) * mu_ref[...]

    r, kw, vw = zs[:, :w], zs[:, w:2 * w], zs[:, 2 * w:3 * w]
    zw = zs[:, 3 * w:3 * w + LORA_PAD]
    za = zs[:, 3 * w + LORA_PAD:3 * w + 2 * LORA_PAD]
    zg = zs[:, 3 * w + 2 * LORA_PAD:3 * w + 2 * LORA_PAD + GATE_LORA]
    par = par_ref[...]
    w0, a0, k_k, k_a, r_k, lnx_w, lnx_b = (par[i:i + 1] for i in range(7))

    dec_pre = w0 + _bdot(jnp.tanh(zw), wd_ref[...])
    softplus = jnp.maximum(-dec_pre, 0.0) + jnp.log1p(jnp.exp(-jnp.abs(dec_pre)))
    logdec = -jnp.exp(-softplus - 0.5)
    iclr = jax.nn.sigmoid(a0 + _bdot(za, wa_ref[...]))
    g_rw = _bdot(jax.nn.sigmoid(zg), wg_ref[...])

    kk = kw * k_k
    k_mod = kw * (1.0 + (iclr - 1.0) * k_a)
    kk_sq, rk_sum = _head_sums([kk * kk, r * k_mod * r_k], ones_ref)
    kk = kk * (1.0 / jnp.maximum(jnp.sqrt(kk_sq), 1e-12))
    a = -kk
    b = kk * iclr
    bonus = rk_sum * vw

    ld_hi = logdec.astype(BF16)
    ld_mid, ld_lo = _split2(logdec - ld_hi.astype(F32))
    ltri = ltri_ref[...]
    cum = (jnp.dot(ltri, ld_hi, preferred_element_type=F32) + jnp.dot(ltri, ld_mid, preferred_element_type=F32)
           + jnp.dot(ltri, ld_lo, preferred_element_type=F32))
    cum_end = cum[c - 1:c, :]
    p_in = jnp.exp(cum)
    p_inv = jnp.exp(-cum)
    p_end = jnp.exp(cum_end - cum)
    a_t = a * jnp.exp(cum - logdec)
    b_t = (b * p_inv).astype(BF16)
    k_t = (k_mod * p_inv).astype(BF16)
    r_t = r * p_in
    b_e = b * p_end
    k_e = k_mod * p_end
    decay_end = jnp.exp(cum_end)

    lane_head = lax.shift_right_logical(lax.broadcasted_iota(jnp.int32, (c, GROUP_LANES), 1), 6)
    head_masks = [lane_head == h for h in range(GROUP_HEADS)]

    def stack(x):
        return jnp.concatenate([jnp.where(m, x, 0.0).astype(BF16) for m in head_masks], axis=0)

    def rep(x):
        return jnp.concatenate([x] * GROUP_HEADS, axis=0)

    m_strict = mstrict_ref[...]
    m_incl = mincl_ref[...]
    eye = (lax.broadcasted_iota(jnp.int32, (GROUP_LANES, GROUP_LANES), 0)
           == lax.broadcasted_iota(jnp.int32, (GROUP_LANES, GROUP_LANES), 1)).astype(F32)
    n4 = GROUP_HEADS * c
    groups = range(RWKV_GROUPS)
    sls = [slice(g * GROUP_LANES, (g + 1) * GROUP_LANES) for g in groups]
    a_s = [stack(a_t[:, sl]) for sl in sls]
    r_s = [stack(r_t[:, sl]) for sl in sls]
    v_s = [stack(vw[:, sl]) for sl in sls]
    be_s = [stack(b_e[:, sl]) for sl in sls]
    ke_s = [stack(k_e[:, sl]) for sl in sls]
    blk = [_bdot_nt(jnp.concatenate([a_s[g], r_s[g]], axis=0),
                    jnp.concatenate([rep(b_t[:, sls[g]]), rep(k_t[:, sls[g]])], axis=0)) for g in groups]
    a_ab = [blk[g][:n4, :n4] * m_strict for g in groups]
    a_ak = [blk[g][:n4, n4:] * m_strict for g in groups]
    a_rb = [blk[g][n4:, :n4] * m_incl for g in groups]
    a_rk = [blk[g][n4:, n4:] * m_incl for g in groups]
    t_inv = [eye + a_ab[g] for g in groups]
    a_pow = a_ab
    for _ in range(5):
        a_pow = [_bdot(a_pow[g], a_pow[g]) for g in groups]
        t_inv = [t_inv[g] + _bdot(t_inv[g], a_pow[g]) for g in groups]
    a_eff = [_bdot(t_inv[g], a_s[g]) for g in groups]
    av = [_bdot(a_ak[g], v_s[g]) for g in groups]
    u_const = [_bdot(t_inv[g], av[g]) for g in groups]
    state = [s_ref[g] for g in groups]
    sb = [state[g].astype(BF16) for g in groups]
    u = [_bdot_nt(a_eff[g], sb[g]) + u_const[g] for g in groups]
    y_st = [_bdot_nt(r_s[g], sb[g]) + _bdot(a_rb[g], u[g]) + _bdot(a_rk[g], v_s[g]) for g in groups]
    for g in groups:
        s_ref[g] = (state[g] * decay_end[:, sls[g]]
                    + _bdot_tn(jnp.concatenate([u[g].astype(BF16), v_s[g]], axis=0),
                               jnp.concatenate([be_s[g], ke_s[g]], axis=0)))
    y = jnp.concatenate([t[:c] + t[c:2 * c] + t[2 * c:3 * c] + t[3 * c:] for t in y_st], axis=-1)

    (y_sum,) = _head_sums([y], ones_ref)
    d = y - y_sum * (1.0 / RWKV_HEAD)
    (d_sq,) = _head_sums([d * d], ones_ref)
    y = d * lax.rsqrt(d_sq * (1.0 / RWKV_HEAD) + GN_EPS_RWKV) * lnx_w + lnx_b + bonus
    o_ref[...] = (y * g_rw).astype(BF16)


def _rwkv_tables():
    c = RWKV_CHUNK
    ltri = (jnp.arange(c)[:, None] >= jnp.arange(c)[None, :]).astype(BF16)
    row = jnp.arange(GROUP_LANES)[:, None]
    col = jnp.arange(GROUP_LANES)[None, :]
    same_head = (row // RWKV_HEAD) == (col // RWKV_HEAD)
    ones = same_head.astype(BF16)
    m_strict = (same_head & (row > col)).astype(F32)
    m_incl = (same_head & (row >= col)).astype(F32)
    return ltri, ones, m_strict, m_incl


def _rwkv(z, mu_p, par, wd, wa, wg):
    assert RWKV_CHUNK == RWKV_HEAD
    ltri, ones, m_strict, m_incl = _rwkv_tables()
    nch = SEQ // RWKV_CHUNK
    full = lambda a: pl.BlockSpec(a.shape, lambda b, n: (0,) * a.ndim)
    return pl.pallas_call(
        _rwkv_kernel,
        grid=(BATCH, nch),
        in_specs=[pl.BlockSpec((RWKV_CHUNK, SEC), lambda b, n: (b * nch + n, 1)),
                  full(mu_p), full(par), full(wd), full(wa), full(wg),
                  full(ltri), full(ones), full(m_strict), full(m_incl)],
        out_specs=pl.BlockSpec((RWKV_CHUNK, RWKV_WIDTH), lambda b, n: (b * nch + n, 0)),
        out_shape=jax.ShapeDtypeStruct((TOKENS, RWKV_WIDTH), BF16),
        scratch_shapes=[pltpu.VMEM((RWKV_GROUPS, GROUP_LANES, GROUP_LANES), F32),
                        pltpu.VMEM((1, SEC), F32)],
        compiler_params=_params("arbitrary", "arbitrary"),
        name="rwkv",
    )(z, mu_p, par, wd, wa, wg, ltri, ones, m_strict, m_incl)


MG_TM = 512
MG_TN = 1024


def _merge_kernel(yr_ref, yw_ref, wr_ref, ww_ref, gr_ref, gw_ref, br_ref, bw_ref, o_ref, wrb_ref, wwb_ref):
    @pl.when(pl.program_id(1) == 0)
    def _():
        wrb_ref[...] = wr_ref[...].astype(BF16)
        wwb_ref[...] = ww_ref[...].astype(BF16)

    pr = jnp.dot(yr_ref[...], wrb_ref[...], preferred_element_type=F32)
    pw = jnp.dot(yw_ref[...], wwb_ref[...], preferred_element_type=F32)
    o = jax.nn.sigmoid(gr_ref[...] + br_ref[...]) * pr + jax.nn.sigmoid(gw_ref[...] + bw_ref[...]) * pw
    o_ref[...] = o.astype(BF16)


def _merge(y_ret, y_rw, w_ret, w_rw, z, b_gate):
    g0 = 2 * SEC // MG_TN
    nd = D_MODEL // MG_TN
    return pl.pallas_call(
        _merge_kernel,
        grid=(nd, TOKENS // MG_TM),
        in_specs=[pl.BlockSpec((MG_TM, RET_WIDTH), lambda j, i: (i, 0)),
                  pl.BlockSpec((MG_TM, RWKV_WIDTH), lambda j, i: (i, 0)),
                  pl.BlockSpec((RET_WIDTH, MG_TN), lambda j, i: (0, j)),
                  pl.BlockSpec((RWKV_WIDTH, MG_TN), lambda j, i: (0, j)),
                  pl.BlockSpec((MG_TM, MG_TN), lambda j, i: (i, g0 + j)),
                  pl.BlockSpec((MG_TM, MG_TN), lambda j, i: (i, g0 + nd + j)),
                  pl.BlockSpec((1, MG_TN), lambda j, i: (0, j)),
                  pl.BlockSpec((1, MG_TN), lambda j, i: (0, nd + j))],
        out_specs=pl.BlockSpec((MG_TM, MG_TN), lambda j, i: (i, j)),
        out_shape=jax.ShapeDtypeStruct((TOKENS, D_MODEL), BF16),
        scratch_shapes=[pltpu.VMEM((RET_WIDTH, MG_TN), BF16), pltpu.VMEM((RWKV_WIDTH, MG_TN), BF16)],
        compiler_params=_params("arbitrary", "arbitrary"),
        name="merge",
    )(y_ret, y_rw, w_ret, w_rw, z, z, b_gate, b_gate)


OP_TM = 512


def _oproj_kernel(m_ref, w_ref, x_ref, mod_ref, gain_ref, h_ref, u_ref, wb_ref):
    @pl.when(pl.program_id(0) == 0)
    def _():
        wb_ref[...] = w_ref[...].astype(BF16)

    mod = mod_ref[0]
    h = x_ref[...] + mod[2:3] * jnp.dot(m_ref[...], wb_ref[...], preferred_element_type=F32)
    h_ref[...] = h
    u_ref[...] = _modulated_norm(h, gain_ref[...], mod[4:5], mod[3:4]).astype(BF16)


def _oproj(merged, w_o, x2, mod3, gain):
    blocks_per_batch = SEQ // OP_TM
    return pl.pallas_call(
        _oproj_kernel,
        grid=(TOKENS // OP_TM,),
        in_specs=[pl.BlockSpec((OP_TM, D_MODEL), lambda i: (i, 0)),
                  pl.BlockSpec((D_MODEL, D_MODEL), lambda i: (0, 0), pipeline_mode=pl.Buffered(1)),
                  pl.BlockSpec((OP_TM, D_MODEL), lambda i: (i, 0)),
                  pl.BlockSpec((1, 6, D_MODEL), lambda i: (i // blocks_per_batch, 0, 0)),
                  pl.BlockSpec((1, D_MODEL), lambda i: (0, 0))],
        out_specs=[pl.BlockSpec((OP_TM, D_MODEL), lambda i: (i, 0)),
                   pl.BlockSpec((OP_TM, D_MODEL), lambda i: (i, 0))],
        out_shape=[jax.ShapeDtypeStruct((TOKENS, D_MODEL), F32),
                   jax.ShapeDtypeStruct((TOKENS, D_MODEL), BF16)],
        scratch_shapes=[pltpu.VMEM((D_MODEL, D_MODEL), BF16)],
        compiler_params=_params("arbitrary"),
        name="oproj",
    )(merged, w_o, x2, mod3, gain)


FI_TM = 1024
FI_TN = 512


def _ffn_in_kernel(u_ref, wg_ref, wu_ref, o_ref, wgb_ref, wub_ref):
    @pl.when(pl.program_id(1) == 0)
    def _():
        wgb_ref[...] = wg_ref[...].astype(BF16)
        wub_ref[...] = wu_ref[...].astype(BF16)

    u = u_ref[...]
    gate = jnp.dot(u, wgb_ref[...], preferred_element_type=F32)
    up = jnp.dot(u, wub_ref[...], preferred_element_type=F32)
    o_ref[...] = (_silu(gate) * up).astype(BF16)


def _ffn_in(u2, w_in):
    nt = FFN_HIDDEN // FI_TN
    return pl.pallas_call(
        _ffn_in_kernel,
        grid=(nt, TOKENS // FI_TM),
        in_specs=[pl.BlockSpec((FI_TM, D_MODEL), lambda j, i: (i, 0)),
                  pl.BlockSpec((D_MODEL, FI_TN), lambda j, i: (0, j)),
                  pl.BlockSpec((D_MODEL, FI_TN), lambda j, i: (0, nt + j))],
        out_specs=pl.BlockSpec((FI_TM, FI_TN), lambda j, i: (i, j)),
        out_shape=jax.ShapeDtypeStruct((TOKENS, FFN_HIDDEN), BF16),
        scratch_shapes=[pltpu.VMEM((D_MODEL, FI_TN), BF16), pltpu.VMEM((D_MODEL, FI_TN), BF16)],
        compiler_params=_params("arbitrary", "arbitrary"),
        name="ffn_in",
    )(u2, w_in, w_in)


FO_TM = 1024
FO_TK = 512


def _ffn_out_kernel(a_ref, w_ref, h_ref, mod_ref, gain_ref, o_ref, acc_ref):
    k = pl.program_id(1)

    @pl.when(k == 0)
    def _():
        acc_ref[...] = jnp.zeros_like(acc_ref)

    acc_ref[...] += jnp.dot(a_ref[...], w_ref[...], preferred_element_type=F32)

    @pl.when(k == pl.num_programs(1) - 1)
    def _():
        h = h_ref[...] + mod_ref[0][5:6] * acc_ref[...]
        o_ref[...] = h * lax.rsqrt(jnp.mean(h * h, axis=-1, keepdims=True) + NORM_EPS) * gain_ref[...]


def _ffn_out(act, w_out, h1, mod3, gain):
    blocks_per_batch = SEQ // FO_TM
    return pl.pallas_call(
        _ffn_out_kernel,
        grid=(TOKENS // FO_TM, FFN_HIDDEN // FO_TK),
        in_specs=[pl.BlockSpec((FO_TM, FO_TK), lambda i, k: (i, k)),
                  pl.BlockSpec((FO_TK, D_MODEL), lambda i, k: (k, 0)),
                  pl.BlockSpec((FO_TM, D_MODEL), lambda i, k: (i, 0)),
                  pl.BlockSpec((1, 6, D_MODEL), lambda i, k: (i // blocks_per_batch, 0, 0)),
                  pl.BlockSpec((1, D_MODEL), lambda i, k: (0, 0))],
        out_specs=pl.BlockSpec((FO_TM, D_MODEL), lambda i, k: (i, 0)),
        out_shape=jax.ShapeDtypeStruct((TOKENS, D_MODEL), F32),
        scratch_shapes=[pltpu.VMEM((FO_TM, D_MODEL), F32)],
        compiler_params=_params("arbitrary", "arbitrary"),
        name="ffn_out",
    )(act, w_out, h1, mod3, gain)


def _pad_cols(t, width):
    return jnp.pad(t, ((0, 0), (0, width - t.shape[1])))


def _pack_lora_rows(w_t):
    o3 = HEAD_COLS
    o4 = o3 + DECAY_LORA
    o5 = o4 + ICLR_LORA
    lora = jnp.concatenate([_pad_rows(w_t[o3:o4], LORA_PAD), _pad_rows(w_t[o4:o5], LORA_PAD), w_t[o5:GATE_ROW0]],
                           axis=0)
    return _pad_rows(lora, 2 * SEC - HEAD_COLS)


def _pack_mu(mu):
    o3 = 3 * RWKV_WIDTH
    o4 = o3 + DECAY_LORA
    o5 = o4 + ICLR_LORA
    mu = mu[None, :]
    rw = jnp.concatenate([mu[:, :o3], _pad_cols(mu[:, o3:o4], LORA_PAD), _pad_cols(mu[:, o4:o5], LORA_PAD),
                          mu[:, o5:]], axis=1)
    return _pad_cols(rw, SEC)


def _pad_rows(t, rows):
    return jnp.pad(t, ((0, rows - t.shape[0]), (0, 0)))


def kernel(x, c, positions, w_ada, b_ada, norm_mix, norm_ffn, norm_final, w_in, b_gate, mu_shift, w0, w_decay_up, a0, w_iclr_up, w_gate_up, k_k, k_a, r_k, lnx_w, lnx_b, w_ret_out, w_rwkv_out, w_o, w_ffn_in, w_ffn_out):
    assert x.shape == (BATCH, SEQ, D_MODEL) and w_ada.shape[0] == 1
    x2 = x.reshape(TOKENS, D_MODEL)
    pos_col = positions.reshape(TOKENS, 1)

    mod = _ada(_pad_rows(c, 8), w_ada[0], b_ada)
    mod3 = mod[:BATCH].reshape(BATCH, 6, D_MODEL)

    w_t = jnp.swapaxes(w_in, 1, 2)[0]
    z = _inproj(_norm1(x2, mod3, norm_mix), w_t, _pack_lora_rows(w_t))
    y_ret = _retention(pos_col, z)

    par = jnp.concatenate([w0, a0, k_k, k_a, r_k.reshape(1, RWKV_WIDTH), lnx_w, lnx_b,
                           jnp.zeros((1, RWKV_WIDTH), F32)], axis=0)
    y_rw = _rwkv(z, _pack_mu(mu_shift[0]), par,
                 _pad_rows(w_decay_up[0], LORA_PAD).astype(BF16), _pad_rows(w_iclr_up[0], LORA_PAD).astype(BF16),
                 w_gate_up[0].astype(BF16))

    merged = _merge(y_ret, y_rw, w_ret_out[0], w_rwkv_out[0], z, b_gate)
    h1, u2 = _oproj(merged, w_o[0], x2, mod3, norm_ffn)
    act = _ffn_in(u2, w_ffn_in[0])
    out = _ffn_out(act, w_ffn_out[0].astype(BF16), h1, mod3, norm_final[None, :])
    return out.reshape(BATCH, SEQ, D_MODEL)
```

```python
import jax
import jax.numpy as jnp
from jax import lax
from jax.experimental import pallas as pl
from jax.experimental.pallas import tpu as pltpu

F32 = jnp.float32
BF16 = jnp.bfloat16

D_MODEL = 2048
BATCH = 2
SEQ = 4096
TOKENS = BATCH * SEQ

RET_HEADS = 4
RET_DIM = 256
RET_WIDTH = RET_HEADS * RET_DIM
RET_CHUNK = 128
ROPE_BASE = 10000.0

RWKV_HEAD = 64
RWKV_WIDTH = 1024
RWKV_HEADS = RWKV_WIDTH // RWKV_HEAD
DECAY_LORA = 96
ICLR_LORA = 96
GATE_LORA = 256
RWKV_CHUNK = 64
GROUP_LANES = 256
GROUP_HEADS = GROUP_LANES // RWKV_HEAD
RWKV_GROUPS = RWKV_WIDTH // GROUP_LANES
LORA_PAD = 128

FFN_HIDDEN = ((8 * D_MODEL // 3 + 255) // 256) * 256
RET_COLS = 4 * RET_WIDTH
SHIFT_COLS = 3 * RWKV_WIDTH + DECAY_LORA + ICLR_LORA + GATE_LORA
GATE_COLS = 2 * D_MODEL
SEC = 4096
HEAD_COLS = RET_COLS + 3 * RWKV_WIDTH
NORM_EPS = 1e-6
GN_EPS_RET = 1e-5
GN_EPS_RWKV = 64e-5

VMEM_LIMIT = 56 * 1024 * 1024
VMEM_LIMIT_MAX = 60 * 1024 * 1024
SUBLANES = 8


def _params(*sem, vmem=VMEM_LIMIT):
    return pltpu.CompilerParams(dimension_semantics=sem, vmem_limit_bytes=vmem)


def _bdot(a, b):
    return jnp.dot(a.astype(BF16), b.astype(BF16), preferred_element_type=F32)


def _bdot_nt(a, b):
    return lax.dot_general(a.astype(BF16), b.astype(BF16), (((1,), (1,)), ((), ())),
                           preferred_element_type=F32)


def _bdot_tn(a, b):
    return lax.dot_general(a.astype(BF16), b.astype(BF16), (((0,), (0,)), ((), ())),
                           preferred_element_type=F32)


def _silu(x):
    return x * jax.nn.sigmoid(x)


ADA_TN = 1024


def _ada_kernel(c_ref, w_ref, b_ref, o_ref):
    o_ref[...] = _bdot(_silu(c_ref[...]), w_ref[...]) + b_ref[...]


def _ada(c_pad, w_ada, b_ada):
    n = w_ada.shape[1]
    return pl.pallas_call(
        _ada_kernel,
        grid=(n // ADA_TN,),
        in_specs=[pl.BlockSpec((8, D_MODEL), lambda j: (0, 0)),
                  pl.BlockSpec((D_MODEL, ADA_TN), lambda j: (0, j)),
                  pl.BlockSpec((1, ADA_TN), lambda j: (0, j))],
        out_specs=pl.BlockSpec((8, ADA_TN), lambda j: (0, j)),
        out_shape=jax.ShapeDtypeStruct((8, n), F32),
        compiler_params=_params("arbitrary"),
        name="ada",
    )(c_pad, w_ada, b_ada)


IN_TM = 1024
IN_TN = 1024


def _modulated_norm(x, gain, scale, shift):
    y = x * lax.rsqrt(jnp.mean(x * x, axis=-1, keepdims=True) + NORM_EPS)
    return y * gain * (1.0 + scale) + shift


def _norm1_kernel(x_ref, mod_ref, gain_ref, u_ref):
    mod = mod_ref[0]
    u_ref[...] = _modulated_norm(x_ref[...], gain_ref[...], mod[1:2], mod[0:1]).astype(BF16)


def _norm1(x2, mod3, gain):
    blocks_per_batch = SEQ // IN_TM
    return pl.pallas_call(
        _norm1_kernel,
        grid=(TOKENS // IN_TM,),
        in_specs=[pl.BlockSpec((IN_TM, D_MODEL), lambda i: (i, 0)),
                  pl.BlockSpec((1, 6, D_MODEL), lambda i: (i // blocks_per_batch, 0, 0)),
                  pl.BlockSpec((1, D_MODEL), lambda i: (0, 0))],
        out_specs=pl.BlockSpec((IN_TM, D_MODEL), lambda i: (i, 0)),
        out_shape=jax.ShapeDtypeStruct((TOKENS, D_MODEL), BF16),
        compiler_params=_params("arbitrary"),
        name="norm1",
    )(x2, mod3, gain)


IN_LORA_BLOCK = HEAD_COLS // IN_TN
IN_RWKV_BLOCK0 = RET_COLS // IN_TN
GATE_ROW0 = RET_COLS + SHIFT_COLS


def _inproj_kernel(u_ref, w_ref, wl_ref, mu_ref, o_ref, wb_ref, carry_ref):
    j = pl.program_id(0)
    i = pl.program_id(1)

    @pl.when((i == 0) & (j != IN_LORA_BLOCK))
    def _():
        wb_ref[...] = w_ref[...].astype(BF16)

    @pl.when((i == 0) & (j == IN_LORA_BLOCK))
    def _():
        wb_ref[...] = wl_ref[...].astype(BF16)

    z = _bdot_nt(u_ref[...], wb_ref[...])

    @pl.when(j < IN_RWKV_BLOCK0)
    def _():
        o_ref[...] = z

    @pl.when(j >= IN_RWKV_BLOCK0)
    def _():
        @pl.when(i % (SEQ // IN_TM) == 0)
        def _():
            carry_ref[...] = jnp.zeros_like(carry_ref)

        row = lax.broadcasted_iota(jnp.int32, z.shape, 0)
        prev = jnp.where(row == 0, carry_ref[...], pltpu.roll(z, 1, 0))
        carry_ref[...] = z[IN_TM - 1:IN_TM, :]
        o_ref[...] = z + (prev - z) * mu_ref[...]


def _inproj(u, w_t, w_lora, mu_p):
    assert HEAD_COLS % IN_TN == 0 and 2 * SEC - HEAD_COLS == IN_TN
    return pl.pallas_call(
        _inproj_kernel,
        grid=(2 * SEC // IN_TN, TOKENS // IN_TM),
        in_specs=[pl.BlockSpec((IN_TM, D_MODEL), lambda j, i: (i, 0)),
                  pl.BlockSpec((IN_TN, D_MODEL), lambda j, i: (jnp.minimum(j, IN_LORA_BLOCK - 1), 0)),
                  pl.BlockSpec((IN_TN, D_MODEL), lambda j, i: (0, 0), pipeline_mode=pl.Buffered(1)),
                  pl.BlockSpec((1, IN_TN), lambda j, i: (0, jnp.maximum(j - IN_RWKV_BLOCK0, 0)))],
        out_specs=pl.BlockSpec((IN_TM, IN_TN), lambda j, i: (i, j)),
        out_shape=jax.ShapeDtypeStruct((TOKENS, 2 * SEC), F32),
        scratch_shapes=[pltpu.VMEM((IN_TN, D_MODEL), BF16), pltpu.VMEM((1, IN_TN), F32)],
        compiler_params=_params("arbitrary", "arbitrary"),
        name="inproj",
    )(u, w_t, w_lora, mu_p)


def _gateproj_kernel(u_ref, w_ref, b_ref, o_ref, wb_ref):
    @pl.when(pl.program_id(1) == 0)
    def _():
        wb_ref[...] = w_ref[...].astype(BF16)

    o_ref[...] = jax.nn.sigmoid(_bdot_nt(u_ref[...], wb_ref[...]) + b_ref[...]).astype(BF16)


def _gateproj(u, w_t, b_gate):
    def w_row(j, i):
        return ((GATE_ROW0 // SUBLANES + j * (IN_TN // SUBLANES)) * SUBLANES, 0)

    return pl.pallas_call(
        _gateproj_kernel,
        grid=(GATE_COLS // IN_TN, TOKENS // IN_TM),
        in_specs=[pl.BlockSpec((IN_TM, D_MODEL), lambda j, i: (i, 0)),
                  pl.BlockSpec((pl.Element(IN_TN), pl.Element(D_MODEL)), w_row),
                  pl.BlockSpec((1, IN_TN), lambda j, i: (0, j))],
        out_specs=pl.BlockSpec((IN_TM, IN_TN), lambda j, i: (i, j)),
        out_shape=jax.ShapeDtypeStruct((TOKENS, GATE_COLS), BF16),
        scratch_shapes=[pltpu.VMEM((IN_TN, D_MODEL), BF16)],
        compiler_params=_params("arbitrary", "arbitrary"),
        name="gateproj",
    )(u, w_t, b_gate)


def _ret_kernel(pos_ref, z_ref, invf_ref, dintra_ref, zeta_ref, xi_ref, cdec_ref, o_ref, state_ref):
    @pl.when(pl.program_id(1) == 0)
    def _():
        state_ref[...] = jnp.zeros_like(state_ref)

    ang = pos_ref[...].astype(F32) * invf_ref[...]
    cos = jnp.cos(ang)
    sin = jnp.sin(ang)
    half = RET_DIM // 2

    def rot(t):
        t1, t2 = t[:, :half], t[:, half:]
        return jnp.concatenate([t1 * cos - t2 * sin, t1 * sin + t2 * cos], axis=-1)

    for h in range(RET_HEADS):
        lo, hi = h * RET_DIM, (h + 1) * RET_DIM
        q = rot(z_ref[:, lo:hi])
        k = rot(z_ref[:, RET_WIDTH + lo:RET_WIDTH + hi]) * (RET_DIM ** -0.5)
        v = z_ref[:, 2 * RET_WIDTH + lo:2 * RET_WIDTH + hi].astype(BF16)
        g = z_ref[:, 3 * RET_WIDTH + lo:3 * RET_WIDTH + hi]
        qb = q.astype(BF16)
        scores = _bdot_nt(qb, k) * dintra_ref[h]
        state = state_ref[h]
        out = _bdot(scores, v) + _bdot(qb, state) * xi_ref[h]
        state_ref[h] = state * cdec_ref[h] + _bdot_tn(k * zeta_ref[h], v)
        mu = jnp.mean(out, axis=-1, keepdims=True)
        d = out - mu
        var = jnp.mean(d * d, axis=-1, keepdims=True)
        o_ref[:, lo:hi] = (_silu(g) * (d * lax.rsqrt(var + GN_EPS_RET))).astype(BF16)


def _retention_tables():
    h = RET_HEADS
    half = RET_DIM // 2
    inv_freq = ROPE_BASE ** (-jnp.arange(half, dtype=F32) / half)
    log_gamma = jnp.log(1.0 - 2.0 ** (-5.0 - jnp.arange(h, dtype=F32)))
    idx = jnp.arange(RET_CHUNK, dtype=F32)
    dist = idx[:, None] - idx[None, :]
    decay_intra = jnp.where(dist >= 0, jnp.exp(log_gamma[:, None, None] * jnp.maximum(dist, 0.0)), 0.0)
    zeta = jnp.exp(log_gamma[:, None] * (RET_CHUNK - 1.0 - idx))
    xi = jnp.exp(log_gamma[:, None] * (idx + 1.0))
    chunk_decay = jnp.exp(log_gamma * RET_CHUNK)
    wide = lambda t: jnp.broadcast_to(t[:, :, None], (h, RET_CHUNK, RET_DIM))
    cdec = jnp.broadcast_to(chunk_decay[:, None, None], (h, 1, RET_DIM))
    return inv_freq[None, :], decay_intra, wide(zeta), wide(xi), cdec


def _retention(pos_col, z):
    invf, dintra, zeta, xi, cdec = _retention_tables()
    nch = SEQ // RET_CHUNK
    full = lambda a: pl.BlockSpec(a.shape, lambda b, n: (0,) * a.ndim)
    return pl.pallas_call(
        _ret_kernel,
        grid=(BATCH, nch),
        in_specs=[pl.BlockSpec((RET_CHUNK, 1), lambda b, n: (b * nch + n, 0)),
                  pl.BlockSpec((RET_CHUNK, SEC), lambda b, n: (b * nch + n, 0)),
                  full(invf), full(dintra), full(zeta), full(xi), full(cdec)],
        out_specs=pl.BlockSpec((RET_CHUNK, RET_WIDTH), lambda b, n: (b * nch + n, 0)),
        out_shape=jax.ShapeDtypeStruct((TOKENS, RET_WIDTH), BF16),
        scratch_shapes=[pltpu.VMEM((RET_HEADS, RET_DIM, RET_DIM), F32)],
        compiler_params=_params("arbitrary", "arbitrary"),
        name="retention",
    )(pos_col, z, invf, dintra, zeta, xi, cdec)


def _split2(x):
    hi = x.astype(BF16)
    return hi, (x - hi.astype(F32)).astype(BF16)


def _head_sums(xs, ones_ref):
    c = RWKV_CHUNK
    pieces = []
    for x in xs:
        for p in _split2(x):
            for g in range(RWKV_GROUPS):
                pieces.append(p[:, g * GROUP_LANES:(g + 1) * GROUP_LANES])
    res = jnp.dot(jnp.concatenate(pieces, axis=0), ones_ref[...], preferred_element_type=F32)
    outs = []
    for i in range(len(xs)):
        base = i * 2 * RWKV_GROUPS * c
        cols = []
        for g in range(RWKV_GROUPS):
            hi = res[base + g * c:base + (g + 1) * c]
            lo = res[base + (RWKV_GROUPS + g) * c:base + (RWKV_GROUPS + g + 1) * c]
            cols.append(hi + lo)
        outs.append(jnp.concatenate(cols, axis=-1))
    return outs


def _rwkv_kernel(zs_ref, par_ref, wd_ref, wa_ref, wg_ref, ltri_ref, ones_ref, mstrict_ref, mincl_ref,
                 o_ref, s_ref):
    c = RWKV_CHUNK
    w = RWKV_WIDTH

    @pl.when(pl.program_id(1) == 0)
    def _():
        s_ref[...] = jnp.zeros_like(s_ref)

    r, kw, vw = zs_ref[:, :w], zs_ref[:, w:2 * w], zs_ref[:, 2 * w:3 * w]
    zw = zs_ref[:, 3 * w:3 * w + LORA_PAD]
    za = zs_ref[:, 3 * w + LORA_PAD:3 * w + 2 * LORA_PAD]
    zg = zs_ref[:, 3 * w + 2 * LORA_PAD:3 * w + 2 * LORA_PAD + GATE_LORA]
    par = par_ref[...]
    w0, a0, k_k, k_a, r_k, lnx_w, lnx_b = (par[i:i + 1] for i in range(7))

    dec_pre = w0 + _bdot(jnp.tanh(zw), wd_ref[...])
    softplus = jnp.maximum(-dec_pre, 0.0) + jnp.log1p(jnp.exp(-jnp.abs(dec_pre)))
    logdec = -jnp.exp(-softplus - 0.5)
    iclr = jax.nn.sigmoid(a0 + _bdot(za, wa_ref[...]))

    kk = kw * k_k
    k_mod = kw * (1.0 + (iclr - 1.0) * k_a)
    (kk_sq,) = _head_sums([kk * kk], ones_ref)
    kk = kk * (1.0 / jnp.maximum(jnp.sqrt(kk_sq), 1e-12))
    a = -kk
    b = kk * iclr

    ld_hi = logdec.astype(BF16)
    ld_mid, ld_lo = _split2(logdec - ld_hi.astype(F32))
    ltri = ltri_ref[...]
    cum = (jnp.dot(ltri, ld_hi, preferred_element_type=F32) + jnp.dot(ltri, ld_mid, preferred_element_type=F32)
           + jnp.dot(ltri, ld_lo, preferred_element_type=F32))
    cum_end = cum[c - 1:c, :]
    p_inv = jnp.exp(-cum)
    a_t = a * jnp.exp(cum - logdec)
    r_t = r * jnp.exp(cum)
    b_t = (b * p_inv).astype(BF16)
    k_t = (k_mod * p_inv).astype(BF16)

    lane_head = lax.shift_right_logical(lax.broadcasted_iota(jnp.int32, (c, GROUP_LANES), 1), 6)
    head_masks = [lane_head == h for h in range(GROUP_HEADS)]

    def stack(x):
        return jnp.concatenate([jnp.where(m, x, 0.0).astype(BF16) for m in head_masks], axis=0)

    def rep(x):
        return jnp.concatenate([x] * GROUP_HEADS, axis=0)

    m_strict = mstrict_ref[...]
    m_incl = mincl_ref[...]
    eye = (lax.broadcasted_iota(jnp.int32, (GROUP_LANES, GROUP_LANES), 0)
           == lax.broadcasted_iota(jnp.int32, (GROUP_LANES, GROUP_LANES), 1)).astype(F32)
    n4 = GROUP_HEADS * c
    groups = range(RWKV_GROUPS)
    sls = [slice(g * GROUP_LANES, (g + 1) * GROUP_LANES) for g in groups]
    a_s = [stack(a_t[:, sl]) for sl in sls]
    r_s = [stack(r_t[:, sl]) for sl in sls]
    blk = [_bdot_nt(jnp.concatenate([a_s[g], r_s[g]], axis=0),
                    jnp.concatenate([rep(b_t[:, sls[g]]), rep(k_t[:, sls[g]])], axis=0)) for g in groups]
    a_ab = [blk[g][:n4, :n4] * m_strict for g in groups]
    a_ak = [blk[g][:n4, n4:] * m_strict for g in groups]
    a_rb = [blk[g][n4:, :n4] * m_incl for g in groups]
    a_rk = [blk[g][n4:, n4:] * m_incl for g in groups]

    t_inv = [eye + a_ab[g] for g in groups]
    a_pow = a_ab

    def inverse_round():
        nonlocal a_pow, t_inv
        a_pow = [_bdot(a_pow[g], a_pow[g]) for g in groups]
        t_inv = [t_inv[g] + _bdot(t_inv[g], a_pow[g]) for g in groups]

    inverse_round()
    g_rw = _bdot(jax.nn.sigmoid(zg), wg_ref[...])
    (rk_sum,) = _head_sums([r * k_mod * r_k], ones_ref)
    bonus = rk_sum * vw
    inverse_round()
    p_end = jnp.exp(cum_end - cum)
    b_e = b * p_end
    k_e = k_mod * p_end
    decay_end = jnp.exp(cum_end)
    inverse_round()
    v_s = [stack(vw[:, sl]) for sl in sls]
    be_s = [stack(b_e[:, sl]) for sl in sls]
    inverse_round()
    ke_s = [stack(k_e[:, sl]) for sl in sls]
    inverse_round()

    a_eff = [_bdot(t_inv[g], a_s[g]) for g in groups]
    av = [_bdot(a_ak[g], v_s[g]) for g in groups]
    u_const = [_bdot(t_inv[g], av[g]) for g in groups]
    state = [s_ref[g] for g in groups]
    sb = [state[g].astype(BF16) for g in groups]
    u = [_bdot_nt(a_eff[g], sb[g]) + u_const[g] for g in groups]
    y_st = [_bdot_nt(r_s[g], sb[g]) + _bdot(a_rb[g], u[g]) + _bdot(a_rk[g], v_s[g]) for g in groups]
    for g in groups:
        s_ref[g] = (state[g] * decay_end[:, sls[g]]
                    + _bdot_tn(jnp.concatenate([u[g].astype(BF16), v_s[g]], axis=0),
                               jnp.concatenate([be_s[g], ke_s[g]], axis=0)))
    y = jnp.concatenate([t[:c] + t[c:2 * c] + t[2 * c:3 * c] + t[3 * c:] for t in y_st], axis=-1)

    (y_sum,) = _head_sums([y], ones_ref)
    d = y - y_sum * (1.0 / RWKV_HEAD)
    (d_sq,) = _head_sums([d * d], ones_ref)
    y = d * lax.rsqrt(d_sq * (1.0 / RWKV_HEAD) + GN_EPS_RWKV) * lnx_w + lnx_b + bonus
    o_ref[...] = (y * g_rw).astype(BF16)


def _rwkv_tables():
    c = RWKV_CHUNK
    ltri = (jnp.arange(c)[:, None] >= jnp.arange(c)[None, :]).astype(BF16)
    row = jnp.arange(GROUP_LANES)[:, None]
    col = jnp.arange(GROUP_LANES)[None, :]
    same_head = (row // RWKV_HEAD) == (col // RWKV_HEAD)
    ones = same_head.astype(BF16)
    m_strict = (same_head & (row > col)).astype(F32)
    m_incl = (same_head & (row >= col)).astype(F32)
    return ltri, ones, m_strict, m_incl


def _rwkv(z, par, wd, wa, wg):
    assert RWKV_CHUNK == RWKV_HEAD
    ltri, ones, m_strict, m_incl = _rwkv_tables()
    nch = SEQ // RWKV_CHUNK
    full = lambda a: pl.BlockSpec(a.shape, lambda b, n: (0,) * a.ndim)
    return pl.pallas_call(
        _rwkv_kernel,
        grid=(BATCH, nch),
        in_specs=[pl.BlockSpec((RWKV_CHUNK, SEC), lambda b, n: (b * nch + n, 1)),
                  full(par), full(wd), full(wa), full(wg),
                  full(ltri), full(ones), full(m_strict), full(m_incl)],
        out_specs=pl.BlockSpec((RWKV_CHUNK, RWKV_WIDTH), lambda b, n: (b * nch + n, 0)),
        out_shape=jax.ShapeDtypeStruct((TOKENS, RWKV_WIDTH), BF16),
        scratch_shapes=[pltpu.VMEM((RWKV_GROUPS, GROUP_LANES, GROUP_LANES), F32)],
        compiler_params=_params("arbitrary", "arbitrary"),
        name="rwkv",
    )(z, par, wd, wa, wg, ltri, ones, m_strict, m_incl)


MG_TM = 512
MG_TN = 1024


def _merge_kernel(yr_ref, yw_ref, wr_ref, ww_ref, gr_ref, gw_ref, o_ref, wrb_ref, wwb_ref):
    @pl.when(pl.program_id(1) == 0)
    def _():
        wrb_ref[...] = wr_ref[...].astype(BF16)
        wwb_ref[...] = ww_ref[...].astype(BF16)

    pr = jnp.dot(yr_ref[...], wrb_ref[...], preferred_element_type=F32)
    pw = jnp.dot(yw_ref[...], wwb_ref[...], preferred_element_type=F32)
    o_ref[...] = (gr_ref[...].astype(F32) * pr + gw_ref[...].astype(F32) * pw).astype(BF16)


def _merge(y_ret, y_rw, w_ret, w_rw, gates):
    nd = D_MODEL // MG_TN
    return pl.pallas_call(
        _merge_kernel,
        grid=(nd, TOKENS // MG_TM),
        in_specs=[pl.BlockSpec((MG_TM, RET_WIDTH), lambda j, i: (i, 0)),
                  pl.BlockSpec((MG_TM, RWKV_WIDTH), lambda j, i: (i, 0)),
                  pl.BlockSpec((RET_WIDTH, MG_TN), lambda j, i: (0, j)),
                  pl.BlockSpec((RWKV_WIDTH, MG_TN), lambda j, i: (0, j)),
                  pl.BlockSpec((MG_TM, MG_TN), lambda j, i: (i, j)),
                  pl.BlockSpec((MG_TM, MG_TN), lambda j, i: (i, nd + j))],
        out_specs=pl.BlockSpec((MG_TM, MG_TN), lambda j, i: (i, j)),
        out_shape=jax.ShapeDtypeStruct((TOKENS, D_MODEL), BF16),
        scratch_shapes=[pltpu.VMEM((RET_WIDTH, MG_TN), BF16), pltpu.VMEM((RWKV_WIDTH, MG_TN), BF16)],
        compiler_params=_params("arbitrary", "arbitrary"),
        name="merge",
    )(y_ret, y_rw, w_ret, w_rw, gates, gates)


OP_TM = 512


def _oproj_kernel(m_ref, w_ref, x_ref, mod_ref, gain_ref, h_ref, u_ref, wb_ref):
    @pl.when(pl.program_id(0) == 0)
    def _():
        wb_ref[...] = w_ref[...].astype(BF16)

    mod = mod_ref[0]
    h = x_ref[...] + mod[2:3] * jnp.dot(m_ref[...], wb_ref[...], preferred_element_type=F32)
    h_ref[...] = h
    u_ref[...] = _modulated_norm(h, gain_ref[...], mod[4:5], mod[3:4]).astype(BF16)


def _oproj(merged, w_o, x2, mod3, gain):
    blocks_per_batch = SEQ // OP_TM
    return pl.pallas_call(
        _oproj_kernel,
        grid=(TOKENS // OP_TM,),
        in_specs=[pl.BlockSpec((OP_TM, D_MODEL), lambda i: (i, 0)),
                  pl.BlockSpec((D_MODEL, D_MODEL), lambda i: (0, 0), pipeline_mode=pl.Buffered(1)),
                  pl.BlockSpec((OP_TM, D_MODEL), lambda i: (i, 0)),
                  pl.BlockSpec((1, 6, D_MODEL), lambda i: (i // blocks_per_batch, 0, 0)),
                  pl.BlockSpec((1, D_MODEL), lambda i: (0, 0))],
        out_specs=[pl.BlockSpec((OP_TM, D_MODEL), lambda i: (i, 0)),
                   pl.BlockSpec((OP_TM, D_MODEL), lambda i: (i, 0))],
        out_shape=[jax.ShapeDtypeStruct((TOKENS, D_MODEL), F32),
                   jax.ShapeDtypeStruct((TOKENS, D_MODEL), BF16)],
        scratch_shapes=[pltpu.VMEM((D_MODEL, D_MODEL), BF16)],
        compiler_params=_params("arbitrary"),
        name="oproj",
    )(merged, w_o, x2, mod3, gain)


FI_TM = 1024
FI_TN = 512


def _ffn_in_kernel(u_ref, wg_ref, wu_ref, o_ref, wgb_ref, wub_ref):
    @pl.when(pl.program_id(1) == 0)
    def _():
        wgb_ref[...] = wg_ref[...].astype(BF16)
        wub_ref[...] = wu_ref[...].astype(BF16)

    u = u_ref[...]
    gate = jnp.dot(u, wgb_ref[...], preferred_element_type=F32)
    up = jnp.dot(u, wub_ref[...], preferred_element_type=F32)
    o_ref[...] = (_silu(gate) * up).astype(BF16)


def _ffn_in(u2, w_in):
    nt = FFN_HIDDEN // FI_TN
    return pl.pallas_call(
        _ffn_in_kernel,
        grid=(nt, TOKENS // FI_TM),
        in_specs=[pl.BlockSpec((FI_TM, D_MODEL), lambda j, i: (i, 0)),
                  pl.BlockSpec((D_MODEL, FI_TN), lambda j, i: (0, j)),
                  pl.BlockSpec((D_MODEL, FI_TN), lambda j, i: (0, nt + j))],
        out_specs=pl.BlockSpec((FI_TM, FI_TN), lambda j, i: (i, j)),
        out_shape=jax.ShapeDtypeStruct((TOKENS, FFN_HIDDEN), BF16),
        scratch_shapes=[pltpu.VMEM((D_MODEL, FI_TN), BF16), pltpu.VMEM((D_MODEL, FI_TN), BF16)],
        compiler_params=_params("arbitrary", "arbitrary"),
        name="ffn_in",
    )(u2, w_in, w_in)


FO_TM = 1024
FO_TK = FFN_HIDDEN // 4


def _ffn_out_kernel(a_ref, w_ref, h_ref, mod_ref, gain_ref, o_ref, acc_ref):
    k = pl.program_id(1)

    @pl.when(k == 0)
    def _():
        acc_ref[...] = jnp.zeros_like(acc_ref)

    acc_ref[...] += jnp.dot(a_ref[...], w_ref[...], preferred_element_type=F32)

    @pl.when(k == pl.num_programs(1) - 1)
    def _():
        h = h_ref[...] + mod_ref[0][5:6] * acc_ref[...]
        o_ref[...] = h * lax.rsqrt(jnp.mean(h * h, axis=-1, keepdims=True) + NORM_EPS) * gain_ref[...]


def _ffn_out(act, w_out, h1, mod3, gain):
    blocks_per_batch = SEQ // FO_TM
    return pl.pallas_call(
        _ffn_out_kernel,
        grid=(TOKENS // FO_TM, FFN_HIDDEN // FO_TK),
        in_specs=[pl.BlockSpec((FO_TM, FO_TK), lambda i, k: (i, k)),
                  pl.BlockSpec((FO_TK, D_MODEL), lambda i, k: (k, 0)),
                  pl.BlockSpec((FO_TM, D_MODEL), lambda i, k: (i, 0), pipeline_mode=pl.Buffered(1)),
                  pl.BlockSpec((1, 6, D_MODEL), lambda i, k: (i // blocks_per_batch, 0, 0)),
                  pl.BlockSpec((1, D_MODEL), lambda i, k: (0, 0))],
        out_specs=pl.BlockSpec((FO_TM, D_MODEL), lambda i, k: (i, 0)),
        out_shape=jax.ShapeDtypeStruct((TOKENS, D_MODEL), F32),
        scratch_shapes=[pltpu.VMEM((FO_TM, D_MODEL), F32)],
        compiler_params=_params("arbitrary", "arbitrary", vmem=VMEM_LIMIT_MAX),
        name="ffn_out",
    )(act, w_out, h1, mod3, gain)


def _pad_cols(t, width):
    return jnp.pad(t, ((0, 0), (0, width - t.shape[1])))


def _pad_rows(t, rows):
    return jnp.pad(t, ((0, rows - t.shape[0]), (0, 0)))


def _pack_lora_rows(w_t):
    o3 = HEAD_COLS
    o4 = o3 + DECAY_LORA
    o5 = o4 + ICLR_LORA
    lora = jnp.concatenate([_pad_rows(w_t[o3:o4], LORA_PAD), _pad_rows(w_t[o4:o5], LORA_PAD), w_t[o5:GATE_ROW0]],
                           axis=0)
    return _pad_rows(lora, 2 * SEC - HEAD_COLS)


def _pack_mu(mu):
    o3 = 3 * RWKV_WIDTH
    o4 = o3 + DECAY_LORA
    o5 = o4 + ICLR_LORA
    mu = mu[None, :]
    rw = jnp.concatenate([mu[:, :o3], _pad_cols(mu[:, o3:o4], LORA_PAD), _pad_cols(mu[:, o4:o5], LORA_PAD),
                          mu[:, o5:]], axis=1)
    return _pad_cols(rw, SEC)


def kernel(x, c, positions, w_ada, b_ada, norm_mix, norm_ffn, norm_final, w_in, b_gate, mu_shift, w0, w_decay_up, a0, w_iclr_up, w_gate_up, k_k, k_a, r_k, lnx_w, lnx_b, w_ret_out, w_rwkv_out, w_o, w_ffn_in, w_ffn_out):
    assert x.shape == (BATCH, SEQ, D_MODEL) and w_ada.shape[0] == 1
    x2 = x.reshape(TOKENS, D_MODEL)
    pos_col = positions.reshape(TOKENS, 1)

    mod = _ada(_pad_rows(c, 8), w_ada[0], b_ada)
    mod3 = mod[:BATCH].reshape(BATCH, 6, D_MODEL)

    w_t = jnp.swapaxes(w_in, 1, 2)[0]
    u1 = _norm1(x2, mod3, norm_mix)
    z = _inproj(u1, w_t, _pack_lora_rows(w_t), _pack_mu(mu_shift[0]))
    gates = _gateproj(u1, w_t, b_gate)
    y_ret = _retention(pos_col, z)

    par = jnp.concatenate([w0, a0, k_k, k_a, r_k.reshape(1, RWKV_WIDTH), lnx_w, lnx_b,
                           jnp.zeros((1, RWKV_WIDTH), F32)], axis=0)
    y_rw = _rwkv(z, par, _pad_rows(w_decay_up[0], LORA_PAD).astype(BF16),
                 _pad_rows(w_iclr_up[0], LORA_PAD).astype(BF16), w_gate_up[0].astype(BF16))

    merged = _merge(y_ret, y_rw, w_ret_out[0], w_rwkv_out[0], gates)
    h1, u2 = _oproj(merged, w_o[0], x2, mod3, norm_ffn)
    act = _ffn_in(u2, w_ffn_in[0])
    out = _ffn_out(act, w_ffn_out[0].astype(BF16), h1, mod3, norm_final[None, :])
    return out.reshape(BATCH, SEQ, D_MODEL)
```

```python
import jax
import jax.numpy as jnp
from jax import lax
from jax.experimental import pallas as pl
from jax.experimental.pallas import tpu as pltpu

F32 = jnp.float32
BF16 = jnp.bfloat16

D_MODEL = 2048
BATCH = 2
SEQ = 4096
TOKENS = BATCH * SEQ

RET_HEADS = 4
RET_DIM = 256
RET_WIDTH = RET_HEADS * RET_DIM
RET_CHUNK = 128
ROPE_BASE = 10000.0

RWKV_HEAD = 64
RWKV_WIDTH = 1024
RWKV_HEADS = RWKV_WIDTH // RWKV_HEAD
DECAY_LORA = 96
ICLR_LORA = 96
GATE_LORA = 256
RWKV_CHUNK = 64
GROUP_LANES = 256
GROUP_HEADS = GROUP_LANES // RWKV_HEAD
RWKV_GROUPS = RWKV_WIDTH // GROUP_LANES
LORA_PAD = 128

FFN_HIDDEN = ((8 * D_MODEL // 3 + 255) // 256) * 256
RET_COLS = 4 * RET_WIDTH
SHIFT_COLS = 3 * RWKV_WIDTH + DECAY_LORA + ICLR_LORA + GATE_LORA
GATE_COLS = 2 * D_MODEL
SEC = 4096
Z_COLS = 3 * SEC
HEAD_COLS = RET_COLS + 3 * RWKV_WIDTH
NORM_EPS = 1e-6
GN_EPS_RET = 1e-5
GN_EPS_RWKV = 64e-5

VMEM_LIMIT = 56 * 1024 * 1024
SUBLANES = 8


def _params(*sem, vmem=VMEM_LIMIT):
    return pltpu.CompilerParams(dimension_semantics=sem, vmem_limit_bytes=vmem)


def _bdot(a, b):
    return jnp.dot(a.astype(BF16), b.astype(BF16), preferred_element_type=F32)


def _bdot_nt(a, b):
    return lax.dot_general(a.astype(BF16), b.astype(BF16), (((1,), (1,)), ((), ())),
                           preferred_element_type=F32)


def _bdot_tn(a, b):
    return lax.dot_general(a.astype(BF16), b.astype(BF16), (((0,), (0,)), ((), ())),
                           preferred_element_type=F32)


def _silu(x):
    return x * jax.nn.sigmoid(x)


ADA_TN = 1024


def _ada_kernel(c_ref, w_ref, b_ref, o_ref):
    o_ref[...] = _bdot(_silu(c_ref[...]), w_ref[...]) + b_ref[...]


def _ada(c_pad, w_ada, b_ada):
    n = w_ada.shape[1]
    return pl.pallas_call(
        _ada_kernel,
        grid=(n // ADA_TN,),
        in_specs=[pl.BlockSpec((8, D_MODEL), lambda j: (0, 0)),
                  pl.BlockSpec((D_MODEL, ADA_TN), lambda j: (0, j)),
                  pl.BlockSpec((1, ADA_TN), lambda j: (0, j))],
        out_specs=pl.BlockSpec((8, ADA_TN), lambda j: (0, j)),
        out_shape=jax.ShapeDtypeStruct((8, n), F32),
        compiler_params=_params("arbitrary"),
        name="ada",
    )(c_pad, w_ada, b_ada)


IN_TM = 1024
IN_TN = 1024


def _modulated_norm(x, gain, scale, shift):
    y = x * lax.rsqrt(jnp.mean(x * x, axis=-1, keepdims=True) + NORM_EPS)
    return y * gain * (1.0 + scale) + shift


def _norm1_kernel(x_ref, mod_ref, gain_ref, u_ref):
    mod = mod_ref[0]
    u_ref[...] = _modulated_norm(x_ref[...], gain_ref[...], mod[1:2], mod[0:1]).astype(BF16)


def _norm1(x2, mod3, gain):
    blocks_per_batch = SEQ // IN_TM
    return pl.pallas_call(
        _norm1_kernel,
        grid=(TOKENS // IN_TM,),
        in_specs=[pl.BlockSpec((IN_TM, D_MODEL), lambda i: (i, 0)),
                  pl.BlockSpec((1, 6, D_MODEL), lambda i: (i // blocks_per_batch, 0, 0)),
                  pl.BlockSpec((1, D_MODEL), lambda i: (0, 0))],
        out_specs=pl.BlockSpec((IN_TM, D_MODEL), lambda i: (i, 0)),
        out_shape=jax.ShapeDtypeStruct((TOKENS, D_MODEL), BF16),
        compiler_params=_params("arbitrary"),
        name="norm1",
    )(x2, mod3, gain)


IN_LORA_BLOCK = HEAD_COLS // IN_TN
IN_RWKV_BLOCK0 = RET_COLS // IN_TN
GATE_ROW0 = RET_COLS + SHIFT_COLS


def _inproj_kernel(u_ref, w_ref, wl_ref, mu_ref, o_ref, wb_ref, carry_ref):
    j = pl.program_id(0)
    i = pl.program_id(1)

    @pl.when((i == 0) & (j != IN_LORA_BLOCK))
    def _():
        wb_ref[...] = w_ref[...].astype(BF16)

    @pl.when((i == 0) & (j == IN_LORA_BLOCK))
    def _():
        wb_ref[...] = wl_ref[...].astype(BF16)

    z = _bdot_nt(u_ref[...], wb_ref[...])
    shifted = (j >= IN_RWKV_BLOCK0) & (j <= IN_LORA_BLOCK)

    @pl.when(jnp.logical_not(shifted))
    def _():
        o_ref[...] = z

    @pl.when(shifted)
    def _():
        @pl.when(i % (SEQ // IN_TM) == 0)
        def _():
            carry_ref[...] = jnp.zeros_like(carry_ref)

        row = lax.broadcasted_iota(jnp.int32, z.shape, 0)
        prev = jnp.where(row == 0, carry_ref[...], pltpu.roll(z, 1, 0))
        carry_ref[...] = z[IN_TM - 1:IN_TM, :]
        o_ref[...] = z + (prev - z) * mu_ref[...]


def _inproj(u, w_t, w_lora, mu_p):
    assert HEAD_COLS % IN_TN == 0 and 2 * SEC - HEAD_COLS == IN_TN

    def w_row(j, i):
        tn, g0 = IN_TN // SUBLANES, GATE_ROW0 // SUBLANES
        head = jnp.minimum(j, IN_LORA_BLOCK - 1) * tn
        return (jnp.where(j > IN_LORA_BLOCK, g0 + (j - IN_LORA_BLOCK - 1) * tn, head) * SUBLANES, 0)

    return pl.pallas_call(
        _inproj_kernel,
        grid=(Z_COLS // IN_TN, TOKENS // IN_TM),
        in_specs=[pl.BlockSpec((IN_TM, D_MODEL), lambda j, i: (i, 0)),
                  pl.BlockSpec((pl.Element(IN_TN), pl.Element(D_MODEL)), w_row),
                  pl.BlockSpec((IN_TN, D_MODEL), lambda j, i: (0, 0), pipeline_mode=pl.Buffered(1)),
                  pl.BlockSpec((1, IN_TN), lambda j, i: (0, jnp.clip(j - IN_RWKV_BLOCK0, 0, SEC // IN_TN - 1)))],
        out_specs=pl.BlockSpec((IN_TM, IN_TN), lambda j, i: (i, j)),
        out_shape=jax.ShapeDtypeStruct((TOKENS, Z_COLS), F32),
        scratch_shapes=[pltpu.VMEM((IN_TN, D_MODEL), BF16), pltpu.VMEM((1, IN_TN), F32)],
        compiler_params=_params("arbitrary", "arbitrary"),
        name="inproj",
    )(u, w_t, w_lora, mu_p)


def _ret_kernel(pos_ref, z_ref, invf_ref, dintra_ref, zeta_ref, xi_ref, cdec_ref, o_ref, state_ref):
    @pl.when(pl.program_id(1) == 0)
    def _():
        state_ref[...] = jnp.zeros_like(state_ref)

    ang = pos_ref[...].astype(F32) * invf_ref[...]
    cos = jnp.cos(ang)
    sin = jnp.sin(ang)
    half = RET_DIM // 2

    def rot(t):
        t1, t2 = t[:, :half], t[:, half:]
        return jnp.concatenate([t1 * cos - t2 * sin, t1 * sin + t2 * cos], axis=-1)

    heads = range(RET_HEADS)
    cols = lambda section, h: slice(section * RET_WIDTH + h * RET_DIM, section * RET_WIDTH + (h + 1) * RET_DIM)
    qb = [rot(z_ref[:, cols(0, h)]).astype(BF16) for h in heads]
    k = [rot(z_ref[:, cols(1, h)]) * (RET_DIM ** -0.5) for h in heads]
    v = [z_ref[:, cols(2, h)].astype(BF16) for h in heads]
    scores = [_bdot_nt(qb[h], k[h]) * dintra_ref[h] for h in heads]
    state = [state_ref[h] for h in heads]
    out = [_bdot(scores[h], v[h]) + _bdot(qb[h], state[h]) * xi_ref[h] for h in heads]
    for h in heads:
        state_ref[h] = state[h] * cdec_ref[h] + _bdot_tn(k[h] * zeta_ref[h], v[h])
    for h in heads:
        mu = jnp.mean(out[h], axis=-1, keepdims=True)
        d = out[h] - mu
        var = jnp.mean(d * d, axis=-1, keepdims=True)
        g = z_ref[:, cols(3, h)]
        o_ref[:, cols(0, h)] = (_silu(g) * (d * lax.rsqrt(var + GN_EPS_RET))).astype(BF16)


def _retention_tables():
    h = RET_HEADS
    half = RET_DIM // 2
    inv_freq = ROPE_BASE ** (-jnp.arange(half, dtype=F32) / half)
    log_gamma = jnp.log(1.0 - 2.0 ** (-5.0 - jnp.arange(h, dtype=F32)))
    idx = jnp.arange(RET_CHUNK, dtype=F32)
    dist = idx[:, None] - idx[None, :]
    decay_intra = jnp.where(dist >= 0, jnp.exp(log_gamma[:, None, None] * jnp.maximum(dist, 0.0)), 0.0)
    zeta = jnp.exp(log_gamma[:, None] * (RET_CHUNK - 1.0 - idx))
    xi = jnp.exp(log_gamma[:, None] * (idx + 1.0))
    chunk_decay = jnp.exp(log_gamma * RET_CHUNK)
    wide = lambda t: jnp.broadcast_to(t[:, :, None], (h, RET_CHUNK, RET_DIM))
    cdec = jnp.broadcast_to(chunk_decay[:, None, None], (h, 1, RET_DIM))
    return inv_freq[None, :], decay_intra, wide(zeta), wide(xi), cdec


def _retention(pos_col, z):
    invf, dintra, zeta, xi, cdec = _retention_tables()
    nch = SEQ // RET_CHUNK
    full = lambda a: pl.BlockSpec(a.shape, lambda b, n: (0,) * a.ndim)
    return pl.pallas_call(
        _ret_kernel,
        grid=(BATCH, nch),
        in_specs=[pl.BlockSpec((RET_CHUNK, 1), lambda b, n: (b * nch + n, 0)),
                  pl.BlockSpec((RET_CHUNK, SEC), lambda b, n: (b * nch + n, 0)),
                  full(invf), full(dintra), full(zeta), full(xi), full(cdec)],
        out_specs=pl.BlockSpec((RET_CHUNK, RET_WIDTH), lambda b, n: (b * nch + n, 0)),
        out_shape=jax.ShapeDtypeStruct((TOKENS, RET_WIDTH), BF16),
        scratch_shapes=[pltpu.VMEM((RET_HEADS, RET_DIM, RET_DIM), F32)],
        compiler_params=_params("arbitrary", "arbitrary"),
        name="retention",
    )(pos_col, z, invf, dintra, zeta, xi, cdec)


def _split2(x):
    hi = x.astype(BF16)
    return hi, (x - hi.astype(F32)).astype(BF16)


def _head_sums(xs, ones_ref):
    c = RWKV_CHUNK
    pieces = []
    for x in xs:
        for p in _split2(x):
            for g in range(RWKV_GROUPS):
                pieces.append(p[:, g * GROUP_LANES:(g + 1) * GROUP_LANES])
    res = jnp.dot(jnp.concatenate(pieces, axis=0), ones_ref[...], preferred_element_type=F32)
    outs = []
    for i in range(len(xs)):
        base = i * 2 * RWKV_GROUPS * c
        cols = []
        for g in range(RWKV_GROUPS):
            hi = res[base + g * c:base + (g + 1) * c]
            lo = res[base + (RWKV_GROUPS + g) * c:base + (RWKV_GROUPS + g + 1) * c]
            cols.append(hi + lo)
        outs.append(jnp.concatenate(cols, axis=-1))
    return outs


def _rwkv_kernel(zs_ref, par_ref, wd_ref, wa_ref, wg_ref, ltri_ref, ones_ref, mstrict_ref, mincl_ref,
                 o_ref, s_ref):
    c = RWKV_CHUNK
    w = RWKV_WIDTH

    @pl.when(pl.program_id(1) == 0)
    def _():
        s_ref[...] = jnp.zeros_like(s_ref)

    r, kw, vw = zs_ref[:, :w], zs_ref[:, w:2 * w], zs_ref[:, 2 * w:3 * w]
    zw = zs_ref[:, 3 * w:3 * w + LORA_PAD]
    za = zs_ref[:, 3 * w + LORA_PAD:3 * w + 2 * LORA_PAD]
    zg = zs_ref[:, 3 * w + 2 * LORA_PAD:3 * w + 2 * LORA_PAD + GATE_LORA]
    par = par_ref[...]
    w0, a0, k_k, k_a, r_k, lnx_w, lnx_b = (par[i:i + 1] for i in range(7))

    dec_pre = w0 + _bdot(jnp.tanh(zw), wd_ref[...])
    softplus = jnp.maximum(-dec_pre, 0.0) + jnp.log1p(jnp.exp(-jnp.abs(dec_pre)))
    logdec = -jnp.exp(-softplus - 0.5)
    iclr = jax.nn.sigmoid(a0 + _bdot(za, wa_ref[...]))

    kk = kw * k_k
    k_mod = kw * (1.0 + (iclr - 1.0) * k_a)
    (kk_sq,) = _head_sums([kk * kk], ones_ref)
    kk = kk * (1.0 / jnp.maximum(jnp.sqrt(kk_sq), 1e-12))
    a = -kk
    b = kk * iclr

    ld_hi = logdec.astype(BF16)
    ld_mid, ld_lo = _split2(logdec - ld_hi.astype(F32))
    ltri = ltri_ref[...]
    cum = (jnp.dot(ltri, ld_hi, preferred_element_type=F32) + jnp.dot(ltri, ld_mid, preferred_element_type=F32)
           + jnp.dot(ltri, ld_lo, preferred_element_type=F32))
    cum_end = cum[c - 1:c, :]
    p_inv = jnp.exp(-cum)
    a_t = a * jnp.exp(cum - logdec)
    r_t = r * jnp.exp(cum)
    b_t = (b * p_inv).astype(BF16)
    k_t = (k_mod * p_inv).astype(BF16)

    lane_head = lax.shift_right_logical(lax.broadcasted_iota(jnp.int32, (c, GROUP_LANES), 1), 6)
    head_masks = [lane_head == h for h in range(GROUP_HEADS)]

    def stack(x):
        return jnp.concatenate([jnp.where(m, x, 0.0).astype(BF16) for m in head_masks], axis=0)

    def rep(x):
        return jnp.concatenate([x] * GROUP_HEADS, axis=0)

    m_strict = mstrict_ref[...]
    m_incl = mincl_ref[...]
    eye = (lax.broadcasted_iota(jnp.int32, (GROUP_LANES, GROUP_LANES), 0)
           == lax.broadcasted_iota(jnp.int32, (GROUP_LANES, GROUP_LANES), 1)).astype(F32)
    n4 = GROUP_HEADS * c
    groups = range(RWKV_GROUPS)
    sls = [slice(g * GROUP_LANES, (g + 1) * GROUP_LANES) for g in groups]
    a_s = [stack(a_t[:, sl]) for sl in sls]
    r_s = [stack(r_t[:, sl]) for sl in sls]
    blk = [_bdot_nt(jnp.concatenate([a_s[g], r_s[g]], axis=0),
                    jnp.concatenate([rep(b_t[:, sls[g]]), rep(k_t[:, sls[g]])], axis=0)) for g in groups]
    a_ab = [blk[g][:n4, :n4] * m_strict for g in groups]
    a_ak = [blk[g][:n4, n4:] * m_strict for g in groups]
    a_rb = [blk[g][n4:, :n4] * m_incl for g in groups]
    a_rk = [blk[g][n4:, n4:] * m_incl for g in groups]

    t_inv = [eye + a_ab[g] for g in groups]
    a_pow = a_ab

    def inverse_round():
        nonlocal a_pow, t_inv
        a_pow = [_bdot(a_pow[g], a_pow[g]) for g in groups]
        t_inv = [t_inv[g] + _bdot(t_inv[g], a_pow[g]) for g in groups]

    inverse_round()
    g_rw = _bdot(jax.nn.sigmoid(zg), wg_ref[...])
    (rk_sum,) = _head_sums([r * k_mod * r_k], ones_ref)
    bonus = rk_sum * vw
    inverse_round()
    p_end = jnp.exp(cum_end - cum)
    b_e = b * p_end
    k_e = k_mod * p_end
    decay_end = jnp.exp(cum_end)
    inverse_round()
    v_s = [stack(vw[:, sl]) for sl in sls]
    be_s = [stack(b_e[:, sl]) for sl in sls]
    inverse_round()
    ke_s = [stack(k_e[:, sl]) for sl in sls]
    inverse_round()

    a_eff = [_bdot(t_inv[g], a_s[g]) for g in groups]
    av = [_bdot(a_ak[g], v_s[g]) for g in groups]
    u_const = [_bdot(t_inv[g], av[g]) for g in groups]
    state = [s_ref[g] for g in groups]
    sb = [state[g].astype(BF16) for g in groups]
    u = [_bdot_nt(a_eff[g], sb[g]) + u_const[g] for g in groups]
    y_st = [_bdot_nt(r_s[g], sb[g]) + _bdot(a_rb[g], u[g]) + _bdot(a_rk[g], v_s[g]) for g in groups]
    for g in groups:
        s_ref[g] = (state[g] * decay_end[:, sls[g]]
                    + _bdot_tn(jnp.concatenate([u[g].astype(BF16), v_s[g]], axis=0),
                               jnp.concatenate([be_s[g], ke_s[g]], axis=0)))
    y = jnp.concatenate([t[:c] + t[c:2 * c] + t[2 * c:3 * c] + t[3 * c:] for t in y_st], axis=-1)

    (y_sum,) = _head_sums([y], ones_ref)
    d = y - y_sum * (1.0 / RWKV_HEAD)
    (d_sq,) = _head_sums([d * d], ones_ref)
    y = d * lax.rsqrt(d_sq * (1.0 / RWKV_HEAD) + GN_EPS_RWKV) * lnx_w + lnx_b + bonus
    o_ref[...] = (y * g_rw).astype(BF16)


def _rwkv_tables():
    c = RWKV_CHUNK
    ltri = (jnp.arange(c)[:, None] >= jnp.arange(c)[None, :]).astype(BF16)
    row = jnp.arange(GROUP_LANES)[:, None]
    col = jnp.arange(GROUP_LANES)[None, :]
    same_head = (row // RWKV_HEAD) == (col // RWKV_HEAD)
    ones = same_head.astype(BF16)
    m_strict = (same_head & (row > col)).astype(F32)
    m_incl = (same_head & (row >= col)).astype(F32)
    return ltri, ones, m_strict, m_incl


def _rwkv(z, par, wd, wa, wg):
    assert RWKV_CHUNK == RWKV_HEAD
    ltri, ones, m_strict, m_incl = _rwkv_tables()
    nch = SEQ // RWKV_CHUNK
    full = lambda a: pl.BlockSpec(a.shape, lambda b, n: (0,) * a.ndim)
    return pl.pallas_call(
        _rwkv_kernel,
        grid=(BATCH, nch),
        in_specs=[pl.BlockSpec((RWKV_CHUNK, SEC), lambda b, n: (b * nch + n, 1)),
                  full(par), full(wd), full(wa), full(wg),
                  full(ltri), full(ones), full(m_strict), full(m_incl)],
        out_specs=pl.BlockSpec((RWKV_CHUNK, RWKV_WIDTH), lambda b, n: (b * nch + n, 0)),
        out_shape=jax.ShapeDtypeStruct((TOKENS, RWKV_WIDTH), BF16),
        scratch_shapes=[pltpu.VMEM((RWKV_GROUPS, GROUP_LANES, GROUP_LANES), F32)],
        compiler_params=_params("arbitrary", "arbitrary"),
        name="rwkv",
    )(z, par, wd, wa, wg, ltri, ones, m_strict, m_incl)


MG_TM = 1024
MG_TN = 1024


def _merge_kernel(yr_ref, yw_ref, wr_ref, ww_ref, gr_ref, gw_ref, br_ref, bw_ref, o_ref, wrb_ref, wwb_ref):
    @pl.when(pl.program_id(1) == 0)
    def _():
        wrb_ref[...] = wr_ref[...].astype(BF16)
        wwb_ref[...] = ww_ref[...].astype(BF16)

    pr = jnp.dot(yr_ref[...], wrb_ref[...], preferred_element_type=F32)
    pw = jnp.dot(yw_ref[...], wwb_ref[...], preferred_element_type=F32)
    o = jax.nn.sigmoid(gr_ref[...] + br_ref[...]) * pr + jax.nn.sigmoid(gw_ref[...] + bw_ref[...]) * pw
    o_ref[...] = o.astype(BF16)


def _merge(y_ret, y_rw, w_ret, w_rw, z, b_gate):
    g0 = 2 * SEC // MG_TN
    nd = D_MODEL // MG_TN
    return pl.pallas_call(
        _merge_kernel,
        grid=(nd, TOKENS // MG_TM),
        in_specs=[pl.BlockSpec((MG_TM, RET_WIDTH), lambda j, i: (i, 0)),
                  pl.BlockSpec((MG_TM, RWKV_WIDTH), lambda j, i: (i, 0)),
                  pl.BlockSpec((RET_WIDTH, MG_TN), lambda j, i: (0, j)),
                  pl.BlockSpec((RWKV_WIDTH, MG_TN), lambda j, i: (0, j)),
                  pl.BlockSpec((MG_TM, MG_TN), lambda j, i: (i, g0 + j)),
                  pl.BlockSpec((MG_TM, MG_TN), lambda j, i: (i, g0 + nd + j)),
                  pl.BlockSpec((1, MG_TN), lambda j, i: (0, j)),
                  pl.BlockSpec((1, MG_TN), lambda j, i: (0, nd + j))],
        out_specs=pl.BlockSpec((MG_TM, MG_TN), lambda j, i: (i, j)),
        out_shape=jax.ShapeDtypeStruct((TOKENS, D_MODEL), BF16),
        scratch_shapes=[pltpu.VMEM((RET_WIDTH, MG_TN), BF16), pltpu.VMEM((RWKV_WIDTH, MG_TN), BF16)],
        compiler_params=_params("arbitrary", "arbitrary"),
        name="merge",
    )(y_ret, y_rw, w_ret, w_rw, z, z, b_gate, b_gate)


OP_TM = 512


def _oproj_kernel(m_ref, w_ref, x_ref, mod_ref, gain_ref, h_ref, u_ref, wb_ref):
    @pl.when(pl.program_id(0) == 0)
    def _():
        wb_ref[...] = w_ref[...].astype(BF16)

    mod = mod_ref[0]
    h = x_ref[...] + mod[2:3] * jnp.dot(m_ref[...], wb_ref[...], preferred_element_type=F32)
    h_ref[...] = h
    u_ref[...] = _modulated_norm(h, gain_ref[...], mod[4:5], mod[3:4]).astype(BF16)


def _oproj(merged, w_o, x2, mod3, gain):
    blocks_per_batch = SEQ // OP_TM
    return pl.pallas_call(
        _oproj_kernel,
        grid=(TOKENS // OP_TM,),
        in_specs=[pl.BlockSpec((OP_TM, D_MODEL), lambda i: (i, 0)),
                  pl.BlockSpec((D_MODEL, D_MODEL), lambda i: (0, 0), pipeline_mode=pl.Buffered(1)),
                  pl.BlockSpec((OP_TM, D_MODEL), lambda i: (i, 0)),
                  pl.BlockSpec((1, 6, D_MODEL), lambda i: (i // blocks_per_batch, 0, 0)),
                  pl.BlockSpec((1, D_MODEL), lambda i: (0, 0))],
        out_specs=[pl.BlockSpec((OP_TM, D_MODEL), lambda i: (i, 0)),
                   pl.BlockSpec((OP_TM, D_MODEL), lambda i: (i, 0))],
        out_shape=[jax.ShapeDtypeStruct((TOKENS, D_MODEL), F32),
                   jax.ShapeDtypeStruct((TOKENS, D_MODEL), BF16)],
        scratch_shapes=[pltpu.VMEM((D_MODEL, D_MODEL), BF16)],
        compiler_params=_params("arbitrary"),
        name="oproj",
    )(merged, w_o, x2, mod3, gain)


FI_TM = 1024
FI_TN = 512


def _ffn_in_kernel(u_ref, wg_ref, wu_ref, o_ref, wgb_ref, wub_ref):
    @pl.when(pl.program_id(1) == 0)
    def _():
        wgb_ref[...] = wg_ref[...].astype(BF16)
        wub_ref[...] = wu_ref[...].astype(BF16)

    u = u_ref[...]
    gate = jnp.dot(u, wgb_ref[...], preferred_element_type=F32)
    up = jnp.dot(u, wub_ref[...], preferred_element_type=F32)
    o_ref[...] = (_silu(gate) * up).astype(BF16)


def _ffn_in(u2, w_in):
    nt = FFN_HIDDEN // FI_TN
    return pl.pallas_call(
        _ffn_in_kernel,
        grid=(nt, TOKENS // FI_TM),
        in_specs=[pl.BlockSpec((FI_TM, D_MODEL), lambda j, i: (i, 0)),
                  pl.BlockSpec((D_MODEL, FI_TN), lambda j, i: (0, j)),
                  pl.BlockSpec((D_MODEL, FI_TN), lambda j, i: (0, nt + j))],
        out_specs=pl.BlockSpec((FI_TM, FI_TN), lambda j, i: (i, j)),
        out_shape=jax.ShapeDtypeStruct((TOKENS, FFN_HIDDEN), BF16),
        scratch_shapes=[pltpu.VMEM((D_MODEL, FI_TN), BF16), pltpu.VMEM((D_MODEL, FI_TN), BF16)],
        compiler_params=_params("arbitrary", "arbitrary"),
        name="ffn_in",
    )(u2, w_in, w_in)


FO_TM = 1024
FO_TK = 512


def _ffn_out_kernel(a_ref, w_ref, h_ref, mod_ref, gain_ref, o_ref, acc_ref):
    k = pl.program_id(1)

    @pl.when(k == 0)
    def _():
        acc_ref[...] = jnp.zeros_like(acc_ref)

    acc_ref[...] += jnp.dot(a_ref[...], w_ref[...], preferred_element_type=F32)

    @pl.when(k == pl.num_programs(1) - 1)
    def _():
        h = h_ref[...] + mod_ref[0][5:6] * acc_ref[...]
        o_ref[...] = h * lax.rsqrt(jnp.mean(h * h, axis=-1, keepdims=True) + NORM_EPS) * gain_ref[...]


def _ffn_out(act, w_out, h1, mod3, gain):
    blocks_per_batch = SEQ // FO_TM
    return pl.pallas_call(
        _ffn_out_kernel,
        grid=(TOKENS // FO_TM, FFN_HIDDEN // FO_TK),
        in_specs=[pl.BlockSpec((FO_TM, FO_TK), lambda i, k: (i, k)),
                  pl.BlockSpec((FO_TK, D_MODEL), lambda i, k: (k, 0)),
                  pl.BlockSpec((FO_TM, D_MODEL), lambda i, k: (i, 0)),
                  pl.BlockSpec((1, 6, D_MODEL), lambda i, k: (i // blocks_per_batch, 0, 0)),
                  pl.BlockSpec((1, D_MODEL), lambda i, k: (0, 0))],
        out_specs=pl.BlockSpec((FO_TM, D_MODEL), lambda i, k: (i, 0)),
        out_shape=jax.ShapeDtypeStruct((TOKENS, D_MODEL), F32),
        scratch_shapes=[pltpu.VMEM((FO_TM, D_MODEL), F32)],
        compiler_params=_params("arbitrary", "arbitrary"),
        name="ffn_out",
    )(act, w_out, h1, mod3, gain)


def _pad_cols(t, width):
    return jnp.pad(t, ((0, 0), (0, width - t.shape[1])))


def _pad_rows(t, rows):
    return jnp.pad(t, ((0, rows - t.shape[0]), (0, 0)))


def _pack_lora_rows(w_t):
    o3 = HEAD_COLS
    o4 = o3 + DECAY_LORA
    o5 = o4 + ICLR_LORA
    lora = jnp.concatenate([_pad_rows(w_t[o3:o4], LORA_PAD), _pad_rows(w_t[o4:o5], LORA_PAD), w_t[o5:GATE_ROW0]],
                           axis=0)
    return _pad_rows(lora, 2 * SEC - HEAD_COLS)


def _pack_mu(mu):
    o3 = 3 * RWKV_WIDTH
    o4 = o3 + DECAY_LORA
    o5 = o4 + ICLR_LORA
    mu = mu[None, :]
    rw = jnp.concatenate([mu[:, :o3], _pad_cols(mu[:, o3:o4], LORA_PAD), _pad_cols(mu[:, o4:o5], LORA_PAD),
                          mu[:, o5:]], axis=1)
    return _pad_cols(rw, SEC)


def kernel(x, c, positions, w_ada, b_ada, norm_mix, norm_ffn, norm_final, w_in, b_gate, mu_shift, w0, w_decay_up, a0, w_iclr_up, w_gate_up, k_k, k_a, r_k, lnx_w, lnx_b, w_ret_out, w_rwkv_out, w_o, w_ffn_in, w_ffn_out):
    assert x.shape == (BATCH, SEQ, D_MODEL) and w_ada.shape[0] == 1
    x2 = x.reshape(TOKENS, D_MODEL)
    pos_col = positions.reshape(TOKENS, 1)

    mod = _ada(_pad_rows(c, 8), w_ada[0], b_ada)
    mod3 = mod[:BATCH].reshape(BATCH, 6, D_MODEL)

    w_t = jnp.swapaxes(w_in, 1, 2)[0]
    u1 = _norm1(x2, mod3, norm_mix)
    z = _inproj(u1, w_t, _pack_lora_rows(w_t), _pack_mu(mu_shift[0]))
    y_ret = _retention(pos_col, z)

    par = jnp.concatenate([w0, a0, k_k, k_a, r_k.reshape(1, RWKV_WIDTH), lnx_w, lnx_b,
                           jnp.zeros((1, RWKV_WIDTH), F32)], axis=0)
    y_rw = _rwkv(z, par, _pad_rows(w_decay_up[0], LORA_PAD).astype(BF16),
                 _pad_rows(w_iclr_up[0], LORA_PAD).astype(BF16), w_gate_up[0].astype(BF16))

    merged = _merge(y_ret, y_rw, w_ret_out[0], w_rwkv_out[0], z, b_gate)
    h1, u2 = _oproj(merged, w_o[0], x2, mod3, norm_ffn)
    act = _ffn_in(u2, w_ffn_in[0])
    out = _ffn_out(act, w_ffn_out[0].astype(BF16), h1, mod3, norm_final[None, :])
    return out.reshape(BATCH, SEQ, D_MODEL)
```

```python
import jax
import jax.numpy as jnp
from jax import lax
from jax.experimental import pallas as pl
from jax.experimental.pallas import tpu as pltpu

F32 = jnp.float32
BF16 = jnp.bfloat16

D_MODEL = 2048
BATCH = 2
SEQ = 4096
TOKENS = BATCH * SEQ

RET_HEADS = 4
RET_DIM = 256
RET_WIDTH = RET_HEADS * RET_DIM
RET_CHUNK = 64
ROPE_BASE = 10000.0

RWKV_HEAD = 64
RWKV_WIDTH = 1024
RWKV_HEADS = RWKV_WIDTH // RWKV_HEAD
DECAY_LORA = 96
ICLR_LORA = 96
GATE_LORA = 256
RWKV_CHUNK = 64
GROUP_LANES = 256
GROUP_HEADS = GROUP_LANES // RWKV_HEAD
RWKV_GROUPS = RWKV_WIDTH // GROUP_LANES
LORA_PAD = 128

FFN_HIDDEN = ((8 * D_MODEL // 3 + 255) // 256) * 256
RET_COLS = 4 * RET_WIDTH
SHIFT_COLS = 3 * RWKV_WIDTH + DECAY_LORA + ICLR_LORA + GATE_LORA
GATE_COLS = 2 * D_MODEL
SEC = 4096
Z_COLS = 3 * SEC
HEAD_COLS = RET_COLS + 3 * RWKV_WIDTH
NORM_EPS = 1e-6
GN_EPS_RET = 1e-5
GN_EPS_RWKV = 64e-5

VMEM_LIMIT = 56 * 1024 * 1024
SUBLANES = 8


def _params(*sem, vmem=VMEM_LIMIT):
    return pltpu.CompilerParams(dimension_semantics=sem, vmem_limit_bytes=vmem)


def _bdot(a, b):
    return jnp.dot(a.astype(BF16), b.astype(BF16), preferred_element_type=F32)


def _bdot_nt(a, b):
    return lax.dot_general(a.astype(BF16), b.astype(BF16), (((1,), (1,)), ((), ())),
                           preferred_element_type=F32)


def _bdot_tn(a, b):
    return lax.dot_general(a.astype(BF16), b.astype(BF16), (((0,), (0,)), ((), ())),
                           preferred_element_type=F32)


def _silu(x):
    return x * jax.nn.sigmoid(x)


ADA_TN = 1024


def _ada_kernel(c_ref, w_ref, b_ref, o_ref):
    o_ref[...] = _bdot(_silu(c_ref[...]), w_ref[...]) + b_ref[...]


def _ada(c_pad, w_ada, b_ada):
    n = w_ada.shape[1]
    return pl.pallas_call(
        _ada_kernel,
        grid=(n // ADA_TN,),
        in_specs=[pl.BlockSpec((8, D_MODEL), lambda j: (0, 0)),
                  pl.BlockSpec((D_MODEL, ADA_TN), lambda j: (0, j)),
                  pl.BlockSpec((1, ADA_TN), lambda j: (0, j))],
        out_specs=pl.BlockSpec((8, ADA_TN), lambda j: (0, j)),
        out_shape=jax.ShapeDtypeStruct((8, n), F32),
        compiler_params=_params("arbitrary"),
        name="ada",
    )(c_pad, w_ada, b_ada)


IN_TM = 1024
IN_TN = 1024


def _modulated_norm(x, gain, scale, shift):
    y = x * lax.rsqrt(jnp.mean(x * x, axis=-1, keepdims=True) + NORM_EPS)
    return y * gain * (1.0 + scale) + shift


def _norm1_kernel(x_ref, mod_ref, gain_ref, u_ref):
    mod = mod_ref[0]
    u_ref[...] = _modulated_norm(x_ref[...], gain_ref[...], mod[1:2], mod[0:1]).astype(BF16)


def _norm1(x2, mod3, gain):
    blocks_per_batch = SEQ // IN_TM
    return pl.pallas_call(
        _norm1_kernel,
        grid=(TOKENS // IN_TM,),
        in_specs=[pl.BlockSpec((IN_TM, D_MODEL), lambda i: (i, 0)),
                  pl.BlockSpec((1, 6, D_MODEL), lambda i: (i // blocks_per_batch, 0, 0)),
                  pl.BlockSpec((1, D_MODEL), lambda i: (0, 0))],
        out_specs=pl.BlockSpec((IN_TM, D_MODEL), lambda i: (i, 0)),
        out_shape=jax.ShapeDtypeStruct((TOKENS, D_MODEL), BF16),
        compiler_params=_params("arbitrary"),
        name="norm1",
    )(x2, mod3, gain)


IN_LORA_BLOCK = HEAD_COLS // IN_TN
IN_RWKV_BLOCK0 = RET_COLS // IN_TN
GATE_ROW0 = RET_COLS + SHIFT_COLS


def _inproj_kernel(u_ref, w_ref, wl_ref, mu_ref, o_ref, wb_ref, carry_ref):
    j = pl.program_id(0)
    i = pl.program_id(1)

    @pl.when((i == 0) & (j != IN_LORA_BLOCK))
    def _():
        wb_ref[...] = w_ref[...].astype(BF16)

    @pl.when((i == 0) & (j == IN_LORA_BLOCK))
    def _():
        wb_ref[...] = wl_ref[...].astype(BF16)

    z = _bdot_nt(u_ref[...], wb_ref[...])
    shifted = (j >= IN_RWKV_BLOCK0) & (j <= IN_LORA_BLOCK)

    @pl.when(jnp.logical_not(shifted))
    def _():
        o_ref[...] = z

    @pl.when(shifted)
    def _():
        @pl.when(i % (SEQ // IN_TM) == 0)
        def _():
            carry_ref[...] = jnp.zeros_like(carry_ref)

        row = lax.broadcasted_iota(jnp.int32, z.shape, 0)
        prev = jnp.where(row == 0, carry_ref[...], pltpu.roll(z, 1, 0))
        carry_ref[...] = z[IN_TM - 1:IN_TM, :]
        o_ref[...] = z + (prev - z) * mu_ref[...]


def _inproj(u, w_t, w_lora, mu_p):
    assert HEAD_COLS % IN_TN == 0 and 2 * SEC - HEAD_COLS == IN_TN

    def w_row(j, i):
        tn, g0 = IN_TN // SUBLANES, GATE_ROW0 // SUBLANES
        head = jnp.minimum(j, IN_LORA_BLOCK - 1) * tn
        return (jnp.where(j > IN_LORA_BLOCK, g0 + (j - IN_LORA_BLOCK - 1) * tn, head) * SUBLANES, 0)

    return pl.pallas_call(
        _inproj_kernel,
        grid=(Z_COLS // IN_TN, TOKENS // IN_TM),
        in_specs=[pl.BlockSpec((IN_TM, D_MODEL), lambda j, i: (i, 0)),
                  pl.BlockSpec((pl.Element(IN_TN), pl.Element(D_MODEL)), w_row),
                  pl.BlockSpec((IN_TN, D_MODEL), lambda j, i: (0, 0), pipeline_mode=pl.Buffered(1)),
                  pl.BlockSpec((1, IN_TN), lambda j, i: (0, jnp.clip(j - IN_RWKV_BLOCK0, 0, SEC // IN_TN - 1)))],
        out_specs=pl.BlockSpec((IN_TM, IN_TN), lambda j, i: (i, j)),
        out_shape=jax.ShapeDtypeStruct((TOKENS, Z_COLS), F32),
        scratch_shapes=[pltpu.VMEM((IN_TN, D_MODEL), BF16), pltpu.VMEM((1, IN_TN), F32)],
        compiler_params=_params("arbitrary", "arbitrary"),
        name="inproj",
    )(u, w_t, w_lora, mu_p)


def _ret_stages(pos_ref, z_ref, invf_ref, dintra_ref, zeta_ref, xi_ref, cdec_ref, o_ref, state_ref):
    ang = pos_ref[...].astype(F32) * invf_ref[...]
    cos = jnp.cos(ang)
    sin = jnp.sin(ang)
    half = RET_DIM // 2
    yield

    def rot(t):
        t1, t2 = t[:, :half], t[:, half:]
        return jnp.concatenate([t1 * cos - t2 * sin, t1 * sin + t2 * cos], axis=-1)

    heads = range(RET_HEADS)
    cols = lambda section, h: slice(section * RET_WIDTH + h * RET_DIM, section * RET_WIDTH + (h + 1) * RET_DIM)
    qb = [rot(z_ref[:, cols(0, h)]).astype(BF16) for h in heads]
    yield
    k = [rot(z_ref[:, cols(1, h)]) * (RET_DIM ** -0.5) for h in heads]
    v = [z_ref[:, cols(2, h)].astype(BF16) for h in heads]
    yield
    scores = [_bdot_nt(qb[h], k[h]) * dintra_ref[h] for h in heads]
    state = [state_ref[h] for h in heads]
    yield
    out = [_bdot(scores[h], v[h]) + _bdot(qb[h], state[h]) * xi_ref[h] for h in heads]
    for h in heads:
        state_ref[h] = state[h] * cdec_ref[h] + _bdot_tn(k[h] * zeta_ref[h], v[h])
    yield
    for h in heads:
        mu = jnp.mean(out[h], axis=-1, keepdims=True)
        d = out[h] - mu
        var = jnp.mean(d * d, axis=-1, keepdims=True)
        g = z_ref[:, cols(3, h)]
        o_ref[:, cols(0, h)] = (_silu(g) * (d * lax.rsqrt(var + GN_EPS_RET))).astype(BF16)
        if h % 2:
            yield


def _retention_tables():
    h = RET_HEADS
    half = RET_DIM // 2
    inv_freq = ROPE_BASE ** (-jnp.arange(half, dtype=F32) / half)
    log_gamma = jnp.log(1.0 - 2.0 ** (-5.0 - jnp.arange(h, dtype=F32)))
    idx = jnp.arange(RET_CHUNK, dtype=F32)
    dist = idx[:, None] - idx[None, :]
    decay_intra = jnp.where(dist >= 0, jnp.exp(log_gamma[:, None, None] * jnp.maximum(dist, 0.0)), 0.0)
    zeta = jnp.exp(log_gamma[:, None] * (RET_CHUNK - 1.0 - idx))
    xi = jnp.exp(log_gamma[:, None] * (idx + 1.0))
    chunk_decay = jnp.exp(log_gamma * RET_CHUNK)
    wide = lambda t: jnp.broadcast_to(t[:, :, None], (h, RET_CHUNK, RET_DIM))
    cdec = jnp.broadcast_to(chunk_decay[:, None, None], (h, 1, RET_DIM))
    return inv_freq[None, :], decay_intra, wide(zeta), wide(xi), cdec


def _split2(x):
    hi = x.astype(BF16)
    return hi, (x - hi.astype(F32)).astype(BF16)


def _head_sums(xs, ones_ref):
    c = RWKV_CHUNK
    pieces = []
    for x in xs:
        for p in _split2(x):
            for g in range(RWKV_GROUPS):
                pieces.append(p[:, g * GROUP_LANES:(g + 1) * GROUP_LANES])
    res = jnp.dot(jnp.concatenate(pieces, axis=0), ones_ref[...], preferred_element_type=F32)
    outs = []
    for i in range(len(xs)):
        base = i * 2 * RWKV_GROUPS * c
        cols = []
        for g in range(RWKV_GROUPS):
            hi = res[base + g * c:base + (g + 1) * c]
            lo = res[base + (RWKV_GROUPS + g) * c:base + (RWKV_GROUPS + g + 1) * c]
            cols.append(hi + lo)
        outs.append(jnp.concatenate(cols, axis=-1))
    return outs


def _rwkv_stages(zs_ref, par_ref, wd_ref, wa_ref, wg_ref, ltri_ref, ones_ref, mstrict_ref, mincl_ref,
                 o_ref, s_ref):
    c = RWKV_CHUNK
    w = RWKV_WIDTH

    r, kw, vw = zs_ref[:, :w], zs_ref[:, w:2 * w], zs_ref[:, 2 * w:3 * w]
    zw = zs_ref[:, 3 * w:3 * w + LORA_PAD]
    za = zs_ref[:, 3 * w + LORA_PAD:3 * w + 2 * LORA_PAD]
    zg = zs_ref[:, 3 * w + 2 * LORA_PAD:3 * w + 2 * LORA_PAD + GATE_LORA]
    par = par_ref[...]
    w0, a0, k_k, k_a, r_k, lnx_w, lnx_b = (par[i:i + 1] for i in range(7))

    dec_pre = w0 + _bdot(jnp.tanh(zw), wd_ref[...])
    softplus = jnp.maximum(-dec_pre, 0.0) + jnp.log1p(jnp.exp(-jnp.abs(dec_pre)))
    logdec = -jnp.exp(-softplus - 0.5)
    iclr = jax.nn.sigmoid(a0 + _bdot(za, wa_ref[...]))

    kk = kw * k_k
    k_mod = kw * (1.0 + (iclr - 1.0) * k_a)
    (kk_sq,) = _head_sums([kk * kk], ones_ref)
    kk = kk * (1.0 / jnp.maximum(jnp.sqrt(kk_sq), 1e-12))
    a = -kk
    b = kk * iclr

    ld_hi = logdec.astype(BF16)
    ld_mid, ld_lo = _split2(logdec - ld_hi.astype(F32))
    ltri = ltri_ref[...]
    cum = (jnp.dot(ltri, ld_hi, preferred_element_type=F32) + jnp.dot(ltri, ld_mid, preferred_element_type=F32)
           + jnp.dot(ltri, ld_lo, preferred_element_type=F32))
    cum_end = cum[c - 1:c, :]
    p_inv = jnp.exp(-cum)
    a_t = a * jnp.exp(cum - logdec)
    r_t = r * jnp.exp(cum)
    b_t = (b * p_inv).astype(BF16)
    k_t = (k_mod * p_inv).astype(BF16)

    lane_head = lax.shift_right_logical(lax.broadcasted_iota(jnp.int32, (c, GROUP_LANES), 1), 6)
    head_masks = [lane_head == h for h in range(GROUP_HEADS)]

    def stack(x):
        return jnp.concatenate([jnp.where(m, x, 0.0).astype(BF16) for m in head_masks], axis=0)

    def rep(x):
        return jnp.concatenate([x] * GROUP_HEADS, axis=0)

    m_strict = mstrict_ref[...]
    m_incl = mincl_ref[...]
    eye = (lax.broadcasted_iota(jnp.int32, (GROUP_LANES, GROUP_LANES), 0)
           == lax.broadcasted_iota(jnp.int32, (GROUP_LANES, GROUP_LANES), 1)).astype(F32)
    n4 = GROUP_HEADS * c
    groups = range(RWKV_GROUPS)
    sls = [slice(g * GROUP_LANES, (g + 1) * GROUP_LANES) for g in groups]
    a_s = [stack(a_t[:, sl]) for sl in sls]
    r_s = [stack(r_t[:, sl]) for sl in sls]
    blk = [_bdot_nt(jnp.concatenate([a_s[g], r_s[g]], axis=0),
                    jnp.concatenate([rep(b_t[:, sls[g]]), rep(k_t[:, sls[g]])], axis=0)) for g in groups]
    a_ab = [blk[g][:n4, :n4] * m_strict for g in groups]
    a_ak = [blk[g][:n4, n4:] * m_strict for g in groups]
    a_rb = [blk[g][n4:, :n4] * m_incl for g in groups]
    a_rk = [blk[g][n4:, n4:] * m_incl for g in groups]

    t_inv = [eye + a_ab[g] for g in groups]
    a_pow = a_ab

    def inverse_round():
        nonlocal a_pow, t_inv
        a_pow = [_bdot(a_pow[g], a_pow[g]) for g in groups]
        t_inv = [t_inv[g] + _bdot(t_inv[g], a_pow[g]) for g in groups]

    yield
    inverse_round()
    g_rw = _bdot(jax.nn.sigmoid(zg), wg_ref[...])
    (rk_sum,) = _head_sums([r * k_mod * r_k], ones_ref)
    bonus = rk_sum * vw
    yield
    inverse_round()
    p_end = jnp.exp(cum_end - cum)
    b_e = b * p_end
    k_e = k_mod * p_end
    decay_end = jnp.exp(cum_end)
    yield
    inverse_round()
    v_s = [stack(vw[:, sl]) for sl in sls]
    be_s = [stack(b_e[:, sl]) for sl in sls]
    yield
    inverse_round()
    ke_s = [stack(k_e[:, sl]) for sl in sls]
    yield
    inverse_round()
    yield

    a_eff = [_bdot(t_inv[g], a_s[g]) for g in groups]
    av = [_bdot(a_ak[g], v_s[g]) for g in groups]
    u_const = [_bdot(t_inv[g], av[g]) for g in groups]
    yield
    state = [s_ref[g] for g in groups]
    sb = [state[g].astype(BF16) for g in groups]
    u = [_bdot_nt(a_eff[g], sb[g]) + u_const[g] for g in groups]
    y_st = [_bdot_nt(r_s[g], sb[g]) + _bdot(a_rb[g], u[g]) + _bdot(a_rk[g], v_s[g]) for g in groups]
    for g in groups:
        s_ref[g] = (state[g] * decay_end[:, sls[g]]
                    + _bdot_tn(jnp.concatenate([u[g].astype(BF16), v_s[g]], axis=0),
                               jnp.concatenate([be_s[g], ke_s[g]], axis=0)))
    y = jnp.concatenate([t[:c] + t[c:2 * c] + t[2 * c:3 * c] + t[3 * c:] for t in y_st], axis=-1)

    (y_sum,) = _head_sums([y], ones_ref)
    d = y - y_sum * (1.0 / RWKV_HEAD)
    (d_sq,) = _head_sums([d * d], ones_ref)
    y = d * lax.rsqrt(d_sq * (1.0 / RWKV_HEAD) + GN_EPS_RWKV) * lnx_w + lnx_b + bonus
    o_ref[...] = (y * g_rw).astype(BF16)


def _emit_interleaved(primary, secondary, pieces_after):
    for n in pieces_after:
        next(primary, None)
        for _ in range(n):
            next(secondary, None)
    for _ in primary:
        pass
    for _ in secondary:
        pass


def _mixer_kernel(pos_ref, zr_ref, zs_ref, invf_ref, dintra_ref, zeta_ref, xi_ref, cdec_ref, par_ref, wd_ref, wa_ref,
                  wg_ref, ltri_ref, ones_ref, mstrict_ref, mincl_ref, or_ref, ow_ref, rs_ref, ws_ref):
    @pl.when(pl.program_id(1) == 0)
    def _():
        rs_ref[...] = jnp.zeros_like(rs_ref)
        ws_ref[...] = jnp.zeros_like(ws_ref)

    _emit_interleaved(
        _rwkv_stages(zs_ref, par_ref, wd_ref, wa_ref, wg_ref, ltri_ref, ones_ref, mstrict_ref, mincl_ref, ow_ref, ws_ref),
        _ret_stages(pos_ref, zr_ref, invf_ref, dintra_ref, zeta_ref, xi_ref, cdec_ref, or_ref, rs_ref),
        pieces_after=(1, 2, 1, 1, 2, 1))


def _rwkv_tables():
    c = RWKV_CHUNK
    ltri = (jnp.arange(c)[:, None] >= jnp.arange(c)[None, :]).astype(BF16)
    row = jnp.arange(GROUP_LANES)[:, None]
    col = jnp.arange(GROUP_LANES)[None, :]
    same_head = (row // RWKV_HEAD) == (col // RWKV_HEAD)
    ones = same_head.astype(BF16)
    m_strict = (same_head & (row > col)).astype(F32)
    m_incl = (same_head & (row >= col)).astype(F32)
    return ltri, ones, m_strict, m_incl


def _mixers(pos_col, z, par, wd, wa, wg):
    assert RWKV_CHUNK == RWKV_HEAD and RET_CHUNK == RWKV_CHUNK
    c = RWKV_CHUNK
    ret_tables = _retention_tables()
    rwkv_tables = _rwkv_tables()
    nch = SEQ // c
    full = lambda a: pl.BlockSpec(a.shape, lambda b, n: (0,) * a.ndim)
    rows = lambda width, col: pl.BlockSpec((c, width), lambda b, n: (b * nch + n, col))
    return pl.pallas_call(
        _mixer_kernel,
        grid=(BATCH, nch),
        in_specs=[rows(1, 0), rows(SEC, 0), rows(SEC, 1), *map(full, ret_tables),
                  full(par), full(wd), full(wa), full(wg), *map(full, rwkv_tables)],
        out_specs=[rows(RET_WIDTH, 0), rows(RWKV_WIDTH, 0)],
        out_shape=[jax.ShapeDtypeStruct((TOKENS, RET_WIDTH), BF16), jax.ShapeDtypeStruct((TOKENS, RWKV_WIDTH), BF16)],
        scratch_shapes=[pltpu.VMEM((RET_HEADS, RET_DIM, RET_DIM), F32),
                        pltpu.VMEM((RWKV_GROUPS, GROUP_LANES, GROUP_LANES), F32)],
        compiler_params=_params("arbitrary", "arbitrary"),
        name="mixers",
    )(pos_col, z, z, *ret_tables, par, wd, wa, wg, *rwkv_tables)


MG_TM = 1024
MG_TN = 1024


def _merge_kernel(yr_ref, yw_ref, wr_ref, ww_ref, gr_ref, gw_ref, br_ref, bw_ref, o_ref, wrb_ref, wwb_ref):
    @pl.when(pl.program_id(1) == 0)
    def _():
        wrb_ref[...] = wr_ref[...].astype(BF16)
        wwb_ref[...] = ww_ref[...].astype(BF16)

    pr = jnp.dot(yr_ref[...], wrb_ref[...], preferred_element_type=F32)
    pw = jnp.dot(yw_ref[...], wwb_ref[...], preferred_element_type=F32)
    o = jax.nn.sigmoid(gr_ref[...] + br_ref[...]) * pr + jax.nn.sigmoid(gw_ref[...] + bw_ref[...]) * pw
    o_ref[...] = o.astype(BF16)


def _merge(y_ret, y_rw, w_ret, w_rw, z, b_gate):
    g0 = 2 * SEC // MG_TN
    nd = D_MODEL // MG_TN
    return pl.pallas_call(
        _merge_kernel,
        grid=(nd, TOKENS // MG_TM),
        in_specs=[pl.BlockSpec((MG_TM, RET_WIDTH), lambda j, i: (i, 0)),
                  pl.BlockSpec((MG_TM, RWKV_WIDTH), lambda j, i: (i, 0)),
                  pl.BlockSpec((RET_WIDTH, MG_TN), lambda j, i: (0, j)),
                  pl.BlockSpec((RWKV_WIDTH, MG_TN), lambda j, i: (0, j)),
                  pl.BlockSpec((MG_TM, MG_TN), lambda j, i: (i, g0 + j)),
                  pl.BlockSpec((MG_TM, MG_TN), lambda j, i: (i, g0 + nd + j)),
                  pl.BlockSpec((1, MG_TN), lambda j, i: (0, j)),
                  pl.BlockSpec((1, MG_TN), lambda j, i: (0, nd + j))],
        out_specs=pl.BlockSpec((MG_TM, MG_TN), lambda j, i: (i, j)),
        out_shape=jax.ShapeDtypeStruct((TOKENS, D_MODEL), BF16),
        scratch_shapes=[pltpu.VMEM((RET_WIDTH, MG_TN), BF16), pltpu.VMEM((RWKV_WIDTH, MG_TN), BF16)],
        compiler_params=_params("arbitrary", "arbitrary"),
        name="merge",
    )(y_ret, y_rw, w_ret, w_rw, z, z, b_gate, b_gate)


OP_TM = 512


def _oproj_kernel(m_ref, w_ref, x_ref, mod_ref, gain_ref, h_ref, u_ref, wb_ref):
    @pl.when(pl.program_id(0) == 0)
    def _():
        wb_ref[...] = w_ref[...].astype(BF16)

    mod = mod_ref[0]
    h = x_ref[...] + mod[2:3] * jnp.dot(m_ref[...], wb_ref[...], preferred_element_type=F32)
    h_ref[...] = h
    u_ref[...] = _modulated_norm(h, gain_ref[...], mod[4:5], mod[3:4]).astype(BF16)


def _oproj(merged, w_o, x2, mod3, gain):
    blocks_per_batch = SEQ // OP_TM
    return pl.pallas_call(
        _oproj_kernel,
        grid=(TOKENS // OP_TM,),
        in_specs=[pl.BlockSpec((OP_TM, D_MODEL), lambda i: (i, 0)),
                  pl.BlockSpec((D_MODEL, D_MODEL), lambda i: (0, 0), pipeline_mode=pl.Buffered(1)),
                  pl.BlockSpec((OP_TM, D_MODEL), lambda i: (i, 0)),
                  pl.BlockSpec((1, 6, D_MODEL), lambda i: (i // blocks_per_batch, 0, 0)),
                  pl.BlockSpec((1, D_MODEL), lambda i: (0, 0))],
        out_specs=[pl.BlockSpec((OP_TM, D_MODEL), lambda i: (i, 0)),
                   pl.BlockSpec((OP_TM, D_MODEL), lambda i: (i, 0))],
        out_shape=[jax.ShapeDtypeStruct((TOKENS, D_MODEL), F32),
                   jax.ShapeDtypeStruct((TOKENS, D_MODEL), BF16)],
        scratch_shapes=[pltpu.VMEM((D_MODEL, D_MODEL), BF16)],
        compiler_params=_params("arbitrary"),
        name="oproj",
    )(merged, w_o, x2, mod3, gain)


FI_TM = 1024
FI_TN = 512


def _ffn_in_kernel(u_ref, wg_ref, wu_ref, o_ref, wgb_ref, wub_ref):
    @pl.when(pl.program_id(1) == 0)
    def _():
        wgb_ref[...] = wg_ref[...].astype(BF16)
        wub_ref[...] = wu_ref[...].astype(BF16)

    u = u_ref[...]
    gate = jnp.dot(u, wgb_ref[...], preferred_element_type=F32)
    up = jnp.dot(u, wub_ref[...], preferred_element_type=F32)
    o_ref[...] = (_silu(gate) * up).astype(BF16)


def _ffn_in(u2, w_in):
    nt = FFN_HIDDEN // FI_TN
    return pl.pallas_call(
        _ffn_in_kernel,
        grid=(nt, TOKENS // FI_TM),
        in_specs=[pl.BlockSpec((FI_TM, D_MODEL), lambda j, i: (i, 0)),
                  pl.BlockSpec((D_MODEL, FI_TN), lambda j, i: (0, j)),
                  pl.BlockSpec((D_MODEL, FI_TN), lambda j, i: (0, nt + j))],
        out_specs=pl.BlockSpec((FI_TM, FI_TN), lambda j, i: (i, j)),
        out_shape=jax.ShapeDtypeStruct((TOKENS, FFN_HIDDEN), BF16),
        scratch_shapes=[pltpu.VMEM((D_MODEL, FI_TN), BF16), pltpu.VMEM((D_MODEL, FI_TN), BF16)],
        compiler_params=_params("arbitrary", "arbitrary"),
        name="ffn_in",
    )(u2, w_in, w_in)


FO_TM = 1024
FO_TK = 512


def _ffn_out_kernel(a_ref, w_ref, h_ref, mod_ref, gain_ref, o_ref, acc_ref):
    k = pl.program_id(1)

    @pl.when(k == 0)
    def _():
        acc_ref[...] = jnp.zeros_like(acc_ref)

    acc_ref[...] += jnp.dot(a_ref[...], w_ref[...], preferred_element_type=F32)

    @pl.when(k == pl.num_programs(1) - 1)
    def _():
        h = h_ref[...] + mod_ref[0][5:6] * acc_ref[...]
        o_ref[...] = h * lax.rsqrt(jnp.mean(h * h, axis=-1, keepdims=True) + NORM_EPS) * gain_ref[...]


def _ffn_out(act, w_out, h1, mod3, gain):
    blocks_per_batch = SEQ // FO_TM
    return pl.pallas_call(
        _ffn_out_kernel,
        grid=(TOKENS // FO_TM, FFN_HIDDEN // FO_TK),
        in_specs=[pl.BlockSpec((FO_TM, FO_TK), lambda i, k: (i, k)),
                  pl.BlockSpec((FO_TK, D_MODEL), lambda i, k: (k, 0)),
                  pl.BlockSpec((FO_TM, D_MODEL), lambda i, k: (i, 0)),
                  pl.BlockSpec((1, 6, D_MODEL), lambda i, k: (i // blocks_per_batch, 0, 0)),
                  pl.BlockSpec((1, D_MODEL), lambda i, k: (0, 0))],
        out_specs=pl.BlockSpec((FO_TM, D_MODEL), lambda i, k: (i, 0)),
        out_shape=jax.ShapeDtypeStruct((TOKENS, D_MODEL), F32),
        scratch_shapes=[pltpu.VMEM((FO_TM, D_MODEL), F32)],
        compiler_params=_params("arbitrary", "arbitrary"),
        name="ffn_out",
    )(act, w_out, h1, mod3, gain)


def _pad_cols(t, width):
    return jnp.pad(t, ((0, 0), (0, width - t.shape[1])))


def _pad_rows(t, rows):
    return jnp.pad(t, ((0, rows - t.shape[0]), (0, 0)))


def _pack_lora_rows(w_t):
    o3 = HEAD_COLS
    o4 = o3 + DECAY_LORA
    o5 = o4 + ICLR_LORA
    lora = jnp.concatenate([_pad_rows(w_t[o3:o4], LORA_PAD), _pad_rows(w_t[o4:o5], LORA_PAD), w_t[o5:GATE_ROW0]],
                           axis=0)
    return _pad_rows(lora, 2 * SEC - HEAD_COLS)


def _pack_mu(mu):
    o3 = 3 * RWKV_WIDTH
    o4 = o3 + DECAY_LORA
    o5 = o4 + ICLR_LORA
    mu = mu[None, :]
    rw = jnp.concatenate([mu[:, :o3], _pad_cols(mu[:, o3:o4], LORA_PAD), _pad_cols(mu[:, o4:o5], LORA_PAD),
                          mu[:, o5:]], axis=1)
    return _pad_cols(rw, SEC)


def kernel(x, c, positions, w_ada, b_ada, norm_mix, norm_ffn, norm_final, w_in, b_gate, mu_shift, w0, w_decay_up, a0, w_iclr_up, w_gate_up, k_k, k_a, r_k, lnx_w, lnx_b, w_ret_out, w_rwkv_out, w_o, w_ffn_in, w_ffn_out):
    assert x.shape == (BATCH, SEQ, D_MODEL) and w_ada.shape[0] == 1
    x2 = x.reshape(TOKENS, D_MODEL)
    pos_col = positions.reshape(TOKENS, 1)

    mod = _ada(_pad_rows(c, 8), w_ada[0], b_ada)
    mod3 = mod[:BATCH].reshape(BATCH, 6, D_MODEL)

    w_t = jnp.swapaxes(w_in, 1, 2)[0]
    u1 = _norm1(x2, mod3, norm_mix)
    z = _inproj(u1, w_t, _pack_lora_rows(w_t), _pack_mu(mu_shift[0]))

    par = jnp.concatenate([w0, a0, k_k, k_a, r_k.reshape(1, RWKV_WIDTH), lnx_w, lnx_b,
                           jnp.zeros((1, RWKV_WIDTH), F32)], axis=0)
    y_ret, y_rw = _mixers(pos_col, z, par, _pad_rows(w_decay_up[0], LORA_PAD).astype(BF16),
                          _pad_rows(w_iclr_up[0], LORA_PAD).astype(BF16), w_gate_up[0].astype(BF16))

    merged = _merge(y_ret, y_rw, w_ret_out[0], w_rwkv_out[0], z, b_gate)
    h1, u2 = _oproj(merged, w_o[0], x2, mod3, norm_ffn)
    act = _ffn_in(u2, w_ffn_in[0])
    out = _ffn_out(act, w_ffn_out[0].astype(BF16), h1, mod3, norm_final[None, :])
    return out.reshape(BATCH, SEQ, D_MODEL)
```

```python
import jax
import jax.numpy as jnp
from jax import lax
from jax.experimental import pallas as pl
from jax.experimental.pallas import tpu as pltpu

F32 = jnp.float32
BF16 = jnp.bfloat16

D_MODEL = 2048
BATCH = 2
SEQ = 4096
TOKENS = BATCH * SEQ

RET_HEADS = 4
RET_DIM = 256
RET_WIDTH = RET_HEADS * RET_DIM
RET_CHUNK = 64
ROPE_BASE = 10000.0

RWKV_HEAD = 64
RWKV_WIDTH = 1024
RWKV_HEADS = RWKV_WIDTH // RWKV_HEAD
DECAY_LORA = 96
ICLR_LORA = 96
GATE_LORA = 256
RWKV_CHUNK = 64
GROUP_LANES = 256
GROUP_HEADS = GROUP_LANES // RWKV_HEAD
RWKV_GROUPS = RWKV_WIDTH // GROUP_LANES
LORA_PAD = 128

FFN_HIDDEN = ((8 * D_MODEL // 3 + 255) // 256) * 256
RET_COLS = 4 * RET_WIDTH
SHIFT_COLS = 3 * RWKV_WIDTH + DECAY_LORA + ICLR_LORA + GATE_LORA
GATE_COLS = 2 * D_MODEL
SEC = 4096
Z_COLS = 3 * SEC
HEAD_COLS = RET_COLS + 3 * RWKV_WIDTH
NORM_EPS = 1e-6
GN_EPS_RET = 1e-5
GN_EPS_RWKV = 64e-5

VMEM_LIMIT = 56 * 1024 * 1024
VMEM_LIMIT_MAX = 60 * 1024 * 1024
SUBLANES = 8


def _params(*sem, vmem=VMEM_LIMIT):
    return pltpu.CompilerParams(dimension_semantics=sem, vmem_limit_bytes=vmem)


def _bdot(a, b):
    return jnp.dot(a.astype(BF16), b.astype(BF16), preferred_element_type=F32)


def _bdot_nt(a, b):
    return lax.dot_general(a.astype(BF16), b.astype(BF16), (((1,), (1,)), ((), ())),
                           preferred_element_type=F32)


def _bdot_tn(a, b):
    return lax.dot_general(a.astype(BF16), b.astype(BF16), (((0,), (0,)), ((), ())),
                           preferred_element_type=F32)


def _silu(x):
    return x * jax.nn.sigmoid(x)


ADA_TN = 1024


def _ada_kernel(c_ref, w_ref, b_ref, o_ref):
    o_ref[...] = _bdot(_silu(c_ref[...]), w_ref[...]) + b_ref[...]


def _ada(c_pad, w_ada, b_ada):
    n = w_ada.shape[1]
    return pl.pallas_call(
        _ada_kernel,
        grid=(n // ADA_TN,),
        in_specs=[pl.BlockSpec((8, D_MODEL), lambda j: (0, 0)),
                  pl.BlockSpec((D_MODEL, ADA_TN), lambda j: (0, j)),
                  pl.BlockSpec((1, ADA_TN), lambda j: (0, j))],
        out_specs=pl.BlockSpec((8, ADA_TN), lambda j: (0, j)),
        out_shape=jax.ShapeDtypeStruct((8, n), F32),
        compiler_params=_params("arbitrary"),
        name="ada",
    )(c_pad, w_ada, b_ada)


IN_TM = 1024
IN_TN = 1024


def _modulated_norm(x, gain, scale, shift):
    y = x * lax.rsqrt(jnp.mean(x * x, axis=-1, keepdims=True) + NORM_EPS)
    return y * gain * (1.0 + scale) + shift


def _norm1_kernel(x_ref, mod_ref, gain_ref, u_ref):
    mod = mod_ref[0]
    u_ref[...] = _modulated_norm(x_ref[...], gain_ref[...], mod[1:2], mod[0:1]).astype(BF16)


def _norm1(x2, mod3, gain):
    blocks_per_batch = SEQ // IN_TM
    return pl.pallas_call(
        _norm1_kernel,
        grid=(TOKENS // IN_TM,),
        in_specs=[pl.BlockSpec((IN_TM, D_MODEL), lambda i: (i, 0)),
                  pl.BlockSpec((1, 6, D_MODEL), lambda i: (i // blocks_per_batch, 0, 0)),
                  pl.BlockSpec((1, D_MODEL), lambda i: (0, 0))],
        out_specs=pl.BlockSpec((IN_TM, D_MODEL), lambda i: (i, 0)),
        out_shape=jax.ShapeDtypeStruct((TOKENS, D_MODEL), BF16),
        compiler_params=_params("arbitrary"),
        name="norm1",
    )(x2, mod3, gain)


IN_LORA_BLOCK = HEAD_COLS // IN_TN
IN_RWKV_BLOCK0 = RET_COLS // IN_TN
GATE_ROW0 = RET_COLS + SHIFT_COLS


def _inproj_kernel(u_ref, w_ref, wl_ref, mu_ref, o_ref, wb_ref, carry_ref):
    j = pl.program_id(0)
    i = pl.program_id(1)

    @pl.when((i == 0) & (j != IN_LORA_BLOCK))
    def _():
        wb_ref[...] = w_ref[...].astype(BF16)

    @pl.when((i == 0) & (j == IN_LORA_BLOCK))
    def _():
        wb_ref[...] = wl_ref[...].astype(BF16)

    z = _bdot_nt(u_ref[...], wb_ref[...])
    shifted = (j >= IN_RWKV_BLOCK0) & (j <= IN_LORA_BLOCK)

    @pl.when(jnp.logical_not(shifted))
    def _():
        o_ref[...] = z

    @pl.when(shifted)
    def _():
        @pl.when(i % (SEQ // IN_TM) == 0)
        def _():
            carry_ref[...] = jnp.zeros_like(carry_ref)

        row = lax.broadcasted_iota(jnp.int32, z.shape, 0)
        prev = jnp.where(row == 0, carry_ref[...], pltpu.roll(z, 1, 0))
        carry_ref[...] = z[IN_TM - 1:IN_TM, :]
        o_ref[...] = z + (prev - z) * mu_ref[...]


def _inproj(u, w_t, w_lora, mu_p):
    assert HEAD_COLS % IN_TN == 0 and 2 * SEC - HEAD_COLS == IN_TN

    def w_row(j, i):
        tn, g0 = IN_TN // SUBLANES, GATE_ROW0 // SUBLANES
        head = jnp.minimum(j, IN_LORA_BLOCK - 1) * tn
        return (jnp.where(j > IN_LORA_BLOCK, g0 + (j - IN_LORA_BLOCK - 1) * tn, head) * SUBLANES, 0)

    return pl.pallas_call(
        _inproj_kernel,
        grid=(Z_COLS // IN_TN, TOKENS // IN_TM),
        in_specs=[pl.BlockSpec((IN_TM, D_MODEL), lambda j, i: (i, 0)),
                  pl.BlockSpec((pl.Element(IN_TN), pl.Element(D_MODEL)), w_row),
                  pl.BlockSpec((IN_TN, D_MODEL), lambda j, i: (0, 0), pipeline_mode=pl.Buffered(1)),
                  pl.BlockSpec((1, IN_TN), lambda j, i: (0, jnp.clip(j - IN_RWKV_BLOCK0, 0, SEC // IN_TN - 1)))],
        out_specs=pl.BlockSpec((IN_TM, IN_TN), lambda j, i: (i, j)),
        out_shape=jax.ShapeDtypeStruct((TOKENS, Z_COLS), F32),
        scratch_shapes=[pltpu.VMEM((IN_TN, D_MODEL), BF16), pltpu.VMEM((1, IN_TN), F32)],
        compiler_params=_params("arbitrary", "arbitrary"),
        name="inproj",
    )(u, w_t, w_lora, mu_p)


def _ret_stages(pos_ref, z_ref, invf_ref, dintra_ref, zeta_ref, xi_ref, cdec_ref, o_ref, state_ref):
    ang = pos_ref[...].astype(F32) * invf_ref[...]
    cos = jnp.cos(ang)
    sin = jnp.sin(ang)
    half = RET_DIM // 2
    yield

    def rot(t):
        t1, t2 = t[:, :half], t[:, half:]
        return jnp.concatenate([t1 * cos - t2 * sin, t1 * sin + t2 * cos], axis=-1)

    heads = range(RET_HEADS)
    cols = lambda section, h: slice(section * RET_WIDTH + h * RET_DIM, section * RET_WIDTH + (h + 1) * RET_DIM)
    qb = [rot(z_ref[:, cols(0, h)]).astype(BF16) for h in heads]
    yield
    k = [rot(z_ref[:, cols(1, h)]) * (RET_DIM ** -0.5) for h in heads]
    v = [z_ref[:, cols(2, h)].astype(BF16) for h in heads]
    yield
    scores = [_bdot_nt(qb[h], k[h]) * dintra_ref[h] for h in heads]
    state = [state_ref[h] for h in heads]
    yield
    out = [_bdot(scores[h], v[h]) + _bdot(qb[h], state[h]) * xi_ref[h] for h in heads]
    for h in heads:
        state_ref[h] = state[h] * cdec_ref[h] + _bdot_tn(k[h] * zeta_ref[h], v[h])
    yield
    for h in heads:
        mu = jnp.mean(out[h], axis=-1, keepdims=True)
        d = out[h] - mu
        var = jnp.mean(d * d, axis=-1, keepdims=True)
        g = z_ref[:, cols(3, h)]
        o_ref[:, cols(0, h)] = (_silu(g) * (d * lax.rsqrt(var + GN_EPS_RET))).astype(BF16)
        if h % 2:
            yield


def _retention_tables():
    h = RET_HEADS
    half = RET_DIM // 2
    inv_freq = ROPE_BASE ** (-jnp.arange(half, dtype=F32) / half)
    log_gamma = jnp.log(1.0 - 2.0 ** (-5.0 - jnp.arange(h, dtype=F32)))
    idx = jnp.arange(RET_CHUNK, dtype=F32)
    dist = idx[:, None] - idx[None, :]
    decay_intra = jnp.where(dist >= 0, jnp.exp(log_gamma[:, None, None] * jnp.maximum(dist, 0.0)), 0.0)
    zeta = jnp.exp(log_gamma[:, None] * (RET_CHUNK - 1.0 - idx))
    xi = jnp.exp(log_gamma[:, None] * (idx + 1.0))
    chunk_decay = jnp.exp(log_gamma * RET_CHUNK)
    wide = lambda t: jnp.broadcast_to(t[:, :, None], (h, RET_CHUNK, RET_DIM))
    cdec = jnp.broadcast_to(chunk_decay[:, None, None], (h, 1, RET_DIM))
    return inv_freq[None, :], decay_intra, wide(zeta), wide(xi), cdec


def _split2(x):
    hi = x.astype(BF16)
    return hi, (x - hi.astype(F32)).astype(BF16)


def _head_sums(xs, ones_ref):
    c = RWKV_CHUNK
    pieces = []
    for x in xs:
        for p in _split2(x):
            for g in range(RWKV_GROUPS):
                pieces.append(p[:, g * GROUP_LANES:(g + 1) * GROUP_LANES])
    res = jnp.dot(jnp.concatenate(pieces, axis=0), ones_ref[...], preferred_element_type=F32)
    outs = []
    for i in range(len(xs)):
        base = i * 2 * RWKV_GROUPS * c
        cols = []
        for g in range(RWKV_GROUPS):
            hi = res[base + g * c:base + (g + 1) * c]
            lo = res[base + (RWKV_GROUPS + g) * c:base + (RWKV_GROUPS + g + 1) * c]
            cols.append(hi + lo)
        outs.append(jnp.concatenate(cols, axis=-1))
    return outs


def _rwkv_stages(zs_ref, par_ref, wd_ref, wa_ref, wg_ref, ltri_ref, ones_ref, mstrict_ref, mincl_ref,
                 o_ref, s_ref):
    c = RWKV_CHUNK
    w = RWKV_WIDTH

    r, kw, vw = zs_ref[:, :w], zs_ref[:, w:2 * w], zs_ref[:, 2 * w:3 * w]
    zw = zs_ref[:, 3 * w:3 * w + LORA_PAD]
    za = zs_ref[:, 3 * w + LORA_PAD:3 * w + 2 * LORA_PAD]
    zg = zs_ref[:, 3 * w + 2 * LORA_PAD:3 * w + 2 * LORA_PAD + GATE_LORA]
    par = par_ref[...]
    w0, a0, k_k, k_a, r_k, lnx_w, lnx_b = (par[i:i + 1] for i in range(7))

    dec_pre = w0 + _bdot(jnp.tanh(zw), wd_ref[...])
    softplus = jnp.maximum(-dec_pre, 0.0) + jnp.log1p(jnp.exp(-jnp.abs(dec_pre)))
    logdec = -jnp.exp(-softplus - 0.5)
    iclr = jax.nn.sigmoid(a0 + _bdot(za, wa_ref[...]))

    kk = kw * k_k
    k_mod = kw * (1.0 + (iclr - 1.0) * k_a)
    (kk_sq,) = _head_sums([kk * kk], ones_ref)
    kk = kk * (1.0 / jnp.maximum(jnp.sqrt(kk_sq), 1e-12))
    a = -kk
    b = kk * iclr

    ld_hi = logdec.astype(BF16)
    ld_mid, ld_lo = _split2(logdec - ld_hi.astype(F32))
    ltri = ltri_ref[...]
    cum = (jnp.dot(ltri, ld_hi, preferred_element_type=F32) + jnp.dot(ltri, ld_mid, preferred_element_type=F32)
           + jnp.dot(ltri, ld_lo, preferred_element_type=F32))
    cum_end = cum[c - 1:c, :]
    p_inv = jnp.exp(-cum)
    a_t = a * jnp.exp(cum - logdec)
    r_t = r * jnp.exp(cum)
    b_t = (b * p_inv).astype(BF16)
    k_t = (k_mod * p_inv).astype(BF16)

    lane_head = lax.shift_right_logical(lax.broadcasted_iota(jnp.int32, (c, GROUP_LANES), 1), 6)
    head_masks = [lane_head == h for h in range(GROUP_HEADS)]

    def stack(x):
        return jnp.concatenate([jnp.where(m, x, 0.0).astype(BF16) for m in head_masks], axis=0)

    def rep(x):
        return jnp.concatenate([x] * GROUP_HEADS, axis=0)

    m_strict = mstrict_ref[...]
    m_incl = mincl_ref[...]
    eye = (lax.broadcasted_iota(jnp.int32, (GROUP_LANES, GROUP_LANES), 0)
           == lax.broadcasted_iota(jnp.int32, (GROUP_LANES, GROUP_LANES), 1)).astype(F32)
    n4 = GROUP_HEADS * c
    groups = range(RWKV_GROUPS)
    sls = [slice(g * GROUP_LANES, (g + 1) * GROUP_LANES) for g in groups]
    a_s = [stack(a_t[:, sl]) for sl in sls]
    r_s = [stack(r_t[:, sl]) for sl in sls]
    blk = [_bdot_nt(jnp.concatenate([a_s[g], r_s[g]], axis=0),
                    jnp.concatenate([rep(b_t[:, sls[g]]), rep(k_t[:, sls[g]])], axis=0)) for g in groups]
    a_ab = [blk[g][:n4, :n4] * m_strict for g in groups]
    a_ak = [blk[g][:n4, n4:] * m_strict for g in groups]
    a_rb = [blk[g][n4:, :n4] * m_incl for g in groups]
    a_rk = [blk[g][n4:, n4:] * m_incl for g in groups]

    t_inv = [eye + a_ab[g] for g in groups]
    a_pow = a_ab

    def inverse_round():
        nonlocal a_pow, t_inv
        a_pow = [_bdot(a_pow[g], a_pow[g]) for g in groups]
        t_inv = [t_inv[g] + _bdot(t_inv[g], a_pow[g]) for g in groups]

    yield
    inverse_round()
    g_rw = _bdot(jax.nn.sigmoid(zg), wg_ref[...])
    (rk_sum,) = _head_sums([r * k_mod * r_k], ones_ref)
    bonus = rk_sum * vw
    yield
    inverse_round()
    p_end = jnp.exp(cum_end - cum)
    b_e = b * p_end
    k_e = k_mod * p_end
    decay_end = jnp.exp(cum_end)
    yield
    inverse_round()
    v_s = [stack(vw[:, sl]) for sl in sls]
    be_s = [stack(b_e[:, sl]) for sl in sls]
    yield
    inverse_round()
    ke_s = [stack(k_e[:, sl]) for sl in sls]
    yield
    inverse_round()
    yield

    a_eff = [_bdot(t_inv[g], a_s[g]) for g in groups]
    av = [_bdot(a_ak[g], v_s[g]) for g in groups]
    u_const = [_bdot(t_inv[g], av[g]) for g in groups]
    yield
    state = [s_ref[g] for g in groups]
    sb = [state[g].astype(BF16) for g in groups]
    u = [_bdot_nt(a_eff[g], sb[g]) + u_const[g] for g in groups]
    y_st = [_bdot_nt(r_s[g], sb[g]) + _bdot(a_rb[g], u[g]) + _bdot(a_rk[g], v_s[g]) for g in groups]
    for g in groups:
        s_ref[g] = (state[g] * decay_end[:, sls[g]]
                    + _bdot_tn(jnp.concatenate([u[g].astype(BF16), v_s[g]], axis=0),
                               jnp.concatenate([be_s[g], ke_s[g]], axis=0)))
    y = jnp.concatenate([t[:c] + t[c:2 * c] + t[2 * c:3 * c] + t[3 * c:] for t in y_st], axis=-1)

    (y_sum,) = _head_sums([y], ones_ref)
    d = y - y_sum * (1.0 / RWKV_HEAD)
    (d_sq,) = _head_sums([d * d], ones_ref)
    y = d * lax.rsqrt(d_sq * (1.0 / RWKV_HEAD) + GN_EPS_RWKV) * lnx_w + lnx_b + bonus
    o_ref[...] = (y * g_rw).astype(BF16)


def _emit_interleaved(primary, secondary, pieces_after):
    for n in pieces_after:
        next(primary, None)
        for _ in range(n):
            next(secondary, None)
    for _ in primary:
        pass
    for _ in secondary:
        pass


def _mixer_kernel(pos_ref, zr_ref, zs_ref, invf_ref, dintra_ref, zeta_ref, xi_ref, cdec_ref, par_ref, wd_ref, wa_ref,
                  wg_ref, ltri_ref, ones_ref, mstrict_ref, mincl_ref, or_ref, ow_ref, rs_ref, ws_ref):
    @pl.when(pl.program_id(1) == 0)
    def _():
        rs_ref[...] = jnp.zeros_like(rs_ref)
        ws_ref[...] = jnp.zeros_like(ws_ref)

    _emit_interleaved(
        _rwkv_stages(zs_ref, par_ref, wd_ref, wa_ref, wg_ref, ltri_ref, ones_ref, mstrict_ref, mincl_ref, ow_ref, ws_ref),
        _ret_stages(pos_ref, zr_ref, invf_ref, dintra_ref, zeta_ref, xi_ref, cdec_ref, or_ref, rs_ref),
        pieces_after=(1, 2, 1, 1, 2, 1))


def _rwkv_tables():
    c = RWKV_CHUNK
    ltri = (jnp.arange(c)[:, None] >= jnp.arange(c)[None, :]).astype(BF16)
    row = jnp.arange(GROUP_LANES)[:, None]
    col = jnp.arange(GROUP_LANES)[None, :]
    same_head = (row // RWKV_HEAD) == (col // RWKV_HEAD)
    ones = same_head.astype(BF16)
    m_strict = (same_head & (row > col)).astype(F32)
    m_incl = (same_head & (row >= col)).astype(F32)
    return ltri, ones, m_strict, m_incl


def _mixers(pos_col, z, par, wd, wa, wg):
    assert RWKV_CHUNK == RWKV_HEAD and RET_CHUNK == RWKV_CHUNK
    c = RWKV_CHUNK
    ret_tables = _retention_tables()
    rwkv_tables = _rwkv_tables()
    nch = SEQ // c
    full = lambda a: pl.BlockSpec(a.shape, lambda b, n: (0,) * a.ndim)
    rows = lambda width, col: pl.BlockSpec((c, width), lambda b, n: (b * nch + n, col))
    return pl.pallas_call(
        _mixer_kernel,
        grid=(BATCH, nch),
        in_specs=[rows(1, 0), rows(SEC, 0), rows(SEC, 1), *map(full, ret_tables),
                  full(par), full(wd), full(wa), full(wg), *map(full, rwkv_tables)],
        out_specs=[rows(RET_WIDTH, 0), rows(RWKV_WIDTH, 0)],
        out_shape=[jax.ShapeDtypeStruct((TOKENS, RET_WIDTH), BF16), jax.ShapeDtypeStruct((TOKENS, RWKV_WIDTH), BF16)],
        scratch_shapes=[pltpu.VMEM((RET_HEADS, RET_DIM, RET_DIM), F32),
                        pltpu.VMEM((RWKV_GROUPS, GROUP_LANES, GROUP_LANES), F32)],
        compiler_params=_params("arbitrary", "arbitrary"),
        name="mixers",
    )(pos_col, z, z, *ret_tables, par, wd, wa, wg, *rwkv_tables)


MG_TM = 1024
MG_TN = 1024


def _merge_kernel(yr_ref, yw_ref, wr_ref, ww_ref, gr_ref, gw_ref, br_ref, bw_ref, o_ref, wrb_ref, wwb_ref):
    @pl.when(pl.program_id(1) == 0)
    def _():
        wrb_ref[...] = wr_ref[...].astype(BF16)
        wwb_ref[...] = ww_ref[...].astype(BF16)

    pr = jnp.dot(yr_ref[...], wrb_ref[...], preferred_element_type=F32)
    pw = jnp.dot(yw_ref[...], wwb_ref[...], preferred_element_type=F32)
    o = jax.nn.sigmoid(gr_ref[...] + br_ref[...]) * pr + jax.nn.sigmoid(gw_ref[...] + bw_ref[...]) * pw
    o_ref[...] = o.astype(BF16)


def _merge(y_ret, y_rw, w_ret, w_rw, z, b_gate):
    g0 = 2 * SEC // MG_TN
    nd = D_MODEL // MG_TN
    return pl.pallas_call(
        _merge_kernel,
        grid=(nd, TOKENS // MG_TM),
        in_specs=[pl.BlockSpec((MG_TM, RET_WIDTH), lambda j, i: (i, 0)),
                  pl.BlockSpec((MG_TM, RWKV_WIDTH), lambda j, i: (i, 0)),
                  pl.BlockSpec((RET_WIDTH, MG_TN), lambda j, i: (0, j)),
                  pl.BlockSpec((RWKV_WIDTH, MG_TN), lambda j, i: (0, j)),
                  pl.BlockSpec((MG_TM, MG_TN), lambda j, i: (i, g0 + j)),
                  pl.BlockSpec((MG_TM, MG_TN), lambda j, i: (i, g0 + nd + j)),
                  pl.BlockSpec((1, MG_TN), lambda j, i: (0, j)),
                  pl.BlockSpec((1, MG_TN), lambda j, i: (0, nd + j))],
        out_specs=pl.BlockSpec((MG_TM, MG_TN), lambda j, i: (i, j)),
        out_shape=jax.ShapeDtypeStruct((TOKENS, D_MODEL), BF16),
        scratch_shapes=[pltpu.VMEM((RET_WIDTH, MG_TN), BF16), pltpu.VMEM((RWKV_WIDTH, MG_TN), BF16)],
        compiler_params=_params("arbitrary", "arbitrary"),
        name="merge",
    )(y_ret, y_rw, w_ret, w_rw, z, z, b_gate, b_gate)


OP_TM = 512


def _oproj_kernel(m_ref, w_ref, x_ref, mod_ref, gain_ref, h_ref, u_ref, wb_ref):
    @pl.when(pl.program_id(0) == 0)
    def _():
        wb_ref[...] = w_ref[...].astype(BF16)

    mod = mod_ref[0]
    h = x_ref[...] + mod[2:3] * jnp.dot(m_ref[...], wb_ref[...], preferred_element_type=F32)
    h_ref[...] = h
    u_ref[...] = _modulated_norm(h, gain_ref[...], mod[4:5], mod[3:4]).astype(BF16)


def _oproj(merged, w_o, x2, mod3, gain):
    blocks_per_batch = SEQ // OP_TM
    return pl.pallas_call(
        _oproj_kernel,
        grid=(TOKENS // OP_TM,),
        in_specs=[pl.BlockSpec((OP_TM, D_MODEL), lambda i: (i, 0)),
                  pl.BlockSpec((D_MODEL, D_MODEL), lambda i: (0, 0), pipeline_mode=pl.Buffered(1)),
                  pl.BlockSpec((OP_TM, D_MODEL), lambda i: (i, 0)),
                  pl.BlockSpec((1, 6, D_MODEL), lambda i: (i // blocks_per_batch, 0, 0)),
                  pl.BlockSpec((1, D_MODEL), lambda i: (0, 0))],
        out_specs=[pl.BlockSpec((OP_TM, D_MODEL), lambda i: (i, 0)),
                   pl.BlockSpec((OP_TM, D_MODEL), lambda i: (i, 0))],
        out_shape=[jax.ShapeDtypeStruct((TOKENS, D_MODEL), F32),
                   jax.ShapeDtypeStruct((TOKENS, D_MODEL), BF16)],
        scratch_shapes=[pltpu.VMEM((D_MODEL, D_MODEL), BF16)],
        compiler_params=_params("arbitrary"),
        name="oproj",
    )(merged, w_o, x2, mod3, gain)


FI_TM = 2048
FI_TN = 512


def _ffn_in_kernel(u_ref, wg_ref, wu_ref, o_ref, wgb_ref, wub_ref):
    @pl.when(pl.program_id(1) == 0)
    def _():
        wgb_ref[...] = wg_ref[...].astype(BF16)
        wub_ref[...] = wu_ref[...].astype(BF16)

    u = u_ref[...]
    gate = jnp.dot(u, wgb_ref[...], preferred_element_type=F32)
    up = jnp.dot(u, wub_ref[...], preferred_element_type=F32)
    o_ref[...] = (_silu(gate) * up).astype(BF16)


def _ffn_in(u2, w_in):
    nt = FFN_HIDDEN // FI_TN
    return pl.pallas_call(
        _ffn_in_kernel,
        grid=(nt, TOKENS // FI_TM),
        in_specs=[pl.BlockSpec((FI_TM, D_MODEL), lambda j, i: (i, 0)),
                  pl.BlockSpec((D_MODEL, FI_TN), lambda j, i: (0, j)),
                  pl.BlockSpec((D_MODEL, FI_TN), lambda j, i: (0, nt + j))],
        out_specs=pl.BlockSpec((FI_TM, FI_TN), lambda j, i: (i, j)),
        out_shape=jax.ShapeDtypeStruct((TOKENS, FFN_HIDDEN), BF16),
        scratch_shapes=[pltpu.VMEM((D_MODEL, FI_TN), BF16), pltpu.VMEM((D_MODEL, FI_TN), BF16)],
        compiler_params=_params("arbitrary", "arbitrary"),
        name="ffn_in",
    )(u2, w_in, w_in)


FO_TM = 1024
FO_TK = 512


def _ffn_out_kernel(a_ref, w_ref, h_ref, mod_ref, gain_ref, o_ref, acc_ref):
    k = pl.program_id(1)

    @pl.when(k == 0)
    def _():
        acc_ref[...] = jnp.zeros_like(acc_ref)

    acc_ref[...] += jnp.dot(a_ref[...], w_ref[...].astype(BF16), preferred_element_type=F32)

    @pl.when(k == pl.num_programs(1) - 1)
    def _():
        h = h_ref[...] + mod_ref[0][5:6] * acc_ref[...]
        o_ref[...] = h * lax.rsqrt(jnp.mean(h * h, axis=-1, keepdims=True) + NORM_EPS) * gain_ref[...]


def _ffn_out(act, w_out, h1, mod3, gain):
    blocks_per_batch = SEQ // FO_TM
    return pl.pallas_call(
        _ffn_out_kernel,
        grid=(TOKENS // FO_TM, FFN_HIDDEN // FO_TK),
        in_specs=[pl.BlockSpec((FO_TM, FO_TK), lambda i, k: (i, k)),
                  pl.BlockSpec((FO_TK, D_MODEL), lambda i, k: (k, 0)),
                  pl.BlockSpec((FO_TM, D_MODEL), lambda i, k: (i, 0)),
                  pl.BlockSpec((1, 6, D_MODEL), lambda i, k: (i // blocks_per_batch, 0, 0)),
                  pl.BlockSpec((1, D_MODEL), lambda i, k: (0, 0))],
        out_specs=pl.BlockSpec((FO_TM, D_MODEL), lambda i, k: (i, 0)),
        out_shape=jax.ShapeDtypeStruct((TOKENS, D_MODEL), F32),
        scratch_shapes=[pltpu.VMEM((FO_TM, D_MODEL), F32)],
        compiler_params=_params("arbitrary", "arbitrary", vmem=VMEM_LIMIT_MAX),
        name="ffn_out",
    )(act, w_out, h1, mod3, gain)


def _pad_cols(t, width):
    return jnp.pad(t, ((0, 0), (0, width - t.shape[1])))


def _pad_rows(t, rows):
    return jnp.pad(t, ((0, rows - t.shape[0]), (0, 0)))


def _pack_lora_rows(w_t):
    o3 = HEAD_COLS
    o4 = o3 + DECAY_LORA
    o5 = o4 + ICLR_LORA
    lora = jnp.concatenate([_pad_rows(w_t[o3:o4], LORA_PAD), _pad_rows(w_t[o4:o5], LORA_PAD), w_t[o5:GATE_ROW0]],
                           axis=0)
    return _pad_rows(lora, 2 * SEC - HEAD_COLS)


def _pack_mu(mu):
    o3 = 3 * RWKV_WIDTH
    o4 = o3 + DECAY_LORA
    o5 = o4 + ICLR_LORA
    mu = mu[None, :]
    rw = jnp.concatenate([mu[:, :o3], _pad_cols(mu[:, o3:o4], LORA_PAD), _pad_cols(mu[:, o4:o5], LORA_PAD),
                          mu[:, o5:]], axis=1)
    return _pad_cols(rw, SEC)


def kernel(x, c, positions, w_ada, b_ada, norm_mix, norm_ffn, norm_final, w_in, b_gate, mu_shift, w0, w_decay_up, a0, w_iclr_up, w_gate_up, k_k, k_a, r_k, lnx_w, lnx_b, w_ret_out, w_rwkv_out, w_o, w_ffn_in, w_ffn_out):
    assert x.shape == (BATCH, SEQ, D_MODEL) and w_ada.shape[0] == 1
    x2 = x.reshape(TOKENS, D_MODEL)
    pos_col = positions.reshape(TOKENS, 1)

    mod = _ada(_pad_rows(c, 8), w_ada[0], b_ada)
    mod3 = mod[:BATCH].reshape(BATCH, 6, D_MODEL)

    w_t = jnp.swapaxes(w_in, 1, 2)[0]
    u1 = _norm1(x2, mod3, norm_mix)
    z = _inproj(u1, w_t, _pack_lora_rows(w_t), _pack_mu(mu_shift[0]))

    par = jnp.concatenate([w0, a0, k_k, k_a, r_k.reshape(1, RWKV_WIDTH), lnx_w, lnx_b,
                           jnp.zeros((1, RWKV_WIDTH), F32)], axis=0)
    y_ret, y_rw = _mixers(pos_col, z, par, _pad_rows(w_decay_up[0], LORA_PAD).astype(BF16),
                          _pad_rows(w_iclr_up[0], LORA_PAD).astype(BF16), w_gate_up[0].astype(BF16))

    merged = _merge(y_ret, y_rw, w_ret_out[0], w_rwkv_out[0], z, b_gate)
    h1, u2 = _oproj(merged, w_o[0], x2, mod3, norm_ffn)
    act = _ffn_in(u2, w_ffn_in[0])
    out = _ffn_out(act, w_ffn_out[0], h1, mod3, norm_final[None, :])
    return out.reshape(BATCH, SEQ, D_MODEL)
```

```python
import jax
import jax.numpy as jnp
from jax import lax
from jax.experimental import pallas as pl
from jax.experimental.pallas import tpu as pltpu

F32 = jnp.float32
BF16 = jnp.bfloat16

D_MODEL = 2048
BATCH = 2
SEQ = 4096
TOKENS = BATCH * SEQ

RET_HEADS = 4
RET_DIM = 256
RET_WIDTH = RET_HEADS * RET_DIM
RET_CHUNK = 64
ROPE_BASE = 10000.0

RWKV_HEAD = 64
RWKV_WIDTH = 1024
RWKV_HEADS = RWKV_WIDTH // RWKV_HEAD
DECAY_LORA = 96
ICLR_LORA = 96
GATE_LORA = 256
RWKV_CHUNK = 64
GROUP_LANES = 256
GROUP_HEADS = GROUP_LANES // RWKV_HEAD
RWKV_GROUPS = RWKV_WIDTH // GROUP_LANES
LORA_PAD = 128

FFN_HIDDEN = ((8 * D_MODEL // 3 + 255) // 256) * 256
RET_COLS = 4 * RET_WIDTH
SHIFT_COLS = 3 * RWKV_WIDTH + DECAY_LORA + ICLR_LORA + GATE_LORA
GATE_COLS = 2 * D_MODEL
SEC = 4096
Z_COLS = 3 * SEC
HEAD_COLS = RET_COLS + 3 * RWKV_WIDTH
NORM_EPS = 1e-6
GN_EPS_RET = 1e-5
GN_EPS_RWKV = 64e-5

VMEM_LIMIT = 56 * 1024 * 1024
VMEM_LIMIT_MAX = 60 * 1024 * 1024
SUBLANES = 8


def _params(*sem, vmem=VMEM_LIMIT):
    return pltpu.CompilerParams(dimension_semantics=sem, vmem_limit_bytes=vmem)


def _bdot(a, b):
    return jnp.dot(a.astype(BF16), b.astype(BF16), preferred_element_type=F32)


def _bdot_nt(a, b):
    return lax.dot_general(a.astype(BF16), b.astype(BF16), (((1,), (1,)), ((), ())),
                           preferred_element_type=F32)


def _bdot_tn(a, b):
    return lax.dot_general(a.astype(BF16), b.astype(BF16), (((0,), (0,)), ((), ())),
                           preferred_element_type=F32)


def _silu(x):
    return x * jax.nn.sigmoid(x)


ADA_TN = 1024


def _ada_kernel(c_ref, w_ref, b_ref, o_ref):
    o_ref[...] = _bdot(_silu(c_ref[...]), w_ref[...]) + b_ref[...]


def _ada(c_pad, w_ada, b_ada):
    n = w_ada.shape[1]
    return pl.pallas_call(
        _ada_kernel,
        grid=(n // ADA_TN,),
        in_specs=[pl.BlockSpec((8, D_MODEL), lambda j: (0, 0)),
                  pl.BlockSpec((D_MODEL, ADA_TN), lambda j: (0, j)),
                  pl.BlockSpec((1, ADA_TN), lambda j: (0, j))],
        out_specs=pl.BlockSpec((8, ADA_TN), lambda j: (0, j)),
        out_shape=jax.ShapeDtypeStruct((8, n), F32),
        compiler_params=_params("arbitrary"),
        name="ada",
    )(c_pad, w_ada, b_ada)


IN_TM = 1024
IN_TN = 1024


def _modulated_norm(x, gain, scale, shift):
    y = x * lax.rsqrt(jnp.mean(x * x, axis=-1, keepdims=True) + NORM_EPS)
    return y * gain * (1.0 + scale) + shift


def _norm1_kernel(x_ref, mod_ref, gain_ref, u_ref):
    mod = mod_ref[0]
    u_ref[...] = _modulated_norm(x_ref[...], gain_ref[...], mod[1:2], mod[0:1]).astype(BF16)


def _norm1(x2, mod3, gain):
    blocks_per_batch = SEQ // IN_TM
    return pl.pallas_call(
        _norm1_kernel,
        grid=(TOKENS // IN_TM,),
        in_specs=[pl.BlockSpec((IN_TM, D_MODEL), lambda i: (i, 0)),
                  pl.BlockSpec((1, 6, D_MODEL), lambda i: (i // blocks_per_batch, 0, 0)),
                  pl.BlockSpec((1, D_MODEL), lambda i: (0, 0))],
        out_specs=pl.BlockSpec((IN_TM, D_MODEL), lambda i: (i, 0)),
        out_shape=jax.ShapeDtypeStruct((TOKENS, D_MODEL), BF16),
        compiler_params=_params("arbitrary"),
        name="norm1",
    )(x2, mod3, gain)


IN_LORA_BLOCK = HEAD_COLS // IN_TN
IN_RWKV_BLOCK0 = RET_COLS // IN_TN
GATE_ROW0 = RET_COLS + SHIFT_COLS


def _inproj_kernel(u_ref, w_ref, wl_ref, mu_ref, o_ref, wb_ref, carry_ref):
    j = pl.program_id(0)
    i = pl.program_id(1)

    @pl.when((i == 0) & (j != IN_LORA_BLOCK))
    def _():
        wb_ref[...] = w_ref[...].astype(BF16)

    @pl.when((i == 0) & (j == IN_LORA_BLOCK))
    def _():
        wb_ref[...] = wl_ref[...].astype(BF16)

    z = _bdot_nt(u_ref[...], wb_ref[...])
    shifted = (j >= IN_RWKV_BLOCK0) & (j <= IN_LORA_BLOCK)

    @pl.when(jnp.logical_not(shifted))
    def _():
        o_ref[...] = z

    @pl.when(shifted)
    def _():
        @pl.when(i % (SEQ // IN_TM) == 0)
        def _():
            carry_ref[...] = jnp.zeros_like(carry_ref)

        mu = mu_ref[...]
        prev = pltpu.roll(z, 1, 0)
        o_ref[...] = z + (prev - z) * mu
        top = z[:SUBLANES]
        row = lax.broadcasted_iota(jnp.int32, top.shape, 0)
        prev_top = jnp.where(row == 0, carry_ref[...], prev[:SUBLANES])
        o_ref[:SUBLANES, :] = top + (prev_top - top) * mu
        carry_ref[...] = z[IN_TM - 1:IN_TM, :]


def _inproj(u, w_t, w_lora, mu_p):
    assert HEAD_COLS % IN_TN == 0 and 2 * SEC - HEAD_COLS == IN_TN

    def w_row(j, i):
        tn, g0 = IN_TN // SUBLANES, GATE_ROW0 // SUBLANES
        head = jnp.minimum(j, IN_LORA_BLOCK - 1) * tn
        return (jnp.where(j > IN_LORA_BLOCK, g0 + (j - IN_LORA_BLOCK - 1) * tn, head) * SUBLANES, 0)

    return pl.pallas_call(
        _inproj_kernel,
        grid=(Z_COLS // IN_TN, TOKENS // IN_TM),
        in_specs=[pl.BlockSpec((IN_TM, D_MODEL), lambda j, i: (i, 0)),
                  pl.BlockSpec((pl.Element(IN_TN), pl.Element(D_MODEL)), w_row),
                  pl.BlockSpec((IN_TN, D_MODEL), lambda j, i: (0, 0), pipeline_mode=pl.Buffered(1)),
                  pl.BlockSpec((1, IN_TN), lambda j, i: (0, jnp.clip(j - IN_RWKV_BLOCK0, 0, SEC // IN_TN - 1)))],
        out_specs=pl.BlockSpec((IN_TM, IN_TN), lambda j, i: (i, j)),
        out_shape=jax.ShapeDtypeStruct((TOKENS, Z_COLS), F32),
        scratch_shapes=[pltpu.VMEM((IN_TN, D_MODEL), BF16), pltpu.VMEM((1, IN_TN), F32)],
        compiler_params=_params("arbitrary", "arbitrary"),
        name="inproj",
    )(u, w_t, w_lora, mu_p)


def _ret_stages(pos_ref, z_ref, invf_ref, dintra_ref, zeta_ref, xi_ref, cdec_ref, o_ref, state_ref):
    ang = pos_ref[...].astype(F32) * invf_ref[...]
    cos = jnp.cos(ang)
    sin = jnp.sin(ang)
    half = RET_DIM // 2
    yield

    def rot(t):
        t1, t2 = t[:, :half], t[:, half:]
        return jnp.concatenate([t1 * cos - t2 * sin, t1 * sin + t2 * cos], axis=-1)

    heads = range(RET_HEADS)
    cols = lambda section, h: slice(section * RET_WIDTH + h * RET_DIM, section * RET_WIDTH + (h + 1) * RET_DIM)
    qb = [rot(z_ref[:, cols(0, h)]).astype(BF16) for h in heads]
    yield
    k = [rot(z_ref[:, cols(1, h)]) * (RET_DIM ** -0.5) for h in heads]
    v = [z_ref[:, cols(2, h)].astype(BF16) for h in heads]
    yield
    scores = [_bdot_nt(qb[h], k[h]) * dintra_ref[h] for h in heads]
    state = [state_ref[h] for h in heads]
    yield
    out = [_bdot(scores[h], v[h]) + _bdot(qb[h], state[h]) * xi_ref[h] for h in heads]
    for h in heads:
        state_ref[h] = state[h] * cdec_ref[h] + _bdot_tn(k[h] * zeta_ref[h], v[h])
    yield
    for h in heads:
        mu = jnp.mean(out[h], axis=-1, keepdims=True)
        d = out[h] - mu
        var = jnp.mean(d * d, axis=-1, keepdims=True)
        g = z_ref[:, cols(3, h)]
        o_ref[:, cols(0, h)] = (_silu(g) * (d * lax.rsqrt(var + GN_EPS_RET))).astype(BF16)
        if h % 2:
            yield


def _retention_tables():
    h = RET_HEADS
    half = RET_DIM // 2
    inv_freq = ROPE_BASE ** (-jnp.arange(half, dtype=F32) / half)
    log_gamma = jnp.log(1.0 - 2.0 ** (-5.0 - jnp.arange(h, dtype=F32)))
    idx = jnp.arange(RET_CHUNK, dtype=F32)
    dist = idx[:, None] - idx[None, :]
    decay_intra = jnp.where(dist >= 0, jnp.exp(log_gamma[:, None, None] * jnp.maximum(dist, 0.0)), 0.0)
    zeta = jnp.exp(log_gamma[:, None] * (RET_CHUNK - 1.0 - idx))
    xi = jnp.exp(log_gamma[:, None] * (idx + 1.0))
    chunk_decay = jnp.exp(log_gamma * RET_CHUNK)
    wide = lambda t: jnp.broadcast_to(t[:, :, None], (h, RET_CHUNK, RET_DIM))
    cdec = jnp.broadcast_to(chunk_decay[:, None, None], (h, 1, RET_DIM))
    return inv_freq[None, :], decay_intra, wide(zeta), wide(xi), cdec


def _split2(x):
    hi = x.astype(BF16)
    return hi, (x - hi.astype(F32)).astype(BF16)


def _head_sums(xs, ones_ref):
    c = RWKV_CHUNK
    pieces = []
    for x in xs:
        for p in _split2(x):
            for g in range(RWKV_GROUPS):
                pieces.append(p[:, g * GROUP_LANES:(g + 1) * GROUP_LANES])
    res = jnp.dot(jnp.concatenate(pieces, axis=0), ones_ref[...], preferred_element_type=F32)
    outs = []
    for i in range(len(xs)):
        base = i * 2 * RWKV_GROUPS * c
        cols = []
        for g in range(RWKV_GROUPS):
            hi = res[base + g * c:base + (g + 1) * c]
            lo = res[base + (RWKV_GROUPS + g) * c:base + (RWKV_GROUPS + g + 1) * c]
            cols.append(hi + lo)
        outs.append(jnp.concatenate(cols, axis=-1))
    return outs


def _rwkv_stages(zs_ref, par_ref, wd_ref, wa_ref, wg_ref, ltri_ref, ones_ref, mstrict_ref, mincl_ref,
                 o_ref, s_ref):
    c = RWKV_CHUNK
    w = RWKV_WIDTH

    r, kw, vw = zs_ref[:, :w], zs_ref[:, w:2 * w], zs_ref[:, 2 * w:3 * w]
    zw = zs_ref[:, 3 * w:3 * w + LORA_PAD]
    za = zs_ref[:, 3 * w + LORA_PAD:3 * w + 2 * LORA_PAD]
    zg = zs_ref[:, 3 * w + 2 * LORA_PAD:3 * w + 2 * LORA_PAD + GATE_LORA]
    par = par_ref[...]
    w0, a0, k_k, k_a, r_k, lnx_w, lnx_b = (par[i:i + 1] for i in range(7))

    dec_pre = w0 + _bdot(jnp.tanh(zw), wd_ref[...])
    softplus = jnp.maximum(-dec_pre, 0.0) + jnp.log1p(jnp.exp(-jnp.abs(dec_pre)))
    logdec = -jnp.exp(-softplus - 0.5)
    iclr = jax.nn.sigmoid(a0 + _bdot(za, wa_ref[...]))

    kk = kw * k_k
    k_mod = kw * (1.0 + (iclr - 1.0) * k_a)
    (kk_sq,) = _head_sums([kk * kk], ones_ref)
    kk = kk * (1.0 / jnp.maximum(jnp.sqrt(kk_sq), 1e-12))
    a = -kk
    b = kk * iclr

    ld_hi = logdec.astype(BF16)
    ld_mid, ld_lo = _split2(logdec - ld_hi.astype(F32))
    ltri = ltri_ref[...]
    cum = (jnp.dot(ltri, ld_hi, preferred_element_type=F32) + jnp.dot(ltri, ld_mid, preferred_element_type=F32)
           + jnp.dot(ltri, ld_lo, preferred_element_type=F32))
    cum_end = cum[c - 1:c, :]
    p_inv = jnp.exp(-cum)
    a_t = a * jnp.exp(cum - logdec)
    r_t = r * jnp.exp(cum)
    b_t = (b * p_inv).astype(BF16)
    k_t = (k_mod * p_inv).astype(BF16)

    lane_head = lax.shift_right_logical(lax.broadcasted_iota(jnp.int32, (c, GROUP_LANES), 1), 6)
    head_masks = [lane_head == h for h in range(GROUP_HEADS)]

    def stack(x):
        return jnp.concatenate([jnp.where(m, x, 0.0).astype(BF16) for m in head_masks], axis=0)

    def rep(x):
        return jnp.concatenate([x] * GROUP_HEADS, axis=0)

    m_strict = mstrict_ref[...]
    m_incl = mincl_ref[...]
    eye = (lax.broadcasted_iota(jnp.int32, (GROUP_LANES, GROUP_LANES), 0)
           == lax.broadcasted_iota(jnp.int32, (GROUP_LANES, GROUP_LANES), 1)).astype(F32)
    n4 = GROUP_HEADS * c
    groups = range(RWKV_GROUPS)
    sls = [slice(g * GROUP_LANES, (g + 1) * GROUP_LANES) for g in groups]
    a_s = [stack(a_t[:, sl]) for sl in sls]
    r_s = [stack(r_t[:, sl]) for sl in sls]
    blk = [_bdot_nt(jnp.concatenate([a_s[g], r_s[g]], axis=0),
                    jnp.concatenate([rep(b_t[:, sls[g]]), rep(k_t[:, sls[g]])], axis=0)) for g in groups]
    a_ab = [blk[g][:n4, :n4] * m_strict for g in groups]
    a_ak = [blk[g][:n4, n4:] * m_strict for g in groups]
    a_rb = [blk[g][n4:, :n4] * m_incl for g in groups]
    a_rk = [blk[g][n4:, n4:] * m_incl for g in groups]

    t_inv = [eye + a_ab[g] for g in groups]
    a_pow = a_ab

    def inverse_round():
        nonlocal a_pow, t_inv
        a_pow = [_bdot(a_pow[g], a_pow[g]) for g in groups]
        t_inv = [t_inv[g] + _bdot(t_inv[g], a_pow[g]) for g in groups]

    yield
    inverse_round()
    g_rw = _bdot(jax.nn.sigmoid(zg), wg_ref[...])
    (rk_sum,) = _head_sums([r * k_mod * r_k], ones_ref)
    bonus = rk_sum * vw
    yield
    inverse_round()
    p_end = jnp.exp(cum_end - cum)
    b_e = b * p_end
    k_e = k_mod * p_end
    decay_end = jnp.exp(cum_end)
    yield
    inverse_round()
    v_s = [stack(vw[:, sl]) for sl in sls]
    be_s = [stack(b_e[:, sl]) for sl in sls]
    yield
    inverse_round()
    ke_s = [stack(k_e[:, sl]) for sl in sls]
    yield
    inverse_round()
    yield

    a_eff = [_bdot(t_inv[g], a_s[g]) for g in groups]
    av = [_bdot(a_ak[g], v_s[g]) for g in groups]
    u_const = [_bdot(t_inv[g], av[g]) for g in groups]
    yield
    state = [s_ref[g] for g in groups]
    sb = [state[g].astype(BF16) for g in groups]
    u = [_bdot_nt(a_eff[g], sb[g]) + u_const[g] for g in groups]
    y_st = [_bdot_nt(r_s[g], sb[g]) + _bdot(a_rb[g], u[g]) + _bdot(a_rk[g], v_s[g]) for g in groups]
    for g in groups:
        s_ref[g] = (state[g] * decay_end[:, sls[g]]
                    + _bdot_tn(jnp.concatenate([u[g].astype(BF16), v_s[g]], axis=0),
                               jnp.concatenate([be_s[g], ke_s[g]], axis=0)))
    y = jnp.concatenate([t[:c] + t[c:2 * c] + t[2 * c:3 * c] + t[3 * c:] for t in y_st], axis=-1)

    (y_sum,) = _head_sums([y], ones_ref)
    d = y - y_sum * (1.0 / RWKV_HEAD)
    (d_sq,) = _head_sums([d * d], ones_ref)
    y = d * lax.rsqrt(d_sq * (1.0 / RWKV_HEAD) + GN_EPS_RWKV) * lnx_w + lnx_b + bonus
    o_ref[...] = (y * g_rw).astype(BF16)


def _emit_interleaved(primary, secondary, pieces_after):
    for n in pieces_after:
        next(primary, None)
        for _ in range(n):
            next(secondary, None)
    for _ in primary:
        pass
    for _ in secondary:
        pass


def _mixer_kernel(pos_ref, zr_ref, zs_ref, invf_ref, dintra_ref, zeta_ref, xi_ref, cdec_ref, par_ref, wd_ref, wa_ref,
                  wg_ref, ltri_ref, ones_ref, mstrict_ref, mincl_ref, or_ref, ow_ref, rs_ref, ws_ref):
    @pl.when(pl.program_id(1) == 0)
    def _():
        rs_ref[...] = jnp.zeros_like(rs_ref)
        ws_ref[...] = jnp.zeros_like(ws_ref)

    _emit_interleaved(
        _rwkv_stages(zs_ref, par_ref, wd_ref, wa_ref, wg_ref, ltri_ref, ones_ref, mstrict_ref, mincl_ref, ow_ref, ws_ref),
        _ret_stages(pos_ref, zr_ref, invf_ref, dintra_ref, zeta_ref, xi_ref, cdec_ref, or_ref, rs_ref),
        pieces_after=(1, 2, 1, 1, 2, 1))


def _rwkv_tables():
    c = RWKV_CHUNK
    ltri = (jnp.arange(c)[:, None] >= jnp.arange(c)[None, :]).astype(BF16)
    row = jnp.arange(GROUP_LANES)[:, None]
    col = jnp.arange(GROUP_LANES)[None, :]
    same_head = (row // RWKV_HEAD) == (col // RWKV_HEAD)
    ones = same_head.astype(BF16)
    m_strict = (same_head & (row > col)).astype(F32)
    m_incl = (same_head & (row >= col)).astype(F32)
    return ltri, ones, m_strict, m_incl


def _mixers(pos_col, z, par, wd, wa, wg):
    assert RWKV_CHUNK == RWKV_HEAD and RET_CHUNK == RWKV_CHUNK
    c = RWKV_CHUNK
    ret_tables = _retention_tables()
    rwkv_tables = _rwkv_tables()
    nch = SEQ // c
    full = lambda a: pl.BlockSpec(a.shape, lambda b, n: (0,) * a.ndim)
    rows = lambda width, col: pl.BlockSpec((c, width), lambda b, n: (b * nch + n, col))
    return pl.pallas_call(
        _mixer_kernel,
        grid=(BATCH, nch),
        in_specs=[rows(1, 0), rows(SEC, 0), rows(SEC, 1), *map(full, ret_tables),
                  full(par), full(wd), full(wa), full(wg), *map(full, rwkv_tables)],
        out_specs=[rows(RET_WIDTH, 0), rows(RWKV_WIDTH, 0)],
        out_shape=[jax.ShapeDtypeStruct((TOKENS, RET_WIDTH), BF16), jax.ShapeDtypeStruct((TOKENS, RWKV_WIDTH), BF16)],
        scratch_shapes=[pltpu.VMEM((RET_HEADS, RET_DIM, RET_DIM), F32),
                        pltpu.VMEM((RWKV_GROUPS, GROUP_LANES, GROUP_LANES), F32)],
        compiler_params=_params("arbitrary", "arbitrary"),
        name="mixers",
    )(pos_col, z, z, *ret_tables, par, wd, wa, wg, *rwkv_tables)


MG_TM = 1024
MG_TN = 1024


def _merge_kernel(yr_ref, yw_ref, wr_ref, ww_ref, gr_ref, gw_ref, br_ref, bw_ref, o_ref, wrb_ref, wwb_ref):
    @pl.when(pl.program_id(1) == 0)
    def _():
        wrb_ref[...] = wr_ref[...].astype(BF16)
        wwb_ref[...] = ww_ref[...].astype(BF16)

    pr = jnp.dot(yr_ref[...], wrb_ref[...], preferred_element_type=F32)
    pw = jnp.dot(yw_ref[...], wwb_ref[...], preferred_element_type=F32)
    o = jax.nn.sigmoid(gr_ref[...] + br_ref[...]) * pr + jax.nn.sigmoid(gw_ref[...] + bw_ref[...]) * pw
    o_ref[...] = o.astype(BF16)


def _merge(y_ret, y_rw, w_ret, w_rw, z, b_gate):
    g0 = 2 * SEC // MG_TN
    nd = D_MODEL // MG_TN
    return pl.pallas_call(
        _merge_kernel,
        grid=(nd, TOKENS // MG_TM),
        in_specs=[pl.BlockSpec((MG_TM, RET_WIDTH), lambda j, i: (i, 0)),
                  pl.BlockSpec((MG_TM, RWKV_WIDTH), lambda j, i: (i, 0)),
                  pl.BlockSpec((RET_WIDTH, MG_TN), lambda j, i: (0, j)),
                  pl.BlockSpec((RWKV_WIDTH, MG_TN), lambda j, i: (0, j)),
                  pl.BlockSpec((MG_TM, MG_TN), lambda j, i: (i, g0 + j)),
                  pl.BlockSpec((MG_TM, MG_TN), lambda j, i: (i, g0 + nd + j)),
                  pl.BlockSpec((1, MG_TN), lambda j, i: (0, j)),
                  pl.BlockSpec((1, MG_TN), lambda j, i: (0, nd + j))],
        out_specs=pl.BlockSpec((MG_TM, MG_TN), lambda j, i: (i, j)),
        out_shape=jax.ShapeDtypeStruct((TOKENS, D_MODEL), BF16),
        scratch_shapes=[pltpu.VMEM((RET_WIDTH, MG_TN), BF16), pltpu.VMEM((RWKV_WIDTH, MG_TN), BF16)],
        compiler_params=_params("arbitrary", "arbitrary"),
        name="merge",
    )(y_ret, y_rw, w_ret, w_rw, z, z, b_gate, b_gate)


OP_TM = 512


def _oproj_kernel(m_ref, w_ref, x_ref, mod_ref, gain_ref, h_ref, u_ref, wb_ref):
    @pl.when(pl.program_id(0) == 0)
    def _():
        wb_ref[...] = w_ref[...].astype(BF16)

    mod = mod_ref[0]
    h = x_ref[...] + mod[2:3] * jnp.dot(m_ref[...], wb_ref[...], preferred_element_type=F32)
    h_ref[...] = h
    u_ref[...] = _modulated_norm(h, gain_ref[...], mod[4:5], mod[3:4]).astype(BF16)


def _oproj(merged, w_o, x2, mod3, gain):
    blocks_per_batch = SEQ // OP_TM
    return pl.pallas_call(
        _oproj_kernel,
        grid=(TOKENS // OP_TM,),
        in_specs=[pl.BlockSpec((OP_TM, D_MODEL), lambda i: (i, 0)),
                  pl.BlockSpec((D_MODEL, D_MODEL), lambda i: (0, 0), pipeline_mode=pl.Buffered(1)),
                  pl.BlockSpec((OP_TM, D_MODEL), lambda i: (i, 0)),
                  pl.BlockSpec((1, 6, D_MODEL), lambda i: (i // blocks_per_batch, 0, 0)),
                  pl.BlockSpec((1, D_MODEL), lambda i: (0, 0))],
        out_specs=[pl.BlockSpec((OP_TM, D_MODEL), lambda i: (i, 0)),
                   pl.BlockSpec((OP_TM, D_MODEL), lambda i: (i, 0))],
        out_shape=[jax.ShapeDtypeStruct((TOKENS, D_MODEL), F32),
                   jax.ShapeDtypeStruct((TOKENS, D_MODEL), BF16)],
        scratch_shapes=[pltpu.VMEM((D_MODEL, D_MODEL), BF16)],
        compiler_params=_params("arbitrary"),
        name="oproj",
    )(merged, w_o, x2, mod3, gain)


FI_TM = 1024
FI_TN = 512


def _ffn_in_kernel(u_ref, wg_ref, wu_ref, o_ref, wgb_ref, wub_ref):
    @pl.when(pl.program_id(1) == 0)
    def _():
        wgb_ref[...] = wg_ref[...].astype(BF16)
        wub_ref[...] = wu_ref[...].astype(BF16)

    u = u_ref[...]
    gate = jnp.dot(u, wgb_ref[...], preferred_element_type=F32)
    up = jnp.dot(u, wub_ref[...], preferred_element_type=F32)
    o_ref[...] = (_silu(gate) * up).astype(BF16)


def _ffn_in(u2, w_in):
    nt = FFN_HIDDEN // FI_TN
    return pl.pallas_call(
        _ffn_in_kernel,
        grid=(nt, TOKENS // FI_TM),
        in_specs=[pl.BlockSpec((FI_TM, D_MODEL), lambda j, i: (i, 0)),
                  pl.BlockSpec((D_MODEL, FI_TN), lambda j, i: (0, j)),
                  pl.BlockSpec((D_MODEL, FI_TN), lambda j, i: (0, nt + j))],
        out_specs=pl.BlockSpec((FI_TM, FI_TN), lambda j, i: (i, j)),
        out_shape=jax.ShapeDtypeStruct((TOKENS, FFN_HIDDEN), BF16),
        scratch_shapes=[pltpu.VMEM((D_MODEL, FI_TN), BF16), pltpu.VMEM((D_MODEL, FI_TN), BF16)],
        compiler_params=_params("arbitrary", "arbitrary"),
        name="ffn_in",
    )(u2, w_in, w_in)


FO_TM = 1024
FO_TK = 512


def _ffn_out_kernel(a_ref, w_ref, h_ref, mod_ref, gain_ref, o_ref, acc_ref):
    k = pl.program_id(1)

    @pl.when(k == 0)
    def _():
        acc_ref[...] = jnp.dot(a_ref[...], w_ref[...].astype(BF16), preferred_element_type=F32)

    @pl.when(k > 0)
    def _():
        acc_ref[...] += jnp.dot(a_ref[...], w_ref[...].astype(BF16), preferred_element_type=F32)

    @pl.when(k == pl.num_programs(1) - 1)
    def _():
        h = h_ref[...] + mod_ref[0][5:6] * acc_ref[...]
        o_ref[...] = h * lax.rsqrt(jnp.mean(h * h, axis=-1, keepdims=True) + NORM_EPS) * gain_ref[...]


def _ffn_out(act, w_out, h1, mod3, gain):
    blocks_per_batch = SEQ // FO_TM
    return pl.pallas_call(
        _ffn_out_kernel,
        grid=(TOKENS // FO_TM, FFN_HIDDEN // FO_TK),
        in_specs=[pl.BlockSpec((FO_TM, FO_TK), lambda i, k: (i, k)),
                  pl.BlockSpec((FO_TK, D_MODEL), lambda i, k: (k, 0)),
                  pl.BlockSpec((FO_TM, D_MODEL), lambda i, k: (i, 0)),
                  pl.BlockSpec((1, 6, D_MODEL), lambda i, k: (i // blocks_per_batch, 0, 0)),
                  pl.BlockSpec((1, D_MODEL), lambda i, k: (0, 0))],
        out_specs=pl.BlockSpec((FO_TM, D_MODEL), lambda i, k: (i, 0)),
        out_shape=jax.ShapeDtypeStruct((TOKENS, D_MODEL), F32),
        scratch_shapes=[pltpu.VMEM((FO_TM, D_MODEL), F32)],
        compiler_params=_params("arbitrary", "arbitrary", vmem=VMEM_LIMIT_MAX),
        name="ffn_out",
    )(act, w_out, h1, mod3, gain)


def _pad_cols(t, width):
    return jnp.pad(t, ((0, 0), (0, width - t.shape[1])))


def _pad_rows(t, rows):
    return jnp.pad(t, ((0, rows - t.shape[0]), (0, 0)))


def _pack_lora_rows(w_t):
    o3 = HEAD_COLS
    o4 = o3 + DECAY_LORA
    o5 = o4 + ICLR_LORA
    lora = jnp.concatenate([_pad_rows(w_t[o3:o4], LORA_PAD), _pad_rows(w_t[o4:o5], LORA_PAD), w_t[o5:GATE_ROW0]],
                           axis=0)
    return _pad_rows(lora, 2 * SEC - HEAD_COLS)


def _pack_mu(mu):
    o3 = 3 * RWKV_WIDTH
    o4 = o3 + DECAY_LORA
    o5 = o4 + ICLR_LORA
    mu = mu[None, :]
    rw = jnp.concatenate([mu[:, :o3], _pad_cols(mu[:, o3:o4], LORA_PAD), _pad_cols(mu[:, o4:o5], LORA_PAD),
                          mu[:, o5:]], axis=1)
    return _pad_cols(rw, SEC)


def kernel(x, c, positions, w_ada, b_ada, norm_mix, norm_ffn, norm_final, w_in, b_gate, mu_shift, w0, w_decay_up, a0, w_iclr_up, w_gate_up, k_k, k_a, r_k, lnx_w, lnx_b, w_ret_out, w_rwkv_out, w_o, w_ffn_in, w_ffn_out):
    assert x.shape == (BATCH, SEQ, D_MODEL) and w_ada.shape[0] == 1
    x2 = x.reshape(TOKENS, D_MODEL)
    pos_col = positions.reshape(TOKENS, 1)

    mod = _ada(_pad_rows(c, 8), w_ada[0], b_ada)
    mod3 = mod[:BATCH].reshape(BATCH, 6, D_MODEL)

    w_t = jnp.swapaxes(w_in, 1, 2)[0]
    u1 = _norm1(x2, mod3, norm_mix)
    z = _inproj(u1, w_t, _pack_lora_rows(w_t), _pack_mu(mu_shift[0]))

    par = jnp.concatenate([w0, a0, k_k, k_a, r_k.reshape(1, RWKV_WIDTH), lnx_w, lnx_b,
                           jnp.zeros((1, RWKV_WIDTH), F32)], axis=0)
    y_ret, y_rw = _mixers(pos_col, z, par, _pad_rows(w_decay_up[0], LORA_PAD).astype(BF16),
                          _pad_rows(w_iclr_up[0], LORA_PAD).astype(BF16), w_gate_up[0].astype(BF16))

    merged = _merge(y_ret, y_rw, w_ret_out[0], w_rwkv_out[0], z, b_gate)
    h1, u2 = _oproj(merged, w_o[0], x2, mod3, norm_ffn)
    act = _ffn_in(u2, w_ffn_in[0])
    out = _ffn_out(act, w_ffn_out[0], h1, mod3, norm_final[None, :])
    return out.reshape(BATCH, SEQ, D_MODEL)
```

```python
import jax
import jax.numpy as jnp
from jax import lax
from jax.experimental import pallas as pl
from jax.experimental.pallas import tpu as pltpu

F32 = jnp.float32
BF16 = jnp.bfloat16

D_MODEL = 2048
BATCH = 2
SEQ = 4096
TOKENS = BATCH * SEQ

RET_HEADS = 4
RET_DIM = 256
RET_WIDTH = RET_HEADS * RET_DIM
RET_CHUNK = 64
ROPE_BASE = 10000.0

RWKV_HEAD = 64
RWKV_WIDTH = 1024
RWKV_HEADS = RWKV_WIDTH // RWKV_HEAD
DECAY_LORA = 96
ICLR_LORA = 96
GATE_LORA = 256
RWKV_CHUNK = 64
GROUP_LANES = 256
GROUP_HEADS = GROUP_LANES // RWKV_HEAD
RWKV_GROUPS = RWKV_WIDTH // GROUP_LANES
LORA_PAD = 128

FFN_HIDDEN = ((8 * D_MODEL // 3 + 255) // 256) * 256
RET_COLS = 4 * RET_WIDTH
SHIFT_COLS = 3 * RWKV_WIDTH + DECAY_LORA + ICLR_LORA + GATE_LORA
GATE_COLS = 2 * D_MODEL
SEC = 4096
Z_COLS = 3 * SEC
HEAD_COLS = RET_COLS + 3 * RWKV_WIDTH
NORM_EPS = 1e-6
GN_EPS_RET = 1e-5
GN_EPS_RWKV = 64e-5

VMEM_LIMIT = 56 * 1024 * 1024
VMEM_LIMIT_MAX = 60 * 1024 * 1024
SUBLANES = 8


def _params(*sem, vmem=VMEM_LIMIT):
    return pltpu.CompilerParams(dimension_semantics=sem, vmem_limit_bytes=vmem)


def _bdot(a, b):
    return jnp.dot(a.astype(BF16), b.astype(BF16), preferred_element_type=F32)


def _bdot_nt(a, b):
    return lax.dot_general(a.astype(BF16), b.astype(BF16), (((1,), (1,)), ((), ())),
                           preferred_element_type=F32)


def _bdot_tn(a, b):
    return lax.dot_general(a.astype(BF16), b.astype(BF16), (((0,), (0,)), ((), ())),
                           preferred_element_type=F32)


def _silu(x):
    return x * jax.nn.sigmoid(x)


ADA_TN = 1024


def _ada_kernel(c_ref, w_ref, b_ref, o_ref):
    o_ref[...] = _bdot(_silu(c_ref[...]), w_ref[...]) + b_ref[...]


def _ada(c_pad, w_ada, b_ada):
    n = w_ada.shape[1]
    return pl.pallas_call(
        _ada_kernel,
        grid=(n // ADA_TN,),
        in_specs=[pl.BlockSpec((8, D_MODEL), lambda j: (0, 0)),
                  pl.BlockSpec((D_MODEL, ADA_TN), lambda j: (0, j)),
                  pl.BlockSpec((1, ADA_TN), lambda j: (0, j))],
        out_specs=pl.BlockSpec((8, ADA_TN), lambda j: (0, j)),
        out_shape=jax.ShapeDtypeStruct((8, n), F32),
        compiler_params=_params("arbitrary"),
        name="ada",
    )(c_pad, w_ada, b_ada)


IN_TM = 1024
IN_TN = 1024


def _modulated_norm(x, gain, scale, shift):
    y = x * lax.rsqrt(jnp.mean(x * x, axis=-1, keepdims=True) + NORM_EPS)
    return y * gain * (1.0 + scale) + shift


def _norm1_kernel(x_ref, mod_ref, gain_ref, u_ref):
    mod = mod_ref[0]
    u_ref[...] = _modulated_norm(x_ref[...], gain_ref[...], mod[1:2], mod[0:1]).astype(BF16)


def _norm1(x2, mod3, gain):
    blocks_per_batch = SEQ // IN_TM
    return pl.pallas_call(
        _norm1_kernel,
        grid=(TOKENS // IN_TM,),
        in_specs=[pl.BlockSpec((IN_TM, D_MODEL), lambda i: (i, 0)),
                  pl.BlockSpec((1, 6, D_MODEL), lambda i: (i // blocks_per_batch, 0, 0)),
                  pl.BlockSpec((1, D_MODEL), lambda i: (0, 0))],
        out_specs=pl.BlockSpec((IN_TM, D_MODEL), lambda i: (i, 0)),
        out_shape=jax.ShapeDtypeStruct((TOKENS, D_MODEL), BF16),
        compiler_params=_params("arbitrary"),
        name="norm1",
    )(x2, mod3, gain)


IN_LORA_BLOCK = HEAD_COLS // IN_TN
IN_RWKV_BLOCK0 = RET_COLS // IN_TN
GATE_ROW0 = RET_COLS + SHIFT_COLS


def _inproj_kernel(u_ref, w_ref, wl_ref, mu_ref, o_ref, wb_ref, carry_ref):
    j = pl.program_id(0)
    i = pl.program_id(1)

    @pl.when((i == 0) & (j != IN_LORA_BLOCK))
    def _():
        wb_ref[...] = w_ref[...].astype(BF16)

    @pl.when((i == 0) & (j == IN_LORA_BLOCK))
    def _():
        wb_ref[...] = wl_ref[...].astype(BF16)

    z = _bdot_nt(u_ref[...], wb_ref[...])
    shifted = (j >= IN_RWKV_BLOCK0) & (j <= IN_LORA_BLOCK)

    @pl.when(jnp.logical_not(shifted))
    def _():
        o_ref[...] = z

    @pl.when(shifted)
    def _():
        @pl.when(i % (SEQ // IN_TM) == 0)
        def _():
            carry_ref[...] = jnp.zeros_like(carry_ref)

        mu = mu_ref[...]
        prev = pltpu.roll(z, 1, 0)
        o_ref[...] = z + (prev - z) * mu
        top = z[:SUBLANES]
        row = lax.broadcasted_iota(jnp.int32, top.shape, 0)
        prev_top = jnp.where(row == 0, carry_ref[...], prev[:SUBLANES])
        o_ref[:SUBLANES, :] = top + (prev_top - top) * mu
        carry_ref[...] = z[IN_TM - 1:IN_TM, :]


def _inproj(u, w_t, w_lora, mu_p):
    assert HEAD_COLS % IN_TN == 0 and 2 * SEC - HEAD_COLS == IN_TN

    def w_row(j, i):
        tn, g0 = IN_TN // SUBLANES, GATE_ROW0 // SUBLANES
        head = jnp.minimum(j, IN_LORA_BLOCK - 1) * tn
        return (jnp.where(j > IN_LORA_BLOCK, g0 + (j - IN_LORA_BLOCK - 1) * tn, head) * SUBLANES, 0)

    return pl.pallas_call(
        _inproj_kernel,
        grid=(Z_COLS // IN_TN, TOKENS // IN_TM),
        in_specs=[pl.BlockSpec((IN_TM, D_MODEL), lambda j, i: (i, 0)),
                  pl.BlockSpec((pl.Element(IN_TN), pl.Element(D_MODEL)), w_row),
                  pl.BlockSpec((IN_TN, D_MODEL), lambda j, i: (0, 0), pipeline_mode=pl.Buffered(1)),
                  pl.BlockSpec((1, IN_TN), lambda j, i: (0, jnp.clip(j - IN_RWKV_BLOCK0, 0, SEC // IN_TN - 1)))],
        out_specs=pl.BlockSpec((IN_TM, IN_TN), lambda j, i: (i, j)),
        out_shape=jax.ShapeDtypeStruct((TOKENS, Z_COLS), F32),
        scratch_shapes=[pltpu.VMEM((IN_TN, D_MODEL), BF16), pltpu.VMEM((1, IN_TN), F32)],
        compiler_params=_params("arbitrary", "arbitrary"),
        name="inproj",
    )(u, w_t, w_lora, mu_p)


def _rows(ref, cols):
    t = ref[:, :, cols]
    return t.reshape(t.shape[0] * t.shape[1], t.shape[2])


def _ret_stages(pos_ref, z_ref, invf_ref, dintra_ref, zeta_ref, xi_ref, cdec_ref, o_ref, state_ref):
    c = RET_CHUNK
    ang = _rows(pos_ref, slice(None)).astype(F32) * invf_ref[...]
    cos = jnp.cos(ang)
    sin = jnp.sin(ang)
    half = RET_DIM // 2
    yield

    def rot(t):
        t1, t2 = t[:, :half], t[:, half:]
        return jnp.concatenate([t1 * cos - t2 * sin, t1 * sin + t2 * cos], axis=-1)

    heads = range(RET_HEADS)
    units = [(b, h) for b in range(BATCH) for h in heads]
    cols = lambda section, h: slice(section * RET_WIDTH + h * RET_DIM, section * RET_WIDTH + (h + 1) * RET_DIM)
    seq = lambda t, b: t[b * c:(b + 1) * c]
    q_all = [rot(_rows(z_ref, cols(0, h))).astype(BF16) for h in heads]
    yield
    k_all = [rot(_rows(z_ref, cols(1, h))) * (RET_DIM ** -0.5) for h in heads]
    v_all = [_rows(z_ref, cols(2, h)).astype(BF16) for h in heads]
    yield
    qb = [seq(q_all[h], b) for b, h in units]
    k = [seq(k_all[h], b) for b, h in units]
    v = [seq(v_all[h], b) for b, h in units]
    scores = [_bdot_nt(qb[u], k[u]) * dintra_ref[h] for u, (b, h) in enumerate(units)]
    state = [state_ref[u] for u in range(len(units))]
    yield
    out = [_bdot(scores[u], v[u]) + _bdot(qb[u], state[u]) * xi_ref[h] for u, (b, h) in enumerate(units)]
    for u, (b, h) in enumerate(units):
        state_ref[u] = state[u] * cdec_ref[h] + _bdot_tn(k[u] * zeta_ref[h], v[u])
    yield
    for u, (b, h) in enumerate(units):
        mu = jnp.mean(out[u], axis=-1, keepdims=True)
        d = out[u] - mu
        var = jnp.mean(d * d, axis=-1, keepdims=True)
        g = z_ref[b, :, cols(3, h)]
        o_ref[b, :, cols(0, h)] = (_silu(g) * (d * lax.rsqrt(var + GN_EPS_RET))).astype(BF16)
        if u % 2:
            yield


def _retention_tables():
    h = RET_HEADS
    half = RET_DIM // 2
    inv_freq = ROPE_BASE ** (-jnp.arange(half, dtype=F32) / half)
    log_gamma = jnp.log(1.0 - 2.0 ** (-5.0 - jnp.arange(h, dtype=F32)))
    idx = jnp.arange(RET_CHUNK, dtype=F32)
    dist = idx[:, None] - idx[None, :]
    decay_intra = jnp.where(dist >= 0, jnp.exp(log_gamma[:, None, None] * jnp.maximum(dist, 0.0)), 0.0)
    zeta = jnp.exp(log_gamma[:, None] * (RET_CHUNK - 1.0 - idx))
    xi = jnp.exp(log_gamma[:, None] * (idx + 1.0))
    chunk_decay = jnp.exp(log_gamma * RET_CHUNK)
    wide = lambda t: jnp.broadcast_to(t[:, :, None], (h, RET_CHUNK, RET_DIM))
    cdec = jnp.broadcast_to(chunk_decay[:, None, None], (h, 1, RET_DIM))
    return inv_freq[None, :], decay_intra, wide(zeta), wide(xi), cdec


def _split2(x):
    hi = x.astype(BF16)
    return hi, (x - hi.astype(F32)).astype(BF16)


def _head_sums(xs, ones_ref):
    c = xs[0].shape[0]
    pieces = []
    for x in xs:
        for p in _split2(x):
            for g in range(RWKV_GROUPS):
                pieces.append(p[:, g * GROUP_LANES:(g + 1) * GROUP_LANES])
    res = jnp.dot(jnp.concatenate(pieces, axis=0), ones_ref[...], preferred_element_type=F32)
    outs = []
    for i in range(len(xs)):
        base = i * 2 * RWKV_GROUPS * c
        cols = []
        for g in range(RWKV_GROUPS):
            hi = res[base + g * c:base + (g + 1) * c]
            lo = res[base + (RWKV_GROUPS + g) * c:base + (RWKV_GROUPS + g + 1) * c]
            cols.append(hi + lo)
        outs.append(jnp.concatenate(cols, axis=-1))
    return outs


def _rwkv_stages(zs_ref, par_ref, wd_ref, wa_ref, wg_ref, ltri_ref, ones_ref, mstrict_ref, mincl_ref,
                 o_ref, s_ref):
    c = RWKV_CHUNK
    w = RWKV_WIDTH
    seq = lambda t, b: t[b * c:(b + 1) * c]

    r, kw, vw = _rows(zs_ref, slice(0, w)), _rows(zs_ref, slice(w, 2 * w)), _rows(zs_ref, slice(2 * w, 3 * w))
    zw = _rows(zs_ref, slice(3 * w, 3 * w + LORA_PAD))
    za = _rows(zs_ref, slice(3 * w + LORA_PAD, 3 * w + 2 * LORA_PAD))
    zg = _rows(zs_ref, slice(3 * w + 2 * LORA_PAD, 3 * w + 2 * LORA_PAD + GATE_LORA))
    par = par_ref[...]
    w0, a0, k_k, k_a, r_k, lnx_w, lnx_b = (par[i:i + 1] for i in range(7))

    dec_pre = w0 + _bdot(jnp.tanh(zw), wd_ref[...])
    softplus = jnp.maximum(-dec_pre, 0.0) + jnp.log1p(jnp.exp(-jnp.abs(dec_pre)))
    logdec = -jnp.exp(-softplus - 0.5)
    iclr = jax.nn.sigmoid(a0 + _bdot(za, wa_ref[...]))

    kk = kw * k_k
    k_mod = kw * (1.0 + (iclr - 1.0) * k_a)
    (kk_sq,) = _head_sums([kk * kk], ones_ref)
    kk = kk * (1.0 / jnp.maximum(jnp.sqrt(kk_sq), 1e-12))
    a = -kk
    b = kk * iclr

    ld_hi = logdec.astype(BF16)
    ld_mid, ld_lo = _split2(logdec - ld_hi.astype(F32))
    ltri = ltri_ref[...]
    cum = (jnp.dot(ltri, ld_hi, preferred_element_type=F32) + jnp.dot(ltri, ld_mid, preferred_element_type=F32)
           + jnp.dot(ltri, ld_lo, preferred_element_type=F32))
    ends = [cum[(b + 1) * c - 1:(b + 1) * c, :] for b in range(BATCH)]
    cum_end = jnp.concatenate([jnp.broadcast_to(e, (c, w)) for e in ends], axis=0)
    p_inv = jnp.exp(-cum)
    a_t = a * jnp.exp(cum - logdec)
    r_t = r * jnp.exp(cum)
    b_t = (b * p_inv).astype(BF16)
    k_t = (k_mod * p_inv).astype(BF16)

    lane_head = lax.shift_right_logical(lax.broadcasted_iota(jnp.int32, (c, GROUP_LANES), 1), 6)
    head_masks = [lane_head == h for h in range(GROUP_HEADS)]

    def stack(x):
        return jnp.concatenate([jnp.where(m, x, 0.0).astype(BF16) for m in head_masks], axis=0)

    def rep(x):
        return jnp.concatenate([x] * GROUP_HEADS, axis=0)

    m_strict = mstrict_ref[...]
    m_incl = mincl_ref[...]
    eye = (lax.broadcasted_iota(jnp.int32, (GROUP_LANES, GROUP_LANES), 0)
           == lax.broadcasted_iota(jnp.int32, (GROUP_LANES, GROUP_LANES), 1)).astype(F32)
    n4 = GROUP_HEADS * c
    lane_groups = [slice(g * GROUP_LANES, (g + 1) * GROUP_LANES) for g in range(RWKV_GROUPS)]
    units = [(b, sl) for b in range(BATCH) for sl in lane_groups]
    groups = range(len(units))
    unit = lambda t, u: seq(t, units[u][0])[:, units[u][1]]
    a_s = [stack(unit(a_t, u)) for u in groups]
    r_s = [stack(unit(r_t, u)) for u in groups]
    blk = [_bdot_nt(jnp.concatenate([a_s[g], r_s[g]], axis=0),
                    jnp.concatenate([rep(unit(b_t, g)), rep(unit(k_t, g))], axis=0)) for g in groups]
    a_ab = [blk[g][:n4, :n4] * m_strict for g in groups]
    a_ak = [blk[g][:n4, n4:] * m_strict for g in groups]
    a_rb = [blk[g][n4:, :n4] * m_incl for g in groups]
    a_rk = [blk[g][n4:, n4:] * m_incl for g in groups]

    t_inv = [eye + a_ab[g] for g in groups]
    a_pow = a_ab

    def inverse_round():
        nonlocal a_pow, t_inv
        a_pow = [_bdot(a_pow[g], a_pow[g]) for g in groups]
        t_inv = [t_inv[g] + _bdot(t_inv[g], a_pow[g]) for g in groups]

    yield
    inverse_round()
    g_rw = _bdot(jax.nn.sigmoid(zg), wg_ref[...])
    (rk_sum,) = _head_sums([r * k_mod * r_k], ones_ref)
    bonus = rk_sum * vw
    yield
    inverse_round()
    p_end = jnp.exp(cum_end - cum)
    b_e = b * p_end
    k_e = k_mod * p_end
    decay_end = [jnp.exp(e) for e in ends]
    yield
    inverse_round()
    v_s = [stack(unit(vw, u)) for u in groups]
    be_s = [stack(unit(b_e, u)) for u in groups]
    yield
    inverse_round()
    ke_s = [stack(unit(k_e, u)) for u in groups]
    yield
    inverse_round()
    yield

    a_eff = [_bdot(t_inv[g], a_s[g]) for g in groups]
    av = [_bdot(a_ak[g], v_s[g]) for g in groups]
    u_const = [_bdot(t_inv[g], av[g]) for g in groups]
    yield
    state = [s_ref[g] for g in groups]
    sb = [state[g].astype(BF16) for g in groups]
    u = [_bdot_nt(a_eff[g], sb[g]) + u_const[g] for g in groups]
    y_st = [_bdot_nt(r_s[g], sb[g]) + _bdot(a_rb[g], u[g]) + _bdot(a_rk[g], v_s[g]) for g in groups]
    for g, (b_idx, sl) in enumerate(units):
        s_ref[g] = (state[g] * decay_end[b_idx][:, sl]
                    + _bdot_tn(jnp.concatenate([u[g].astype(BF16), v_s[g]], axis=0),
                               jnp.concatenate([be_s[g], ke_s[g]], axis=0)))
    heads_summed = [t[:c] + t[c:2 * c] + t[2 * c:3 * c] + t[3 * c:] for t in y_st]
    y = jnp.concatenate([jnp.concatenate(heads_summed[b * RWKV_GROUPS:(b + 1) * RWKV_GROUPS], axis=-1)
                         for b in range(BATCH)], axis=0)

    (y_sum,) = _head_sums([y], ones_ref)
    d = y - y_sum * (1.0 / RWKV_HEAD)
    (d_sq,) = _head_sums([d * d], ones_ref)
    y = d * lax.rsqrt(d_sq * (1.0 / RWKV_HEAD) + GN_EPS_RWKV) * lnx_w + lnx_b + bonus
    o_ref[...] = (y * g_rw).astype(BF16).reshape(o_ref.shape)


def _emit_interleaved(primary, secondary, pieces_after):
    for n in pieces_after:
        next(primary, None)
        for _ in range(n):
            next(secondary, None)
    for _ in primary:
        pass
    for _ in secondary:
        pass


def _mixer_kernel(pos_ref, zr_ref, zs_ref, invf_ref, dintra_ref, zeta_ref, xi_ref, cdec_ref, par_ref, wd_ref, wa_ref,
                  wg_ref, ltri_ref, ones_ref, mstrict_ref, mincl_ref, or_ref, ow_ref, rs_ref, ws_ref):
    @pl.when(pl.program_id(0) == 0)
    def _():
        rs_ref[...] = jnp.zeros_like(rs_ref)
        ws_ref[...] = jnp.zeros_like(ws_ref)

    _emit_interleaved(
        _rwkv_stages(zs_ref, par_ref, wd_ref, wa_ref, wg_ref, ltri_ref, ones_ref, mstrict_ref, mincl_ref, ow_ref, ws_ref),
        _ret_stages(pos_ref, zr_ref, invf_ref, dintra_ref, zeta_ref, xi_ref, cdec_ref, or_ref, rs_ref),
        pieces_after=(1, 2, 1, 1, 2, 1))


def _rwkv_tables():
    c = RWKV_CHUNK
    tok = jnp.arange(BATCH * c)
    ltri = ((tok[:, None] >= tok[None, :]) & (tok[:, None] // c == tok[None, :] // c)).astype(BF16)
    row = jnp.arange(GROUP_LANES)[:, None]
    col = jnp.arange(GROUP_LANES)[None, :]
    same_head = (row // RWKV_HEAD) == (col // RWKV_HEAD)
    ones = same_head.astype(BF16)
    m_strict = (same_head & (row > col)).astype(F32)
    m_incl = (same_head & (row >= col)).astype(F32)
    return ltri, ones, m_strict, m_incl


def _mixers(pos3, z3, par, wd, wa, wg):
    assert RWKV_CHUNK == RWKV_HEAD and RET_CHUNK == RWKV_CHUNK
    c = RWKV_CHUNK
    ret_tables = _retention_tables()
    rwkv_tables = _rwkv_tables()
    full = lambda a: pl.BlockSpec(a.shape, lambda n: (0,) * a.ndim)
    rows = lambda width, col: pl.BlockSpec((BATCH, c, width), lambda n: (0, n, col))
    return pl.pallas_call(
        _mixer_kernel,
        grid=(SEQ // c,),
        in_specs=[rows(1, 0), rows(SEC, 0), rows(SEC, 1), *map(full, ret_tables),
                  full(par), full(wd), full(wa), full(wg), *map(full, rwkv_tables)],
        out_specs=[rows(RET_WIDTH, 0), rows(RWKV_WIDTH, 0)],
        out_shape=[jax.ShapeDtypeStruct((BATCH, SEQ, RET_WIDTH), BF16),
                   jax.ShapeDtypeStruct((BATCH, SEQ, RWKV_WIDTH), BF16)],
        scratch_shapes=[pltpu.VMEM((BATCH * RET_HEADS, RET_DIM, RET_DIM), F32),
                        pltpu.VMEM((BATCH * RWKV_GROUPS, GROUP_LANES, GROUP_LANES), F32)],
        compiler_params=_params("arbitrary"),
        name="mixers",
    )(pos3, z3, z3, *ret_tables, par, wd, wa, wg, *rwkv_tables)


MG_TM = 1024
MG_TN = 1024


def _merge_kernel(yr_ref, yw_ref, wr_ref, ww_ref, gr_ref, gw_ref, br_ref, bw_ref, o_ref, wrb_ref, wwb_ref):
    @pl.when(pl.program_id(1) == 0)
    def _():
        wrb_ref[...] = wr_ref[...].astype(BF16)
        wwb_ref[...] = ww_ref[...].astype(BF16)

    pr = jnp.dot(yr_ref[...], wrb_ref[...], preferred_element_type=F32)
    pw = jnp.dot(yw_ref[...], wwb_ref[...], preferred_element_type=F32)
    o = jax.nn.sigmoid(gr_ref[...] + br_ref[...]) * pr + jax.nn.sigmoid(gw_ref[...] + bw_ref[...]) * pw
    o_ref[...] = o.astype(BF16)


def _merge(y_ret, y_rw, w_ret, w_rw, z, b_gate):
    g0 = 2 * SEC // MG_TN
    nd = D_MODEL // MG_TN
    return pl.pallas_call(
        _merge_kernel,
        grid=(nd, TOKENS // MG_TM),
        in_specs=[pl.BlockSpec((MG_TM, RET_WIDTH), lambda j, i: (i, 0)),
                  pl.BlockSpec((MG_TM, RWKV_WIDTH), lambda j, i: (i, 0)),
                  pl.BlockSpec((RET_WIDTH, MG_TN), lambda j, i: (0, j)),
                  pl.BlockSpec((RWKV_WIDTH, MG_TN), lambda j, i: (0, j)),
                  pl.BlockSpec((MG_TM, MG_TN), lambda j, i: (i, g0 + j)),
                  pl.BlockSpec((MG_TM, MG_TN), lambda j, i: (i, g0 + nd + j)),
                  pl.BlockSpec((1, MG_TN), lambda j, i: (0, j)),
                  pl.BlockSpec((1, MG_TN), lambda j, i: (0, nd + j))],
        out_specs=pl.BlockSpec((MG_TM, MG_TN), lambda j, i: (i, j)),
        out_shape=jax.ShapeDtypeStruct((TOKENS, D_MODEL), BF16),
        scratch_shapes=[pltpu.VMEM((RET_WIDTH, MG_TN), BF16), pltpu.VMEM((RWKV_WIDTH, MG_TN), BF16)],
        compiler_params=_params("arbitrary", "arbitrary"),
        name="merge",
    )(y_ret, y_rw, w_ret, w_rw, z, z, b_gate, b_gate)


OP_TM = 512


def _oproj_kernel(m_ref, w_ref, x_ref, mod_ref, gain_ref, h_ref, u_ref, wb_ref):
    @pl.when(pl.program_id(0) == 0)
    def _():
        wb_ref[...] = w_ref[...].astype(BF16)

    mod = mod_ref[0]
    h = x_ref[...] + mod[2:3] * jnp.dot(m_ref[...], wb_ref[...], preferred_element_type=F32)
    h_ref[...] = h
    u_ref[...] = _modulated_norm(h, gain_ref[...], mod[4:5], mod[3:4]).astype(BF16)


def _oproj(merged, w_o, x2, mod3, gain):
    blocks_per_batch = SEQ // OP_TM
    return pl.pallas_call(
        _oproj_kernel,
        grid=(TOKENS // OP_TM,),
        in_specs=[pl.BlockSpec((OP_TM, D_MODEL), lambda i: (i, 0)),
                  pl.BlockSpec((D_MODEL, D_MODEL), lambda i: (0, 0), pipeline_mode=pl.Buffered(1)),
                  pl.BlockSpec((OP_TM, D_MODEL), lambda i: (i, 0)),
                  pl.BlockSpec((1, 6, D_MODEL), lambda i: (i // blocks_per_batch, 0, 0)),
                  pl.BlockSpec((1, D_MODEL), lambda i: (0, 0))],
        out_specs=[pl.BlockSpec((OP_TM, D_MODEL), lambda i: (i, 0)),
                   pl.BlockSpec((OP_TM, D_MODEL), lambda i: (i, 0))],
        out_shape=[jax.ShapeDtypeStruct((TOKENS, D_MODEL), F32),
                   jax.ShapeDtypeStruct((TOKENS, D_MODEL), BF16)],
        scratch_shapes=[pltpu.VMEM((D_MODEL, D_MODEL), BF16)],
        compiler_params=_params("arbitrary"),
        name="oproj",
    )(merged, w_o, x2, mod3, gain)


FI_TM = 1024
FI_TN = 512


def _ffn_in_kernel(u_ref, wg_ref, wu_ref, o_ref, wgb_ref, wub_ref):
    @pl.when(pl.program_id(1) == 0)
    def _():
        wgb_ref[...] = wg_ref[...].astype(BF16)
        wub_ref[...] = wu_ref[...].astype(BF16)

    u = u_ref[...]
    gate = jnp.dot(u, wgb_ref[...], preferred_element_type=F32)
    up = jnp.dot(u, wub_ref[...], preferred_element_type=F32)
    o_ref[...] = (_silu(gate) * up).astype(BF16)


def _ffn_in(u2, w_in):
    nt = FFN_HIDDEN // FI_TN
    return pl.pallas_call(
        _ffn_in_kernel,
        grid=(nt, TOKENS // FI_TM),
        in_specs=[pl.BlockSpec((FI_TM, D_MODEL), lambda j, i: (i, 0)),
                  pl.BlockSpec((D_MODEL, FI_TN), lambda j, i: (0, j)),
                  pl.BlockSpec((D_MODEL, FI_TN), lambda j, i: (0, nt + j))],
        out_specs=pl.BlockSpec((FI_TM, FI_TN), lambda j, i: (i, j)),
        out_shape=jax.ShapeDtypeStruct((TOKENS, FFN_HIDDEN), BF16),
        scratch_shapes=[pltpu.VMEM((D_MODEL, FI_TN), BF16), pltpu.VMEM((D_MODEL, FI_TN), BF16)],
        compiler_params=_params("arbitrary", "arbitrary"),
        name="ffn_in",
    )(u2, w_in, w_in)


FO_TM = 1024
FO_TK = 512


def _ffn_out_kernel(a_ref, w_ref, h_ref, mod_ref, gain_ref, o_ref, acc_ref):
    k = pl.program_id(1)

    @pl.when(k == 0)
    def _():
        acc_ref[...] = jnp.dot(a_ref[...], w_ref[...].astype(BF16), preferred_element_type=F32)

    @pl.when(k > 0)
    def _():
        acc_ref[...] += jnp.dot(a_ref[...], w_ref[...].astype(BF16), preferred_element_type=F32)

    @pl.when(k == pl.num_programs(1) - 1)
    def _():
        h = h_ref[...] + mod_ref[0][5:6] * acc_ref[...]
        o_ref[...] = h * lax.rsqrt(jnp.mean(h * h, axis=-1, keepdims=True) + NORM_EPS) * gain_ref[...]


def _ffn_out(act, w_out, h1, mod3, gain):
    blocks_per_batch = SEQ // FO_TM
    return pl.pallas_call(
        _ffn_out_kernel,
        grid=(TOKENS // FO_TM, FFN_HIDDEN // FO_TK),
        in_specs=[pl.BlockSpec((FO_TM, FO_TK), lambda i, k: (i, k)),
                  pl.BlockSpec((FO_TK, D_MODEL), lambda i, k: (k, 0)),
                  pl.BlockSpec((FO_TM, D_MODEL), lambda i, k: (i, 0)),
                  pl.BlockSpec((1, 6, D_MODEL), lambda i, k: (i // blocks_per_batch, 0, 0)),
                  pl.BlockSpec((1, D_MODEL), lambda i, k: (0, 0))],
        out_specs=pl.BlockSpec((FO_TM, D_MODEL), lambda i, k: (i, 0)),
        out_shape=jax.ShapeDtypeStruct((TOKENS, D_MODEL), F32),
        scratch_shapes=[pltpu.VMEM((FO_TM, D_MODEL), F32)],
        compiler_params=_params("arbitrary", "arbitrary", vmem=VMEM_LIMIT_MAX),
        name="ffn_out",
    )(act, w_out, h1, mod3, gain)


def _pad_cols(t, width):
    return jnp.pad(t, ((0, 0), (0, width - t.shape[1])))


def _pad_rows(t, rows):
    return jnp.pad(t, ((0, rows - t.shape[0]), (0, 0)))


def _pack_lora_rows(w_t):
    o3 = HEAD_COLS
    o4 = o3 + DECAY_LORA
    o5 = o4 + ICLR_LORA
    lora = jnp.concatenate([_pad_rows(w_t[o3:o4], LORA_PAD), _pad_rows(w_t[o4:o5], LORA_PAD), w_t[o5:GATE_ROW0]],
                           axis=0)
    return _pad_rows(lora, 2 * SEC - HEAD_COLS)


def _pack_mu(mu):
    o3 = 3 * RWKV_WIDTH
    o4 = o3 + DECAY_LORA
    o5 = o4 + ICLR_LORA
    mu = mu[None, :]
    rw = jnp.concatenate([mu[:, :o3], _pad_cols(mu[:, o3:o4], LORA_PAD), _pad_cols(mu[:, o4:o5], LORA_PAD),
                          mu[:, o5:]], axis=1)
    return _pad_cols(rw, SEC)


def kernel(x, c, positions, w_ada, b_ada, norm_mix, norm_ffn, norm_final, w_in, b_gate, mu_shift, w0, w_decay_up, a0, w_iclr_up, w_gate_up, k_k, k_a, r_k, lnx_w, lnx_b, w_ret_out, w_rwkv_out, w_o, w_ffn_in, w_ffn_out):
    assert x.shape == (BATCH, SEQ, D_MODEL) and w_ada.shape[0] == 1
    x2 = x.reshape(TOKENS, D_MODEL)

    mod = _ada(_pad_rows(c, 8), w_ada[0], b_ada)
    mod3 = mod[:BATCH].reshape(BATCH, 6, D_MODEL)

    w_t = jnp.swapaxes(w_in, 1, 2)[0]
    u1 = _norm1(x2, mod3, norm_mix)
    z = _inproj(u1, w_t, _pack_lora_rows(w_t), _pack_mu(mu_shift[0]))

    par = jnp.concatenate([w0, a0, k_k, k_a, r_k.reshape(1, RWKV_WIDTH), lnx_w, lnx_b,
                           jnp.zeros((1, RWKV_WIDTH), F32)], axis=0)
    y_ret, y_rw = _mixers(positions.reshape(BATCH, SEQ, 1), z.reshape(BATCH, SEQ, Z_COLS), par,
                          _pad_rows(w_decay_up[0], LORA_PAD).astype(BF16),
                          _pad_rows(w_iclr_up[0], LORA_PAD).astype(BF16), w_gate_up[0].astype(BF16))
    y_ret = y_ret.reshape(TOKENS, RET_WIDTH)
    y_rw = y_rw.reshape(TOKENS, RWKV_WIDTH)

    merged = _merge(y_ret, y_rw, w_ret_out[0], w_rwkv_out[0], z, b_gate)
    h1, u2 = _oproj(merged, w_o[0], x2, mod3, norm_ffn)
    act = _ffn_in(u2, w_ffn_in[0])
    out = _ffn_out(act, w_ffn_out[0], h1, mod3, norm_final[None, :])
    return out.reshape(BATCH, SEQ, D_MODEL)
```

```python
import jax
import jax.numpy as jnp
from jax import lax
from jax.experimental import pallas as pl
from jax.experimental.pallas import tpu as pltpu

F32 = jnp.float32
BF16 = jnp.bfloat16

D_MODEL = 2048
BATCH = 2
SEQ = 4096
TOKENS = BATCH * SEQ

RET_HEADS = 4
RET_DIM = 256
RET_WIDTH = RET_HEADS * RET_DIM
RET_CHUNK = 128
ROPE_BASE = 10000.0

RWKV_HEAD = 64
RWKV_WIDTH = 1024
RWKV_HEADS = RWKV_WIDTH // RWKV_HEAD
DECAY_LORA = 96
ICLR_LORA = 96
GATE_LORA = 256
RWKV_CHUNK = 64
STEP_CHUNKS = RET_CHUNK // RWKV_CHUNK
GROUP_LANES = 256
GROUP_HEADS = GROUP_LANES // RWKV_HEAD
RWKV_GROUPS = RWKV_WIDTH // GROUP_LANES
LORA_PAD = 128

FFN_HIDDEN = ((8 * D_MODEL // 3 + 255) // 256) * 256
RET_COLS = 4 * RET_WIDTH
SHIFT_COLS = 3 * RWKV_WIDTH + DECAY_LORA + ICLR_LORA + GATE_LORA
GATE_COLS = 2 * D_MODEL
SEC = 4096
Z_COLS = 3 * SEC
HEAD_COLS = RET_COLS + 3 * RWKV_WIDTH
NORM_EPS = 1e-6
GN_EPS_RET = 1e-5
GN_EPS_RWKV = 64e-5

VMEM_LIMIT = 56 * 1024 * 1024
VMEM_LIMIT_MAX = 60 * 1024 * 1024
SUBLANES = 8


def _params(*sem, vmem=VMEM_LIMIT):
    return pltpu.CompilerParams(dimension_semantics=sem, vmem_limit_bytes=vmem)


def _bdot(a, b):
    return jnp.dot(a.astype(BF16), b.astype(BF16), preferred_element_type=F32)


def _bdot_nt(a, b):
    return lax.dot_general(a.astype(BF16), b.astype(BF16), (((1,), (1,)), ((), ())),
                           preferred_element_type=F32)


def _bdot_tn(a, b):
    return lax.dot_general(a.astype(BF16), b.astype(BF16), (((0,), (0,)), ((), ())),
                           preferred_element_type=F32)


def _silu(x):
    return x * jax.nn.sigmoid(x)


ADA_TN = 1024


def _ada_kernel(c_ref, w_ref, b_ref, o_ref):
    o_ref[...] = _bdot(_silu(c_ref[...]), w_ref[...]) + b_ref[...]


def _ada(c_pad, w_ada, b_ada):
    n = w_ada.shape[1]
    return pl.pallas_call(
        _ada_kernel,
        grid=(n // ADA_TN,),
        in_specs=[pl.BlockSpec((8, D_MODEL), lambda j: (0, 0)),
                  pl.BlockSpec((D_MODEL, ADA_TN), lambda j: (0, j)),
                  pl.BlockSpec((1, ADA_TN), lambda j: (0, j))],
        out_specs=pl.BlockSpec((8, ADA_TN), lambda j: (0, j)),
        out_shape=jax.ShapeDtypeStruct((8, n), F32),
        compiler_params=_params("arbitrary"),
        name="ada",
    )(c_pad, w_ada, b_ada)


IN_TM = 1024
IN_TN = 1024


def _modulated_norm(x, gain, scale, shift):
    y = x * lax.rsqrt(jnp.mean(x * x, axis=-1, keepdims=True) + NORM_EPS)
    return y * gain * (1.0 + scale) + shift


def _norm1_kernel(x_ref, mod_ref, gain_ref, u_ref):
    mod = mod_ref[0]
    u_ref[...] = _modulated_norm(x_ref[...], gain_ref[...], mod[1:2], mod[0:1]).astype(BF16)


def _norm1(x2, mod3, gain):
    blocks_per_batch = SEQ // IN_TM
    return pl.pallas_call(
        _norm1_kernel,
        grid=(TOKENS // IN_TM,),
        in_specs=[pl.BlockSpec((IN_TM, D_MODEL), lambda i: (i, 0)),
                  pl.BlockSpec((1, 6, D_MODEL), lambda i: (i // blocks_per_batch, 0, 0)),
                  pl.BlockSpec((1, D_MODEL), lambda i: (0, 0))],
        out_specs=pl.BlockSpec((IN_TM, D_MODEL), lambda i: (i, 0)),
        out_shape=jax.ShapeDtypeStruct((TOKENS, D_MODEL), BF16),
        compiler_params=_params("arbitrary"),
        name="norm1",
    )(x2, mod3, gain)


IN_LORA_BLOCK = HEAD_COLS // IN_TN
IN_RWKV_BLOCK0 = RET_COLS // IN_TN
GATE_ROW0 = RET_COLS + SHIFT_COLS


def _inproj_kernel(u_ref, w_ref, wl_ref, mu_ref, o_ref, wb_ref, carry_ref):
    j = pl.program_id(0)
    i = pl.program_id(1)

    @pl.when((i == 0) & (j != IN_LORA_BLOCK))
    def _():
        wb_ref[...] = w_ref[...].astype(BF16)

    @pl.when((i == 0) & (j == IN_LORA_BLOCK))
    def _():
        wb_ref[...] = wl_ref[...].astype(BF16)

    z = _bdot_nt(u_ref[...], wb_ref[...])
    shifted = (j >= IN_RWKV_BLOCK0) & (j <= IN_LORA_BLOCK)

    @pl.when(jnp.logical_not(shifted))
    def _():
        o_ref[...] = z

    @pl.when(shifted)
    def _():
        @pl.when(i % (SEQ // IN_TM) == 0)
        def _():
            carry_ref[...] = jnp.zeros_like(carry_ref)

        mu = mu_ref[...]
        prev = pltpu.roll(z, 1, 0)
        o_ref[...] = z + (prev - z) * mu
        top = z[:SUBLANES]
        row = lax.broadcasted_iota(jnp.int32, top.shape, 0)
        prev_top = jnp.where(row == 0, carry_ref[...], prev[:SUBLANES])
        o_ref[:SUBLANES, :] = top + (prev_top - top) * mu
        carry_ref[...] = z[IN_TM - 1:IN_TM, :]


def _inproj(u, w_t, w_lora, mu_p):
    assert HEAD_COLS % IN_TN == 0 and 2 * SEC - HEAD_COLS == IN_TN

    def w_row(j, i):
        tn, g0 = IN_TN // SUBLANES, GATE_ROW0 // SUBLANES
        head = jnp.minimum(j, IN_LORA_BLOCK - 1) * tn
        return (jnp.where(j > IN_LORA_BLOCK, g0 + (j - IN_LORA_BLOCK - 1) * tn, head) * SUBLANES, 0)

    return pl.pallas_call(
        _inproj_kernel,
        grid=(Z_COLS // IN_TN, TOKENS // IN_TM),
        in_specs=[pl.BlockSpec((IN_TM, D_MODEL), lambda j, i: (i, 0)),
                  pl.BlockSpec((pl.Element(IN_TN), pl.Element(D_MODEL)), w_row),
                  pl.BlockSpec((IN_TN, D_MODEL), lambda j, i: (0, 0), pipeline_mode=pl.Buffered(1)),
                  pl.BlockSpec((1, IN_TN), lambda j, i: (0, jnp.clip(j - IN_RWKV_BLOCK0, 0, SEC // IN_TN - 1)))],
        out_specs=pl.BlockSpec((IN_TM, IN_TN), lambda j, i: (i, j)),
        out_shape=jax.ShapeDtypeStruct((TOKENS, Z_COLS), F32),
        scratch_shapes=[pltpu.VMEM((IN_TN, D_MODEL), BF16), pltpu.VMEM((1, IN_TN), F32)],
        compiler_params=_params("arbitrary", "arbitrary"),
        name="inproj",
    )(u, w_t, w_lora, mu_p)


def _rows(ref, cols):
    t = ref[:, :, cols]
    return t.reshape(t.shape[0] * t.shape[1], t.shape[2])


def _ret_stages(pos_ref, z_ref, invf_ref, dintra_ref, zeta_ref, xi_ref, cdec_ref, o_ref, state_ref):
    c = RET_CHUNK
    ang = _rows(pos_ref, slice(None)).astype(F32) * invf_ref[...]
    cos = jnp.cos(ang)
    sin = jnp.sin(ang)
    half = RET_DIM // 2
    yield

    def rot(t):
        t1, t2 = t[:, :half], t[:, half:]
        return jnp.concatenate([t1 * cos - t2 * sin, t1 * sin + t2 * cos], axis=-1)

    heads = range(RET_HEADS)
    units = [(b, h) for b in range(BATCH) for h in heads]
    cols = lambda section, h: slice(section * RET_WIDTH + h * RET_DIM, section * RET_WIDTH + (h + 1) * RET_DIM)
    seq = lambda t, b: t[b * c:(b + 1) * c]
    q_all = [rot(_rows(z_ref, cols(0, h))).astype(BF16) for h in heads]
    yield
    k_all = [rot(_rows(z_ref, cols(1, h))) * (RET_DIM ** -0.5) for h in heads]
    v_all = [_rows(z_ref, cols(2, h)).astype(BF16) for h in heads]
    yield
    qb = [seq(q_all[h], b) for b, h in units]
    k = [seq(k_all[h], b) for b, h in units]
    v = [seq(v_all[h], b) for b, h in units]
    scores = [_bdot_nt(qb[u], k[u]) * dintra_ref[h] for u, (b, h) in enumerate(units)]
    state = [state_ref[u] for u in range(len(units))]
    yield
    out = [_bdot(scores[u], v[u]) + _bdot(qb[u], state[u]) * xi_ref[h] for u, (b, h) in enumerate(units)]
    for u, (b, h) in enumerate(units):
        state_ref[u] = state[u] * cdec_ref[h] + _bdot_tn(k[u] * zeta_ref[h], v[u])
    yield
    for u, (b, h) in enumerate(units):
        mu = jnp.mean(out[u], axis=-1, keepdims=True)
        d = out[u] - mu
        var = jnp.mean(d * d, axis=-1, keepdims=True)
        g = z_ref[b, :, cols(3, h)]
        o_ref[b, :, cols(0, h)] = (_silu(g) * (d * lax.rsqrt(var + GN_EPS_RET))).astype(BF16)
        if u % 2:
            yield


def _retention_tables():
    h = RET_HEADS
    half = RET_DIM // 2
    inv_freq = ROPE_BASE ** (-jnp.arange(half, dtype=F32) / half)
    log_gamma = jnp.log(1.0 - 2.0 ** (-5.0 - jnp.arange(h, dtype=F32)))
    idx = jnp.arange(RET_CHUNK, dtype=F32)
    dist = idx[:, None] - idx[None, :]
    decay_intra = jnp.where(dist >= 0, jnp.exp(log_gamma[:, None, None] * jnp.maximum(dist, 0.0)), 0.0)
    zeta = jnp.exp(log_gamma[:, None] * (RET_CHUNK - 1.0 - idx))
    xi = jnp.exp(log_gamma[:, None] * (idx + 1.0))
    chunk_decay = jnp.exp(log_gamma * RET_CHUNK)
    wide = lambda t: jnp.broadcast_to(t[:, :, None], (h, RET_CHUNK, RET_DIM))
    cdec = jnp.broadcast_to(chunk_decay[:, None, None], (h, 1, RET_DIM))
    return inv_freq[None, :], decay_intra, wide(zeta), wide(xi), cdec


def _split2(x):
    hi = x.astype(BF16)
    return hi, (x - hi.astype(F32)).astype(BF16)


def _head_sums(xs, ones_ref):
    c = xs[0].shape[0]
    pieces = []
    for x in xs:
        for p in _split2(x):
            for g in range(RWKV_GROUPS):
                pieces.append(p[:, g * GROUP_LANES:(g + 1) * GROUP_LANES])
    res = jnp.dot(jnp.concatenate(pieces, axis=0), ones_ref[...], preferred_element_type=F32)
    outs = []
    for i in range(len(xs)):
        base = i * 2 * RWKV_GROUPS * c
        cols = []
        for g in range(RWKV_GROUPS):
            hi = res[base + g * c:base + (g + 1) * c]
            lo = res[base + (RWKV_GROUPS + g) * c:base + (RWKV_GROUPS + g + 1) * c]
            cols.append(hi + lo)
        outs.append(jnp.concatenate(cols, axis=-1))
    return outs


def _rwkv_stages(zs_ref, par_ref, wd_ref, wa_ref, wg_ref, ltri_ref, ones_ref, mstrict_ref, mincl_ref,
                 o_ref, s_ref):
    c = RWKV_CHUNK
    w = RWKV_WIDTH
    n_blocks = BATCH * STEP_CHUNKS
    seq = lambda t, rb: t[rb * c:(rb + 1) * c]

    r, kw, vw = _rows(zs_ref, slice(0, w)), _rows(zs_ref, slice(w, 2 * w)), _rows(zs_ref, slice(2 * w, 3 * w))
    zw = _rows(zs_ref, slice(3 * w, 3 * w + LORA_PAD))
    za = _rows(zs_ref, slice(3 * w + LORA_PAD, 3 * w + 2 * LORA_PAD))
    zg = _rows(zs_ref, slice(3 * w + 2 * LORA_PAD, 3 * w + 2 * LORA_PAD + GATE_LORA))
    par = par_ref[...]
    w0, a0, k_k, k_a, r_k, lnx_w, lnx_b = (par[i:i + 1] for i in range(7))

    dec_pre = w0 + _bdot(jnp.tanh(zw), wd_ref[...])
    softplus = jnp.maximum(-dec_pre, 0.0) + jnp.log1p(jnp.exp(-jnp.abs(dec_pre)))
    logdec = -jnp.exp(-softplus - 0.5)
    iclr = jax.nn.sigmoid(a0 + _bdot(za, wa_ref[...]))

    kk = kw * k_k
    k_mod = kw * (1.0 + (iclr - 1.0) * k_a)
    (kk_sq,) = _head_sums([kk * kk], ones_ref)
    kk = kk * (1.0 / jnp.maximum(jnp.sqrt(kk_sq), 1e-12))
    a = -kk
    b = kk * iclr

    ld_hi = logdec.astype(BF16)
    ld_mid, ld_lo = _split2(logdec - ld_hi.astype(F32))
    ltri = ltri_ref[...]
    cum = (jnp.dot(ltri, ld_hi, preferred_element_type=F32) + jnp.dot(ltri, ld_mid, preferred_element_type=F32)
           + jnp.dot(ltri, ld_lo, preferred_element_type=F32))
    ends = [cum[(rb + 1) * c - 1:(rb + 1) * c, :] for rb in range(n_blocks)]
    cum_end = jnp.concatenate([jnp.broadcast_to(e, (c, w)) for e in ends], axis=0)
    p_inv = jnp.exp(-cum)
    a_t = a * jnp.exp(cum - logdec)
    r_t = r * jnp.exp(cum)
    b_t = (b * p_inv).astype(BF16)
    k_t = (k_mod * p_inv).astype(BF16)

    lane_head = lax.shift_right_logical(lax.broadcasted_iota(jnp.int32, (c, GROUP_LANES), 1), 6)
    head_masks = [lane_head == h for h in range(GROUP_HEADS)]

    def stack(x):
        return jnp.concatenate([jnp.where(m, x, 0.0).astype(BF16) for m in head_masks], axis=0)

    def rep(x):
        return jnp.concatenate([x] * GROUP_HEADS, axis=0)

    m_strict = mstrict_ref[...]
    m_incl = mincl_ref[...]
    eye = (lax.broadcasted_iota(jnp.int32, (GROUP_LANES, GROUP_LANES), 0)
           == lax.broadcasted_iota(jnp.int32, (GROUP_LANES, GROUP_LANES), 1)).astype(F32)
    n4 = GROUP_HEADS * c
    lane_groups = [slice(g * GROUP_LANES, (g + 1) * GROUP_LANES) for g in range(RWKV_GROUPS)]
    units = [(rb, sl) for rb in range(n_blocks) for sl in lane_groups]
    groups = range(len(units))
    unit = lambda t, u: seq(t, units[u][0])[:, units[u][1]]
    a_s = [stack(unit(a_t, u)) for u in groups]
    r_s = [stack(unit(r_t, u)) for u in groups]
    blk = [_bdot_nt(jnp.concatenate([a_s[g], r_s[g]], axis=0),
                    jnp.concatenate([rep(unit(b_t, g)), rep(unit(k_t, g))], axis=0)) for g in groups]
    a_ab = [blk[g][:n4, :n4] * m_strict for g in groups]
    a_ak = [blk[g][:n4, n4:] * m_strict for g in groups]
    a_rb = [blk[g][n4:, :n4] * m_incl for g in groups]
    a_rk = [blk[g][n4:, n4:] * m_incl for g in groups]

    t_inv = [eye + a_ab[g] for g in groups]
    a_pow = a_ab

    def inverse_round():
        nonlocal a_pow, t_inv
        a_pow = [_bdot(a_pow[g], a_pow[g]) for g in groups]
        t_inv = [t_inv[g] + _bdot(t_inv[g], a_pow[g]) for g in groups]

    yield
    inverse_round()
    g_rw = _bdot(jax.nn.sigmoid(zg), wg_ref[...])
    (rk_sum,) = _head_sums([r * k_mod * r_k], ones_ref)
    bonus = rk_sum * vw
    yield
    inverse_round()
    p_end = jnp.exp(cum_end - cum)
    b_e = b * p_end
    k_e = k_mod * p_end
    decay_end = [jnp.exp(e) for e in ends]
    yield
    inverse_round()
    v_s = [stack(unit(vw, u)) for u in groups]
    be_s = [stack(unit(b_e, u)) for u in groups]
    yield
    inverse_round()
    ke_s = [stack(unit(k_e, u)) for u in groups]
    yield
    inverse_round()
    yield

    a_eff = [_bdot(t_inv[g], a_s[g]) for g in groups]
    av = [_bdot(a_ak[g], v_s[g]) for g in groups]
    u_const = [_bdot(t_inv[g], av[g]) for g in groups]
    yield
    state, y_st = {}, {}
    for j in range(STEP_CHUNKS):
        now = [g for g in groups if units[g][0] % STEP_CHUNKS == j]
        slot = lambda g: (units[g][0] // STEP_CHUNKS) * RWKV_GROUPS + g % RWKV_GROUPS
        before = {g: s_ref[slot(g)] if j == 0 else state[g - RWKV_GROUPS] for g in now}
        sb = {g: before[g].astype(BF16) for g in now}
        u = {g: _bdot_nt(a_eff[g], sb[g]) + u_const[g] for g in now}
        for g in now:
            y_st[g] = _bdot_nt(r_s[g], sb[g]) + _bdot(a_rb[g], u[g]) + _bdot(a_rk[g], v_s[g])
        for g in now:
            rb, sl = units[g]
            state[g] = (before[g] * decay_end[rb][:, sl]
                        + _bdot_tn(jnp.concatenate([u[g].astype(BF16), v_s[g]], axis=0),
                                   jnp.concatenate([be_s[g], ke_s[g]], axis=0)))
            if j == STEP_CHUNKS - 1:
                s_ref[slot(g)] = state[g]
        yield
    heads_summed = [t[:c] + t[c:2 * c] + t[2 * c:3 * c] + t[3 * c:] for t in (y_st[g] for g in groups)]
    y = jnp.concatenate([jnp.concatenate(heads_summed[rb * RWKV_GROUPS:(rb + 1) * RWKV_GROUPS], axis=-1)
                         for rb in range(n_blocks)], axis=0)

    (y_sum,) = _head_sums([y], ones_ref)
    d = y - y_sum * (1.0 / RWKV_HEAD)
    (d_sq,) = _head_sums([d * d], ones_ref)
    y = d * lax.rsqrt(d_sq * (1.0 / RWKV_HEAD) + GN_EPS_RWKV) * lnx_w + lnx_b + bonus
    o_ref[...] = (y * g_rw).astype(BF16).reshape(o_ref.shape)


def _emit_interleaved(primary, secondary, pieces_after):
    for n in pieces_after:
        next(primary, None)
        for _ in range(n):
            next(secondary, None)
    for _ in primary:
        pass
    for _ in secondary:
        pass


def _mixer_kernel(pos_ref, zr_ref, zs_ref, invf_ref, dintra_ref, zeta_ref, xi_ref, cdec_ref, par_ref, wd_ref, wa_ref,
                  wg_ref, ltri_ref, ones_ref, mstrict_ref, mincl_ref, or_ref, ow_ref, rs_ref, ws_ref):
    @pl.when(pl.program_id(0) == 0)
    def _():
        rs_ref[...] = jnp.zeros_like(rs_ref)
        ws_ref[...] = jnp.zeros_like(ws_ref)

    _emit_interleaved(
        _rwkv_stages(zs_ref, par_ref, wd_ref, wa_ref, wg_ref, ltri_ref, ones_ref, mstrict_ref, mincl_ref, ow_ref, ws_ref),
        _ret_stages(pos_ref, zr_ref, invf_ref, dintra_ref, zeta_ref, xi_ref, cdec_ref, or_ref, rs_ref),
        pieces_after=(1, 2, 1, 1, 2, 1))


def _rwkv_tables():
    c = RWKV_CHUNK
    tok = jnp.arange(BATCH * STEP_CHUNKS * c)
    ltri = ((tok[:, None] >= tok[None, :]) & (tok[:, None] // c == tok[None, :] // c)).astype(BF16)
    row = jnp.arange(GROUP_LANES)[:, None]
    col = jnp.arange(GROUP_LANES)[None, :]
    same_head = (row // RWKV_HEAD) == (col // RWKV_HEAD)
    ones = same_head.astype(BF16)
    m_strict = (same_head & (row > col)).astype(F32)
    m_incl = (same_head & (row >= col)).astype(F32)
    return ltri, ones, m_strict, m_incl


def _mixers(pos3, z3, par, wd, wa, wg):
    assert RWKV_CHUNK == RWKV_HEAD and RET_CHUNK == STEP_CHUNKS * RWKV_CHUNK
    c = RET_CHUNK
    ret_tables = _retention_tables()
    rwkv_tables = _rwkv_tables()
    full = lambda a: pl.BlockSpec(a.shape, lambda n: (0,) * a.ndim)
    rows = lambda width, col: pl.BlockSpec((BATCH, c, width), lambda n: (0, n, col))
    return pl.pallas_call(
        _mixer_kernel,
        grid=(SEQ // c,),
        in_specs=[rows(1, 0), rows(SEC, 0), rows(SEC, 1), *map(full, ret_tables),
                  full(par), full(wd), full(wa), full(wg), *map(full, rwkv_tables)],
        out_specs=[rows(RET_WIDTH, 0), rows(RWKV_WIDTH, 0)],
        out_shape=[jax.ShapeDtypeStruct((BATCH, SEQ, RET_WIDTH), BF16),
                   jax.ShapeDtypeStruct((BATCH, SEQ, RWKV_WIDTH), BF16)],
        scratch_shapes=[pltpu.VMEM((BATCH * RET_HEADS, RET_DIM, RET_DIM), F32),
                        pltpu.VMEM((BATCH * RWKV_GROUPS, GROUP_LANES, GROUP_LANES), F32)],
        compiler_params=_params("arbitrary"),
        name="mixers",
    )(pos3, z3, z3, *ret_tables, par, wd, wa, wg, *rwkv_tables)


MG_TM = 1024
MG_TN = 1024


def _merge_kernel(yr_ref, yw_ref, wr_ref, ww_ref, gr_ref, gw_ref, br_ref, bw_ref, o_ref, wrb_ref, wwb_ref):
    @pl.when(pl.program_id(1) == 0)
    def _():
        wrb_ref[...] = wr_ref[...].astype(BF16)
        wwb_ref[...] = ww_ref[...].astype(BF16)

    pr = jnp.dot(yr_ref[...], wrb_ref[...], preferred_element_type=F32)
    pw = jnp.dot(yw_ref[...], wwb_ref[...], preferred_element_type=F32)
    o = jax.nn.sigmoid(gr_ref[...] + br_ref[...]) * pr + jax.nn.sigmoid(gw_ref[...] + bw_ref[...]) * pw
    o_ref[...] = o.astype(BF16)


def _merge(y_ret, y_rw, w_ret, w_rw, z, b_gate):
    g0 = 2 * SEC // MG_TN
    nd = D_MODEL // MG_TN
    return pl.pallas_call(
        _merge_kernel,
        grid=(nd, TOKENS // MG_TM),
        in_specs=[pl.BlockSpec((MG_TM, RET_WIDTH), lambda j, i: (i, 0)),
                  pl.BlockSpec((MG_TM, RWKV_WIDTH), lambda j, i: (i, 0)),
                  pl.BlockSpec((RET_WIDTH, MG_TN), lambda j, i: (0, j)),
                  pl.BlockSpec((RWKV_WIDTH, MG_TN), lambda j, i: (0, j)),
                  pl.BlockSpec((MG_TM, MG_TN), lambda j, i: (i, g0 + j)),
                  pl.BlockSpec((MG_TM, MG_TN), lambda j, i: (i, g0 + nd + j)),
                  pl.BlockSpec((1, MG_TN), lambda j, i: (0, j)),
                  pl.BlockSpec((1, MG_TN), lambda j, i: (0, nd + j))],
        out_specs=pl.BlockSpec((MG_TM, MG_TN), lambda j, i: (i, j)),
        out_shape=jax.ShapeDtypeStruct((TOKENS, D_MODEL), BF16),
        scratch_shapes=[pltpu.VMEM((RET_WIDTH, MG_TN), BF16), pltpu.VMEM((RWKV_WIDTH, MG_TN), BF16)],
        compiler_params=_params("arbitrary", "arbitrary"),
        name="merge",
    )(y_ret, y_rw, w_ret, w_rw, z, z, b_gate, b_gate)


OP_TM = 512


def _oproj_kernel(m_ref, w_ref, x_ref, mod_ref, gain_ref, h_ref, u_ref, wb_ref):
    @pl.when(pl.program_id(0) == 0)
    def _():
        wb_ref[...] = w_ref[...].astype(BF16)

    mod = mod_ref[0]
    h = x_ref[...] + mod[2:3] * jnp.dot(m_ref[...], wb_ref[...], preferred_element_type=F32)
    h_ref[...] = h
    u_ref[...] = _modulated_norm(h, gain_ref[...], mod[4:5], mod[3:4]).astype(BF16)


def _oproj(merged, w_o, x2, mod3, gain):
    blocks_per_batch = SEQ // OP_TM
    return pl.pallas_call(
        _oproj_kernel,
        grid=(TOKENS // OP_TM,),
        in_specs=[pl.BlockSpec((OP_TM, D_MODEL), lambda i: (i, 0)),
                  pl.BlockSpec((D_MODEL, D_MODEL), lambda i: (0, 0), pipeline_mode=pl.Buffered(1)),
                  pl.BlockSpec((OP_TM, D_MODEL), lambda i: (i, 0)),
                  pl.BlockSpec((1, 6, D_MODEL), lambda i: (i // blocks_per_batch, 0, 0)),
                  pl.BlockSpec((1, D_MODEL), lambda i: (0, 0))],
        out_specs=[pl.BlockSpec((OP_TM, D_MODEL), lambda i: (i, 0)),
                   pl.BlockSpec((OP_TM, D_MODEL), lambda i: (i, 0))],
        out_shape=[jax.ShapeDtypeStruct((TOKENS, D_MODEL), F32),
                   jax.ShapeDtypeStruct((TOKENS, D_MODEL), BF16)],
        scratch_shapes=[pltpu.VMEM((D_MODEL, D_MODEL), BF16)],
        compiler_params=_params("arbitrary"),
        name="oproj",
    )(merged, w_o, x2, mod3, gain)


FI_TM = 1024
FI_TN = 512


def _ffn_in_kernel(u_ref, wg_ref, wu_ref, o_ref, wgb_ref, wub_ref):
    @pl.when(pl.program_id(1) == 0)
    def _():
        wgb_ref[...] = wg_ref[...].astype(BF16)
        wub_ref[...] = wu_ref[...].astype(BF16)

    u = u_ref[...]
    gate = jnp.dot(u, wgb_ref[...], preferred_element_type=F32)
    up = jnp.dot(u, wub_ref[...], preferred_element_type=F32)
    o_ref[...] = (_silu(gate) * up).astype(BF16)


def _ffn_in(u2, w_in):
    nt = FFN_HIDDEN // FI_TN
    return pl.pallas_call(
        _ffn_in_kernel,
        grid=(nt, TOKENS // FI_TM),
        in_specs=[pl.BlockSpec((FI_TM, D_MODEL), lambda j, i: (i, 0)),
                  pl.BlockSpec((D_MODEL, FI_TN), lambda j, i: (0, j)),
                  pl.BlockSpec((D_MODEL, FI_TN), lambda j, i: (0, nt + j))],
        out_specs=pl.BlockSpec((FI_TM, FI_TN), lambda j, i: (i, j)),
        out_shape=jax.ShapeDtypeStruct((TOKENS, FFN_HIDDEN), BF16),
        scratch_shapes=[pltpu.VMEM((D_MODEL, FI_TN), BF16), pltpu.VMEM((D_MODEL, FI_TN), BF16)],
        compiler_params=_params("arbitrary", "arbitrary"),
        name="ffn_in",
    )(u2, w_in, w_in)


FO_TM = 1024
FO_TK = 512


def _ffn_out_kernel(a_ref, w_ref, h_ref, mod_ref, gain_ref, o_ref, acc_ref):
    k = pl.program_id(1)

    @pl.when(k == 0)
    def _():
        acc_ref[...] = jnp.dot(a_ref[...], w_ref[...].astype(BF16), preferred_element_type=F32)

    @pl.when(k > 0)
    def _():
        acc_ref[...] += jnp.dot(a_ref[...], w_ref[...].astype(BF16), preferred_element_type=F32)

    @pl.when(k == pl.num_programs(1) - 1)
    def _():
        h = h_ref[...] + mod_ref[0][5:6] * acc_ref[...]
        o_ref[...] = h * lax.rsqrt(jnp.mean(h * h, axis=-1, keepdims=True) + NORM_EPS) * gain_ref[...]


def _ffn_out(act, w_out, h1, mod3, gain):
    blocks_per_batch = SEQ // FO_TM
    return pl.pallas_call(
        _ffn_out_kernel,
        grid=(TOKENS // FO_TM, FFN_HIDDEN // FO_TK),
        in_specs=[pl.BlockSpec((FO_TM, FO_TK), lambda i, k: (i, k)),
                  pl.BlockSpec((FO_TK, D_MODEL), lambda i, k: (k, 0)),
                  pl.BlockSpec((FO_TM, D_MODEL), lambda i, k: (i, 0)),
                  pl.BlockSpec((1, 6, D_MODEL), lambda i, k: (i // blocks_per_batch, 0, 0)),
                  pl.BlockSpec((1, D_MODEL), lambda i, k: (0, 0))],
        out_specs=pl.BlockSpec((FO_TM, D_MODEL), lambda i, k: (i, 0)),
        out_shape=jax.ShapeDtypeStruct((TOKENS, D_MODEL), F32),
        scratch_shapes=[pltpu.VMEM((FO_TM, D_MODEL), F32)],
        compiler_params=_params("arbitrary", "arbitrary", vmem=VMEM_LIMIT_MAX),
        name="ffn_out",
    )(act, w_out, h1, mod3, gain)


def _pad_cols(t, width):
    return jnp.pad(t, ((0, 0), (0, width - t.shape[1])))


def _pad_rows(t, rows):
    return jnp.pad(t, ((0, rows - t.shape[0]), (0, 0)))


def _pack_lora_rows(w_t):
    o3 = HEAD_COLS
    o4 = o3 + DECAY_LORA
    o5 = o4 + ICLR_LORA
    lora = jnp.concatenate([_pad_rows(w_t[o3:o4], LORA_PAD), _pad_rows(w_t[o4:o5], LORA_PAD), w_t[o5:GATE_ROW0]],
                           axis=0)
    return _pad_rows(lora, 2 * SEC - HEAD_COLS)


def _pack_mu(mu):
    o3 = 3 * RWKV_WIDTH
    o4 = o3 + DECAY_LORA
    o5 = o4 + ICLR_LORA
    mu = mu[None, :]
    rw = jnp.concatenate([mu[:, :o3], _pad_cols(mu[:, o3:o4], LORA_PAD), _pad_cols(mu[:, o4:o5], LORA_PAD),
                          mu[:, o5:]], axis=1)
    return _pad_cols(rw, SEC)


def kernel(x, c, positions, w_ada, b_ada, norm_mix, norm_ffn, norm_final, w_in, b_gate, mu_shift, w0, w_decay_up, a0, w_iclr_up, w_gate_up, k_k, k_a, r_k, lnx_w, lnx_b, w_ret_out, w_rwkv_out, w_o, w_ffn_in, w_ffn_out):
    assert x.shape == (BATCH, SEQ, D_MODEL) and w_ada.shape[0] == 1
    x2 = x.reshape(TOKENS, D_MODEL)

    mod = _ada(_pad_rows(c, 8), w_ada[0], b_ada)
    mod3 = mod[:BATCH].reshape(BATCH, 6, D_MODEL)

    w_t = jnp.swapaxes(w_in, 1, 2)[0]
    u1 = _norm1(x2, mod3, norm_mix)
    z = _inproj(u1, w_t, _pack_lora_rows(w_t), _pack_mu(mu_shift[0]))

    par = jnp.concatenate([w0, a0, k_k, k_a, r_k.reshape(1, RWKV_WIDTH), lnx_w, lnx_b,
                           jnp.zeros((1, RWKV_WIDTH), F32)], axis=0)
    y_ret, y_rw = _mixers(positions.reshape(BATCH, SEQ, 1), z.reshape(BATCH, SEQ, Z_COLS), par,
                          _pad_rows(w_decay_up[0], LORA_PAD).astype(BF16),
                          _pad_rows(w_iclr_up[0], LORA_PAD).astype(BF16), w_gate_up[0].astype(BF16))
    y_ret = y_ret.reshape(TOKENS, RET_WIDTH)
    y_rw = y_rw.reshape(TOKENS, RWKV_WIDTH)

    merged = _merge(y_ret, y_rw, w_ret_out[0], w_rwkv_out[0], z, b_gate)
    h1, u2 = _oproj(merged, w_o[0], x2, mod3, norm_ffn)
    act = _ffn_in(u2, w_ffn_in[0])
    out = _ffn_out(act, w_ffn_out[0], h1, mod3, norm_final[None, :])
    return out.reshape(BATCH, SEQ, D_MODEL)
```

```python
import jax
import jax.numpy as jnp
from jax import lax
from jax.experimental import pallas as pl
from jax.experimental.pallas import tpu as pltpu

F32 = jnp.float32
BF16 = jnp.bfloat16

D_MODEL = 2048
BATCH = 2
SEQ = 4096
TOKENS = BATCH * SEQ

RET_HEADS = 4
RET_DIM = 256
RET_WIDTH = RET_HEADS * RET_DIM
RET_CHUNK = 128
ROPE_BASE = 10000.0

RWKV_HEAD = 64
RWKV_WIDTH = 1024
RWKV_HEADS = RWKV_WIDTH // RWKV_HEAD
DECAY_LORA = 96
ICLR_LORA = 96
GATE_LORA = 256
RWKV_CHUNK = 64
STEP_CHUNKS = RET_CHUNK // RWKV_CHUNK
GROUP_LANES = 256
GROUP_HEADS = GROUP_LANES // RWKV_HEAD
RWKV_GROUPS = RWKV_WIDTH // GROUP_LANES
LORA_PAD = 128

FFN_HIDDEN = ((8 * D_MODEL // 3 + 255) // 256) * 256
RET_COLS = 4 * RET_WIDTH
SHIFT_COLS = 3 * RWKV_WIDTH + DECAY_LORA + ICLR_LORA + GATE_LORA
GATE_COLS = 2 * D_MODEL
SEC = 4096
Z_COLS = 3 * SEC
HEAD_COLS = RET_COLS + 3 * RWKV_WIDTH
NORM_EPS = 1e-6
GN_EPS_RET = 1e-5
GN_EPS_RWKV = 64e-5

VMEM_LIMIT = 56 * 1024 * 1024
VMEM_LIMIT_MAX = 60 * 1024 * 1024
SUBLANES = 8


def _params(*sem, vmem=VMEM_LIMIT):
    return pltpu.CompilerParams(dimension_semantics=sem, vmem_limit_bytes=vmem)


def _bdot(a, b):
    return jnp.dot(a.astype(BF16), b.astype(BF16), preferred_element_type=F32)


def _bdot_nt(a, b):
    return lax.dot_general(a.astype(BF16), b.astype(BF16), (((1,), (1,)), ((), ())),
                           preferred_element_type=F32)


def _bdot_tn(a, b):
    return lax.dot_general(a.astype(BF16), b.astype(BF16), (((0,), (0,)), ((), ())),
                           preferred_element_type=F32)


def _silu(x):
    return x * jax.nn.sigmoid(x)


ADA_TN = 1024


def _ada_kernel(c_ref, w_ref, b_ref, o_ref):
    o_ref[...] = _bdot(_silu(c_ref[...]), w_ref[...]) + b_ref[...]


def _ada(c_pad, w_ada, b_ada):
    n = w_ada.shape[1]
    return pl.pallas_call(
        _ada_kernel,
        grid=(n // ADA_TN,),
        in_specs=[pl.BlockSpec((8, D_MODEL), lambda j: (0, 0)),
                  pl.BlockSpec((D_MODEL, ADA_TN), lambda j: (0, j)),
                  pl.BlockSpec((1, ADA_TN), lambda j: (0, j))],
        out_specs=pl.BlockSpec((8, ADA_TN), lambda j: (0, j)),
        out_shape=jax.ShapeDtypeStruct((8, n), F32),
        compiler_params=_params("arbitrary"),
        name="ada",
    )(c_pad, w_ada, b_ada)


IN_TM = 1024
IN_TN = 1024


def _modulated_norm(x, gain, scale, shift):
    y = x * lax.rsqrt(jnp.mean(x * x, axis=-1, keepdims=True) + NORM_EPS)
    return y * gain * (1.0 + scale) + shift


def _norm1_kernel(x_ref, mod_ref, gain_ref, u_ref):
    mod = mod_ref[0]
    u_ref[...] = _modulated_norm(x_ref[...], gain_ref[...], mod[1:2], mod[0:1]).astype(BF16)


def _norm1(x2, mod3, gain):
    blocks_per_batch = SEQ // IN_TM
    return pl.pallas_call(
        _norm1_kernel,
        grid=(TOKENS // IN_TM,),
        in_specs=[pl.BlockSpec((IN_TM, D_MODEL), lambda i: (i, 0)),
                  pl.BlockSpec((1, 6, D_MODEL), lambda i: (i // blocks_per_batch, 0, 0)),
                  pl.BlockSpec((1, D_MODEL), lambda i: (0, 0))],
        out_specs=pl.BlockSpec((IN_TM, D_MODEL), lambda i: (i, 0)),
        out_shape=jax.ShapeDtypeStruct((TOKENS, D_MODEL), BF16),
        compiler_params=_params("arbitrary"),
        name="norm1",
    )(x2, mod3, gain)


IN_LORA_BLOCK = HEAD_COLS // IN_TN
IN_RWKV_BLOCK0 = RET_COLS // IN_TN
GATE_ROW0 = RET_COLS + SHIFT_COLS


def _inproj_kernel(u_ref, w_ref, wl_ref, o_ref, wb_ref):
    j = pl.program_id(0)
    i = pl.program_id(1)

    @pl.when((i == 0) & (j != IN_LORA_BLOCK))
    def _():
        wb_ref[...] = w_ref[...].astype(BF16)

    @pl.when((i == 0) & (j == IN_LORA_BLOCK))
    def _():
        wb_ref[...] = wl_ref[...].astype(BF16)

    o_ref[...] = _bdot_nt(u_ref[...], wb_ref[...])


def _inproj(u, w_t, w_lora):
    assert HEAD_COLS % IN_TN == 0 and 2 * SEC - HEAD_COLS == IN_TN

    def w_row(j, i):
        tn, g0 = IN_TN // SUBLANES, GATE_ROW0 // SUBLANES
        head = jnp.minimum(j, IN_LORA_BLOCK - 1) * tn
        return (jnp.where(j > IN_LORA_BLOCK, g0 + (j - IN_LORA_BLOCK - 1) * tn, head) * SUBLANES, 0)

    return pl.pallas_call(
        _inproj_kernel,
        grid=(Z_COLS // IN_TN, TOKENS // IN_TM),
        in_specs=[pl.BlockSpec((IN_TM, D_MODEL), lambda j, i: (i, 0)),
                  pl.BlockSpec((pl.Element(IN_TN), pl.Element(D_MODEL)), w_row),
                  pl.BlockSpec((IN_TN, D_MODEL), lambda j, i: (0, 0), pipeline_mode=pl.Buffered(1))],
        out_specs=pl.BlockSpec((IN_TM, IN_TN), lambda j, i: (i, j)),
        out_shape=jax.ShapeDtypeStruct((TOKENS, Z_COLS), F32),
        scratch_shapes=[pltpu.VMEM((IN_TN, D_MODEL), BF16)],
        compiler_params=_params("arbitrary", "arbitrary"),
        name="inproj",
    )(u, w_t, w_lora)


def _rows(ref, cols):
    t = ref[:, :, cols]
    return t.reshape(t.shape[0] * t.shape[1], t.shape[2])


def _ret_stages(pos_ref, z_ref, invf_ref, dintra_ref, zeta_ref, xi_ref, cdec_ref, o_ref, state_ref):
    c = RET_CHUNK
    ang = _rows(pos_ref, slice(None)).astype(F32) * invf_ref[...]
    cos = jnp.cos(ang)
    sin = jnp.sin(ang)
    half = RET_DIM // 2
    yield

    def rot(t):
        t1, t2 = t[:, :half], t[:, half:]
        return jnp.concatenate([t1 * cos - t2 * sin, t1 * sin + t2 * cos], axis=-1)

    heads = range(RET_HEADS)
    units = [(b, h) for b in range(BATCH) for h in heads]
    cols = lambda section, h: slice(section * RET_WIDTH + h * RET_DIM, section * RET_WIDTH + (h + 1) * RET_DIM)
    seq = lambda t, b: t[b * c:(b + 1) * c]
    q_all = [rot(_rows(z_ref, cols(0, h))).astype(BF16) for h in heads]
    yield
    k_all = [rot(_rows(z_ref, cols(1, h))) * (RET_DIM ** -0.5) for h in heads]
    v_all = [_rows(z_ref, cols(2, h)).astype(BF16) for h in heads]
    yield
    qb = [seq(q_all[h], b) for b, h in units]
    k = [seq(k_all[h], b) for b, h in units]
    v = [seq(v_all[h], b) for b, h in units]
    scores = [_bdot_nt(qb[u], k[u]) * dintra_ref[h] for u, (b, h) in enumerate(units)]
    state = [state_ref[u] for u in range(len(units))]
    yield
    out = [_bdot(scores[u], v[u]) + _bdot(qb[u], state[u]) * xi_ref[h] for u, (b, h) in enumerate(units)]
    for u, (b, h) in enumerate(units):
        state_ref[u] = state[u] * cdec_ref[h] + _bdot_tn(k[u] * zeta_ref[h], v[u])
    yield
    for u, (b, h) in enumerate(units):
        mu = jnp.mean(out[u], axis=-1, keepdims=True)
        d = out[u] - mu
        var = jnp.mean(d * d, axis=-1, keepdims=True)
        g = z_ref[b, :, cols(3, h)]
        o_ref[b, :, cols(0, h)] = (_silu(g) * (d * lax.rsqrt(var + GN_EPS_RET))).astype(BF16)
        if u % 2:
            yield


def _retention_tables():
    h = RET_HEADS
    half = RET_DIM // 2
    inv_freq = ROPE_BASE ** (-jnp.arange(half, dtype=F32) / half)
    log_gamma = jnp.log(1.0 - 2.0 ** (-5.0 - jnp.arange(h, dtype=F32)))
    idx = jnp.arange(RET_CHUNK, dtype=F32)
    dist = idx[:, None] - idx[None, :]
    decay_intra = jnp.where(dist >= 0, jnp.exp(log_gamma[:, None, None] * jnp.maximum(dist, 0.0)), 0.0)
    zeta = jnp.exp(log_gamma[:, None] * (RET_CHUNK - 1.0 - idx))
    xi = jnp.exp(log_gamma[:, None] * (idx + 1.0))
    chunk_decay = jnp.exp(log_gamma * RET_CHUNK)
    wide = lambda t: jnp.broadcast_to(t[:, :, None], (h, RET_CHUNK, RET_DIM))
    cdec = jnp.broadcast_to(chunk_decay[:, None, None], (h, 1, RET_DIM))
    return inv_freq[None, :], decay_intra, wide(zeta), wide(xi), cdec


def _split2(x):
    hi = x.astype(BF16)
    return hi, (x - hi.astype(F32)).astype(BF16)


def _head_sums(xs, ones_ref):
    c = xs[0].shape[0]
    pieces = []
    for x in xs:
        for p in _split2(x):
            for g in range(RWKV_GROUPS):
                pieces.append(p[:, g * GROUP_LANES:(g + 1) * GROUP_LANES])
    res = jnp.dot(jnp.concatenate(pieces, axis=0), ones_ref[...], preferred_element_type=F32)
    outs = []
    for i in range(len(xs)):
        base = i * 2 * RWKV_GROUPS * c
        cols = []
        for g in range(RWKV_GROUPS):
            hi = res[base + g * c:base + (g + 1) * c]
            lo = res[base + (RWKV_GROUPS + g) * c:base + (RWKV_GROUPS + g + 1) * c]
            cols.append(hi + lo)
        outs.append(jnp.concatenate(cols, axis=-1))
    return outs


def _rwkv_stages(zs_ref, mu_ref, par_ref, wd_ref, wa_ref, wg_ref, ltri_ref, ones_ref, mstrict_ref, mincl_ref,
                 o_ref, s_ref, carry_ref):
    c = RWKV_CHUNK
    w = RWKV_WIDTH
    n_blocks = BATCH * STEP_CHUNKS
    seq = lambda t, rb: t[rb * c:(rb + 1) * c]

    def shifted(cols):
        t = zs_ref[:, :, cols]
        first = lax.broadcasted_iota(jnp.int32, t.shape, 1) == 0
        prev = jnp.where(first, carry_ref[:, :, cols], pltpu.roll(t, 1, 1))
        t = t + (prev - t) * mu_ref[:, cols]
        return t.reshape(t.shape[0] * t.shape[1], t.shape[2])

    r, kw, vw = shifted(slice(0, w)), shifted(slice(w, 2 * w)), shifted(slice(2 * w, 3 * w))
    zw = shifted(slice(3 * w, 3 * w + LORA_PAD))
    za = shifted(slice(3 * w + LORA_PAD, 3 * w + 2 * LORA_PAD))
    zg = shifted(slice(3 * w + 2 * LORA_PAD, 3 * w + 2 * LORA_PAD + GATE_LORA))
    carry_ref[...] = zs_ref[:, RET_CHUNK - 1:RET_CHUNK, :]
    par = par_ref[...]
    w0, a0, k_k, k_a, r_k, lnx_w, lnx_b = (par[i:i + 1] for i in range(7))

    dec_pre = w0 + _bdot(jnp.tanh(zw), wd_ref[...])
    softplus = jnp.maximum(-dec_pre, 0.0) + jnp.log1p(jnp.exp(-jnp.abs(dec_pre)))
    logdec = -jnp.exp(-softplus - 0.5)
    iclr = jax.nn.sigmoid(a0 + _bdot(za, wa_ref[...]))

    kk = kw * k_k
    k_mod = kw * (1.0 + (iclr - 1.0) * k_a)
    (kk_sq,) = _head_sums([kk * kk], ones_ref)
    kk = kk * (1.0 / jnp.maximum(jnp.sqrt(kk_sq), 1e-12))
    a = -kk
    b = kk * iclr

    ld_hi = logdec.astype(BF16)
    ld_mid, ld_lo = _split2(logdec - ld_hi.astype(F32))
    ltri = ltri_ref[...]
    cum = (jnp.dot(ltri, ld_hi, preferred_element_type=F32) + jnp.dot(ltri, ld_mid, preferred_element_type=F32)
           + jnp.dot(ltri, ld_lo, preferred_element_type=F32))
    ends = [cum[(rb + 1) * c - 1:(rb + 1) * c, :] for rb in range(n_blocks)]
    cum_end = jnp.concatenate([jnp.broadcast_to(e, (c, w)) for e in ends], axis=0)
    p_inv = jnp.exp(-cum)
    a_t = a * jnp.exp(cum - logdec)
    r_t = r * jnp.exp(cum)
    b_t = (b * p_inv).astype(BF16)
    k_t = (k_mod * p_inv).astype(BF16)

    lane_head = lax.shift_right_logical(lax.broadcasted_iota(jnp.int32, (c, GROUP_LANES), 1), 6)
    head_masks = [lane_head == h for h in range(GROUP_HEADS)]

    def stack(x):
        return jnp.concatenate([jnp.where(m, x, 0.0).astype(BF16) for m in head_masks], axis=0)

    def rep(x):
        return jnp.concatenate([x] * GROUP_HEADS, axis=0)

    m_strict = mstrict_ref[...]
    m_incl = mincl_ref[...]
    eye = (lax.broadcasted_iota(jnp.int32, (GROUP_LANES, GROUP_LANES), 0)
           == lax.broadcasted_iota(jnp.int32, (GROUP_LANES, GROUP_LANES), 1)).astype(F32)
    n4 = GROUP_HEADS * c
    lane_groups = [slice(g * GROUP_LANES, (g + 1) * GROUP_LANES) for g in range(RWKV_GROUPS)]
    units = [(rb, sl) for rb in range(n_blocks) for sl in lane_groups]
    groups = range(len(units))
    unit = lambda t, u: seq(t, units[u][0])[:, units[u][1]]
    a_s = [stack(unit(a_t, u)) for u in groups]
    r_s = [stack(unit(r_t, u)) for u in groups]
    blk = [_bdot_nt(jnp.concatenate([a_s[g], r_s[g]], axis=0),
                    jnp.concatenate([rep(unit(b_t, g)), rep(unit(k_t, g))], axis=0)) for g in groups]
    a_ab = [blk[g][:n4, :n4] * m_strict for g in groups]
    a_ak = [blk[g][:n4, n4:] * m_strict for g in groups]
    a_rb = [blk[g][n4:, :n4] * m_incl for g in groups]
    a_rk = [blk[g][n4:, n4:] * m_incl for g in groups]

    t_inv = [eye + a_ab[g] for g in groups]
    a_pow = a_ab

    def inverse_round():
        nonlocal a_pow, t_inv
        a_pow = [_bdot(a_pow[g], a_pow[g]) for g in groups]
        t_inv = [t_inv[g] + _bdot(t_inv[g], a_pow[g]) for g in groups]

    yield
    inverse_round()
    g_rw = _bdot(jax.nn.sigmoid(zg), wg_ref[...])
    (rk_sum,) = _head_sums([r * k_mod * r_k], ones_ref)
    bonus = rk_sum * vw
    yield
    inverse_round()
    p_end = jnp.exp(cum_end - cum)
    b_e = b * p_end
    k_e = k_mod * p_end
    decay_end = [jnp.exp(e) for e in ends]
    yield
    inverse_round()
    v_s = [stack(unit(vw, u)) for u in groups]
    be_s = [stack(unit(b_e, u)) for u in groups]
    yield
    inverse_round()
    ke_s = [stack(unit(k_e, u)) for u in groups]
    yield
    inverse_round()
    yield

    a_eff = [_bdot(t_inv[g], a_s[g]) for g in groups]
    av = [_bdot(a_ak[g], v_s[g]) for g in groups]
    u_const = [_bdot(t_inv[g], av[g]) for g in groups]
    yield
    state, y_st = {}, {}
    for j in range(STEP_CHUNKS):
        now = [g for g in groups if units[g][0] % STEP_CHUNKS == j]
        slot = lambda g: (units[g][0] // STEP_CHUNKS) * RWKV_GROUPS + g % RWKV_GROUPS
        before = {g: s_ref[slot(g)] if j == 0 else state[g - RWKV_GROUPS] for g in now}
        sb = {g: before[g].astype(BF16) for g in now}
        u = {g: _bdot_nt(a_eff[g], sb[g]) + u_const[g] for g in now}
        for g in now:
            y_st[g] = _bdot_nt(r_s[g], sb[g]) + _bdot(a_rb[g], u[g]) + _bdot(a_rk[g], v_s[g])
        for g in now:
            rb, sl = units[g]
            state[g] = (before[g] * decay_end[rb][:, sl]
                        + _bdot_tn(jnp.concatenate([u[g].astype(BF16), v_s[g]], axis=0),
                                   jnp.concatenate([be_s[g], ke_s[g]], axis=0)))
            if j == STEP_CHUNKS - 1:
                s_ref[slot(g)] = state[g]
        yield
    heads_summed = [t[:c] + t[c:2 * c] + t[2 * c:3 * c] + t[3 * c:] for t in (y_st[g] for g in groups)]
    y = jnp.concatenate([jnp.concatenate(heads_summed[rb * RWKV_GROUPS:(rb + 1) * RWKV_GROUPS], axis=-1)
                         for rb in range(n_blocks)], axis=0)

    (y_sum,) = _head_sums([y], ones_ref)
    d = y - y_sum * (1.0 / RWKV_HEAD)
    (d_sq,) = _head_sums([d * d], ones_ref)
    y = d * lax.rsqrt(d_sq * (1.0 / RWKV_HEAD) + GN_EPS_RWKV) * lnx_w + lnx_b + bonus
    o_ref[...] = (y * g_rw).astype(BF16).reshape(o_ref.shape)


def _emit_interleaved(primary, secondary, pieces_after):
    for n in pieces_after:
        next(primary, None)
        for _ in range(n):
            next(secondary, None)
    for _ in primary:
        pass
    for _ in secondary:
        pass


def _mixer_kernel(pos_ref, zr_ref, zs_ref, invf_ref, dintra_ref, zeta_ref, xi_ref, cdec_ref, mu_ref, par_ref, wd_ref,
                  wa_ref, wg_ref, ltri_ref, ones_ref, mstrict_ref, mincl_ref, or_ref, ow_ref, rs_ref, ws_ref,
                  carry_ref):
    @pl.when(pl.program_id(0) == 0)
    def _():
        rs_ref[...] = jnp.zeros_like(rs_ref)
        ws_ref[...] = jnp.zeros_like(ws_ref)
        carry_ref[...] = jnp.zeros_like(carry_ref)

    _emit_interleaved(
        _rwkv_stages(zs_ref, mu_ref, par_ref, wd_ref, wa_ref, wg_ref, ltri_ref, ones_ref, mstrict_ref, mincl_ref,
                     ow_ref, ws_ref, carry_ref),
        _ret_stages(pos_ref, zr_ref, invf_ref, dintra_ref, zeta_ref, xi_ref, cdec_ref, or_ref, rs_ref),
        pieces_after=(1, 2, 1, 1, 2, 1))


def _rwkv_tables():
    c = RWKV_CHUNK
    tok = jnp.arange(BATCH * STEP_CHUNKS * c)
    ltri = ((tok[:, None] >= tok[None, :]) & (tok[:, None] // c == tok[None, :] // c)).astype(BF16)
    row = jnp.arange(GROUP_LANES)[:, None]
    col = jnp.arange(GROUP_LANES)[None, :]
    same_head = (row // RWKV_HEAD) == (col // RWKV_HEAD)
    ones = same_head.astype(BF16)
    m_strict = (same_head & (row > col)).astype(F32)
    m_incl = (same_head & (row >= col)).astype(F32)
    return ltri, ones, m_strict, m_incl


def _mixers(pos3, z3, mu_p, par, wd, wa, wg):
    assert RWKV_CHUNK == RWKV_HEAD and RET_CHUNK == STEP_CHUNKS * RWKV_CHUNK
    c = RET_CHUNK
    ret_tables = _retention_tables()
    rwkv_tables = _rwkv_tables()
    full = lambda a: pl.BlockSpec(a.shape, lambda n: (0,) * a.ndim)
    rows = lambda width, col: pl.BlockSpec((BATCH, c, width), lambda n: (0, n, col))
    return pl.pallas_call(
        _mixer_kernel,
        grid=(SEQ // c,),
        in_specs=[rows(1, 0), rows(SEC, 0), rows(SEC, 1), *map(full, ret_tables),
                  full(mu_p), full(par), full(wd), full(wa), full(wg), *map(full, rwkv_tables)],
        out_specs=[rows(RET_WIDTH, 0), rows(RWKV_WIDTH, 0)],
        out_shape=[jax.ShapeDtypeStruct((BATCH, SEQ, RET_WIDTH), BF16),
                   jax.ShapeDtypeStruct((BATCH, SEQ, RWKV_WIDTH), BF16)],
        scratch_shapes=[pltpu.VMEM((BATCH * RET_HEADS, RET_DIM, RET_DIM), F32),
                        pltpu.VMEM((BATCH * RWKV_GROUPS, GROUP_LANES, GROUP_LANES), F32),
                        pltpu.VMEM((BATCH, 1, SEC), F32)],
        compiler_params=_params("arbitrary"),
        name="mixers",
    )(pos3, z3, z3, *ret_tables, mu_p, par, wd, wa, wg, *rwkv_tables)


MG_TM = 1024
MG_TN = 1024


def _merge_kernel(yr_ref, yw_ref, wr_ref, ww_ref, gr_ref, gw_ref, br_ref, bw_ref, o_ref, wrb_ref, wwb_ref):
    @pl.when(pl.program_id(1) == 0)
    def _():
        wrb_ref[...] = wr_ref[...].astype(BF16)
        wwb_ref[...] = ww_ref[...].astype(BF16)

    pr = jnp.dot(yr_ref[...], wrb_ref[...], preferred_element_type=F32)
    pw = jnp.dot(yw_ref[...], wwb_ref[...], preferred_element_type=F32)
    o = jax.nn.sigmoid(gr_ref[...] + br_ref[...]) * pr + jax.nn.sigmoid(gw_ref[...] + bw_ref[...]) * pw
    o_ref[...] = o.astype(BF16)


def _merge(y_ret, y_rw, w_ret, w_rw, z, b_gate):
    g0 = 2 * SEC // MG_TN
    nd = D_MODEL // MG_TN
    return pl.pallas_call(
        _merge_kernel,
        grid=(nd, TOKENS // MG_TM),
        in_specs=[pl.BlockSpec((MG_TM, RET_WIDTH), lambda j, i: (i, 0)),
                  pl.BlockSpec((MG_TM, RWKV_WIDTH), lambda j, i: (i, 0)),
                  pl.BlockSpec((RET_WIDTH, MG_TN), lambda j, i: (0, j)),
                  pl.BlockSpec((RWKV_WIDTH, MG_TN), lambda j, i: (0, j)),
                  pl.BlockSpec((MG_TM, MG_TN), lambda j, i: (i, g0 + j)),
                  pl.BlockSpec((MG_TM, MG_TN), lambda j, i: (i, g0 + nd + j)),
                  pl.BlockSpec((1, MG_TN), lambda j, i: (0, j)),
                  pl.BlockSpec((1, MG_TN), lambda j, i: (0, nd + j))],
        out_specs=pl.BlockSpec((MG_TM, MG_TN), lambda j, i: (i, j)),
        out_shape=jax.ShapeDtypeStruct((TOKENS, D_MODEL), BF16),
        scratch_shapes=[pltpu.VMEM((RET_WIDTH, MG_TN), BF16), pltpu.VMEM((RWKV_WIDTH, MG_TN), BF16)],
        compiler_params=_params("arbitrary", "arbitrary"),
        name="merge",
    )(y_ret, y_rw, w_ret, w_rw, z, z, b_gate, b_gate)


OP_TM = 512


def _oproj_kernel(m_ref, w_ref, x_ref, mod_ref, gain_ref, h_ref, u_ref, wb_ref):
    @pl.when(pl.program_id(0) == 0)
    def _():
        wb_ref[...] = w_ref[...].astype(BF16)

    mod = mod_ref[0]
    h = x_ref[...] + mod[2:3] * jnp.dot(m_ref[...], wb_ref[...], preferred_element_type=F32)
    h_ref[...] = h
    u_ref[...] = _modulated_norm(h, gain_ref[...], mod[4:5], mod[3:4]).astype(BF16)


def _oproj(merged, w_o, x2, mod3, gain):
    blocks_per_batch = SEQ // OP_TM
    return pl.pallas_call(
        _oproj_kernel,
        grid=(TOKENS // OP_TM,),
        in_specs=[pl.BlockSpec((OP_TM, D_MODEL), lambda i: (i, 0)),
                  pl.BlockSpec((D_MODEL, D_MODEL), lambda i: (0, 0), pipeline_mode=pl.Buffered(1)),
                  pl.BlockSpec((OP_TM, D_MODEL), lambda i: (i, 0)),
                  pl.BlockSpec((1, 6, D_MODEL), lambda i: (i // blocks_per_batch, 0, 0)),
                  pl.BlockSpec((1, D_MODEL), lambda i: (0, 0))],
        out_specs=[pl.BlockSpec((OP_TM, D_MODEL), lambda i: (i, 0)),
                   pl.BlockSpec((OP_TM, D_MODEL), lambda i: (i, 0))],
        out_shape=[jax.ShapeDtypeStruct((TOKENS, D_MODEL), F32),
                   jax.ShapeDtypeStruct((TOKENS, D_MODEL), BF16)],
        scratch_shapes=[pltpu.VMEM((D_MODEL, D_MODEL), BF16)],
        compiler_params=_params("arbitrary"),
        name="oproj",
    )(merged, w_o, x2, mod3, gain)


FI_TM = 1024
FI_TN = 512


def _ffn_in_kernel(u_ref, wg_ref, wu_ref, o_ref, wgb_ref, wub_ref):
    @pl.when(pl.program_id(1) == 0)
    def _():
        wgb_ref[...] = wg_ref[...].astype(BF16)
        wub_ref[...] = wu_ref[...].astype(BF16)

    u = u_ref[...]
    gate = jnp.dot(u, wgb_ref[...], preferred_element_type=F32)
    up = jnp.dot(u, wub_ref[...], preferred_element_type=F32)
    o_ref[...] = (_silu(gate) * up).astype(BF16)


def _ffn_in(u2, w_in):
    nt = FFN_HIDDEN // FI_TN
    return pl.pallas_call(
        _ffn_in_kernel,
        grid=(nt, TOKENS // FI_TM),
        in_specs=[pl.BlockSpec((FI_TM, D_MODEL), lambda j, i: (i, 0)),
                  pl.BlockSpec((D_MODEL, FI_TN), lambda j, i: (0, j)),
                  pl.BlockSpec((D_MODEL, FI_TN), lambda j, i: (0, nt + j))],
        out_specs=pl.BlockSpec((FI_TM, FI_TN), lambda j, i: (i, j)),
        out_shape=jax.ShapeDtypeStruct((TOKENS, FFN_HIDDEN), BF16),
        scratch_shapes=[pltpu.VMEM((D_MODEL, FI_TN), BF16), pltpu.VMEM((D_MODEL, FI_TN), BF16)],
        compiler_params=_params("arbitrary", "arbitrary"),
        name="ffn_in",
    )(u2, w_in, w_in)


FO_TM = 1024
FO_TK = 512


def _ffn_out_kernel(a_ref, w_ref, h_ref, mod_ref, gain_ref, o_ref, acc_ref):
    k = pl.program_id(1)

    @pl.when(k == 0)
    def _():
        acc_ref[...] = jnp.dot(a_ref[...], w_ref[...].astype(BF16), preferred_element_type=F32)

    @pl.when(k > 0)
    def _():
        acc_ref[...] += jnp.dot(a_ref[...], w_ref[...].astype(BF16), preferred_element_type=F32)

    @pl.when(k == pl.num_programs(1) - 1)
    def _():
        h = h_ref[...] + mod_ref[0][5:6] * acc_ref[...]
        o_ref[...] = h * lax.rsqrt(jnp.mean(h * h, axis=-1, keepdims=True) + NORM_EPS) * gain_ref[...]


def _ffn_out(act, w_out, h1, mod3, gain):
    blocks_per_batch = SEQ // FO_TM
    return pl.pallas_call(
        _ffn_out_kernel,
        grid=(TOKENS // FO_TM, FFN_HIDDEN // FO_TK),
        in_specs=[pl.BlockSpec((FO_TM, FO_TK), lambda i, k: (i, k)),
                  pl.BlockSpec((FO_TK, D_MODEL), lambda i, k: (k, 0)),
                  pl.BlockSpec((FO_TM, D_MODEL), lambda i, k: (i, 0)),
                  pl.BlockSpec((1, 6, D_MODEL), lambda i, k: (i // blocks_per_batch, 0, 0)),
                  pl.BlockSpec((1, D_MODEL), lambda i, k: (0, 0))],
        out_specs=pl.BlockSpec((FO_TM, D_MODEL), lambda i, k: (i, 0)),
        out_shape=jax.ShapeDtypeStruct((TOKENS, D_MODEL), F32),
        scratch_shapes=[pltpu.VMEM((FO_TM, D_MODEL), F32)],
        compiler_params=_params("arbitrary", "arbitrary", vmem=VMEM_LIMIT_MAX),
        name="ffn_out",
    )(act, w_out, h1, mod3, gain)


def _pad_cols(t, width):
    return jnp.pad(t, ((0, 0), (0, width - t.shape[1])))


def _pad_rows(t, rows):
    return jnp.pad(t, ((0, rows - t.shape[0]), (0, 0)))


def _pack_lora_rows(w_t):
    o3 = HEAD_COLS
    o4 = o3 + DECAY_LORA
    o5 = o4 + ICLR_LORA
    lora = jnp.concatenate([_pad_rows(w_t[o3:o4], LORA_PAD), _pad_rows(w_t[o4:o5], LORA_PAD), w_t[o5:GATE_ROW0]],
                           axis=0)
    return _pad_rows(lora, 2 * SEC - HEAD_COLS)


def _pack_mu(mu):
    o3 = 3 * RWKV_WIDTH
    o4 = o3 + DECAY_LORA
    o5 = o4 + ICLR_LORA
    mu = mu[None, :]
    rw = jnp.concatenate([mu[:, :o3], _pad_cols(mu[:, o3:o4], LORA_PAD), _pad_cols(mu[:, o4:o5], LORA_PAD),
                          mu[:, o5:]], axis=1)
    return _pad_cols(rw, SEC)


def kernel(x, c, positions, w_ada, b_ada, norm_mix, norm_ffn, norm_final, w_in, b_gate, mu_shift, w0, w_decay_up, a0, w_iclr_up, w_gate_up, k_k, k_a, r_k, lnx_w, lnx_b, w_ret_out, w_rwkv_out, w_o, w_ffn_in, w_ffn_out):
    assert x.shape == (BATCH, SEQ, D_MODEL) and w_ada.shape[0] == 1
    x2 = x.reshape(TOKENS, D_MODEL)

    mod = _ada(_pad_rows(c, 8), w_ada[0], b_ada)
    mod3 = mod[:BATCH].reshape(BATCH, 6, D_MODEL)

    w_t = jnp.swapaxes(w_in, 1, 2)[0]
    u1 = _norm1(x2, mod3, norm_mix)
    z = _inproj(u1, w_t, _pack_lora_rows(w_t))

    par = jnp.concatenate([w0, a0, k_k, k_a, r_k.reshape(1, RWKV_WIDTH), lnx_w, lnx_b,
                           jnp.zeros((1, RWKV_WIDTH), F32)], axis=0)
    y_ret, y_rw = _mixers(positions.reshape(BATCH, SEQ, 1), z.reshape(BATCH, SEQ, Z_COLS), _pack_mu(mu_shift[0]), par,
                          _pad_rows(w_decay_up[0], LORA_PAD).astype(BF16),
                          _pad_rows(w_iclr_up[0], LORA_PAD).astype(BF16), w_gate_up[0].astype(BF16))
    y_ret = y_ret.reshape(TOKENS, RET_WIDTH)
    y_rw = y_rw.reshape(TOKENS, RWKV_WIDTH)

    merged = _merge(y_ret, y_rw, w_ret_out[0], w_rwkv_out[0], z, b_gate)
    h1, u2 = _oproj(merged, w_o[0], x2, mod3, norm_ffn)
    act = _ffn_in(u2, w_ffn_in[0])
    out = _ffn_out(act, w_ffn_out[0], h1, mod3, norm_final[None, :])
    return out.reshape(BATCH, SEQ, D_MODEL)
```

```python
import jax
import jax.numpy as jnp
from jax import lax
from jax.experimental import pallas as pl
from jax.experimental.pallas import tpu as pltpu

F32 = jnp.float32
BF16 = jnp.bfloat16

D_MODEL = 2048
BATCH = 2
SEQ = 4096
TOKENS = BATCH * SEQ

RET_HEADS = 4
RET_DIM = 256
RET_WIDTH = RET_HEADS * RET_DIM
RET_CHUNK = 128
ROPE_BASE = 10000.0

RWKV_HEAD = 64
RWKV_WIDTH = 1024
RWKV_HEADS = RWKV_WIDTH // RWKV_HEAD
DECAY_LORA = 96
ICLR_LORA = 96
GATE_LORA = 256
RWKV_CHUNK = 64
STEP_CHUNKS = RET_CHUNK // RWKV_CHUNK
GROUP_LANES = 256
GROUP_HEADS = GROUP_LANES // RWKV_HEAD
RWKV_GROUPS = RWKV_WIDTH // GROUP_LANES
LORA_PAD = 128

FFN_HIDDEN = ((8 * D_MODEL // 3 + 255) // 256) * 256
RET_COLS = 4 * RET_WIDTH
SHIFT_COLS = 3 * RWKV_WIDTH + DECAY_LORA + ICLR_LORA + GATE_LORA
GATE_COLS = 2 * D_MODEL
SEC = 4096
Z_COLS = 3 * SEC
HEAD_COLS = RET_COLS + 3 * RWKV_WIDTH
NORM_EPS = 1e-6
GN_EPS_RET = 1e-5
GN_EPS_RWKV = 64e-5

VMEM_LIMIT = 56 * 1024 * 1024
VMEM_LIMIT_MAX = 60 * 1024 * 1024
SUBLANES = 8


def _params(*sem, vmem=VMEM_LIMIT):
    return pltpu.CompilerParams(dimension_semantics=sem, vmem_limit_bytes=vmem)


def _bdot(a, b):
    return jnp.dot(a.astype(BF16), b.astype(BF16), preferred_element_type=F32)


def _bdot_nt(a, b):
    return lax.dot_general(a.astype(BF16), b.astype(BF16), (((1,), (1,)), ((), ())),
                           preferred_element_type=F32)


def _bdot_tn(a, b):
    return lax.dot_general(a.astype(BF16), b.astype(BF16), (((0,), (0,)), ((), ())),
                           preferred_element_type=F32)


def _silu(x):
    return x * jax.nn.sigmoid(x)


ADA_TN = 1024


def _ada_kernel(c_ref, w_ref, b_ref, o_ref):
    o_ref[...] = _bdot(_silu(c_ref[...]), w_ref[...]) + b_ref[...]


def _ada(c_pad, w_ada, b_ada):
    n = w_ada.shape[1]
    return pl.pallas_call(
        _ada_kernel,
        grid=(n // ADA_TN,),
        in_specs=[pl.BlockSpec((8, D_MODEL), lambda j: (0, 0)),
                  pl.BlockSpec((D_MODEL, ADA_TN), lambda j: (0, j)),
                  pl.BlockSpec((1, ADA_TN), lambda j: (0, j))],
        out_specs=pl.BlockSpec((8, ADA_TN), lambda j: (0, j)),
        out_shape=jax.ShapeDtypeStruct((8, n), F32),
        compiler_params=_params("arbitrary"),
        name="ada",
    )(c_pad, w_ada, b_ada)


IN_TM = 1024
IN_TN = 1024


def _modulated_norm(x, gain, scale, shift):
    y = x * lax.rsqrt(jnp.mean(x * x, axis=-1, keepdims=True) + NORM_EPS)
    return y * gain * (1.0 + scale) + shift


def _norm1_kernel(x_ref, mod_ref, gain_ref, u_ref):
    mod = mod_ref[0]
    u_ref[...] = _modulated_norm(x_ref[...], gain_ref[...], mod[1:2], mod[0:1]).astype(BF16)


def _norm1(x2, mod3, gain):
    blocks_per_batch = SEQ // IN_TM
    return pl.pallas_call(
        _norm1_kernel,
        grid=(TOKENS // IN_TM,),
        in_specs=[pl.BlockSpec((IN_TM, D_MODEL), lambda i: (i, 0)),
                  pl.BlockSpec((1, 6, D_MODEL), lambda i: (i // blocks_per_batch, 0, 0)),
                  pl.BlockSpec((1, D_MODEL), lambda i: (0, 0))],
        out_specs=pl.BlockSpec((IN_TM, D_MODEL), lambda i: (i, 0)),
        out_shape=jax.ShapeDtypeStruct((TOKENS, D_MODEL), BF16),
        compiler_params=_params("arbitrary"),
        name="norm1",
    )(x2, mod3, gain)


IN_LORA_BLOCK = HEAD_COLS // IN_TN
IN_RWKV_BLOCK0 = RET_COLS // IN_TN
GATE_ROW0 = RET_COLS + SHIFT_COLS


def _inproj_kernel(u_ref, w_ref, wl_ref, o_ref, wb_ref):
    j = pl.program_id(0)
    i = pl.program_id(1)

    @pl.when((i == 0) & (j != IN_LORA_BLOCK))
    def _():
        wb_ref[...] = w_ref[...].astype(BF16)

    @pl.when((i == 0) & (j == IN_LORA_BLOCK))
    def _():
        wb_ref[...] = wl_ref[...].astype(BF16)

    o_ref[...] = _bdot_nt(u_ref[...], wb_ref[...])


def _inproj(u, w_t, w_lora):
    assert HEAD_COLS % IN_TN == 0 and 2 * SEC - HEAD_COLS == IN_TN

    def w_row(j, i):
        tn, g0 = IN_TN // SUBLANES, GATE_ROW0 // SUBLANES
        head = jnp.minimum(j, IN_LORA_BLOCK - 1) * tn
        return (jnp.where(j > IN_LORA_BLOCK, g0 + (j - IN_LORA_BLOCK - 1) * tn, head) * SUBLANES, 0)

    return pl.pallas_call(
        _inproj_kernel,
        grid=(Z_COLS // IN_TN, TOKENS // IN_TM),
        in_specs=[pl.BlockSpec((IN_TM, D_MODEL), lambda j, i: (i, 0)),
                  pl.BlockSpec((pl.Element(IN_TN), pl.Element(D_MODEL)), w_row),
                  pl.BlockSpec((IN_TN, D_MODEL), lambda j, i: (0, 0), pipeline_mode=pl.Buffered(1))],
        out_specs=pl.BlockSpec((IN_TM, IN_TN), lambda j, i: (i, j)),
        out_shape=jax.ShapeDtypeStruct((TOKENS, Z_COLS), F32),
        scratch_shapes=[pltpu.VMEM((IN_TN, D_MODEL), BF16)],
        compiler_params=_params("arbitrary", "arbitrary"),
        name="inproj",
    )(u, w_t, w_lora)


def _rows(ref, cols):
    t = ref[:, :, cols]
    return t.reshape(t.shape[0] * t.shape[1], t.shape[2])


def _ret_stages(pos_ref, z_ref, invf_ref, dintra_ref, zeta_ref, xi_ref, cdec_ref, o_ref, state_ref):
    c = RET_CHUNK
    ang = _rows(pos_ref, slice(None)).astype(F32) * invf_ref[...]
    cos = jnp.cos(ang)
    sin = jnp.sin(ang)
    half = RET_DIM // 2
    yield

    def rot(t):
        t1, t2 = t[:, :half], t[:, half:]
        return jnp.concatenate([t1 * cos - t2 * sin, t1 * sin + t2 * cos], axis=-1)

    heads = range(RET_HEADS)
    units = [(b, h) for b in range(BATCH) for h in heads]
    cols = lambda section, h: slice(section * RET_WIDTH + h * RET_DIM, section * RET_WIDTH + (h + 1) * RET_DIM)
    seq = lambda t, b: t[b * c:(b + 1) * c]
    q_all = [rot(_rows(z_ref, cols(0, h))).astype(BF16) for h in heads]
    yield
    k_all = [rot(_rows(z_ref, cols(1, h))) * (RET_DIM ** -0.5) for h in heads]
    v_all = [_rows(z_ref, cols(2, h)).astype(BF16) for h in heads]
    yield
    qb = [seq(q_all[h], b) for b, h in units]
    k = [seq(k_all[h], b) for b, h in units]
    v = [seq(v_all[h], b) for b, h in units]
    scores = [_bdot_nt(qb[u], k[u]) * dintra_ref[h] for u, (b, h) in enumerate(units)]
    state = [state_ref[u] for u in range(len(units))]
    yield
    out = [_bdot(scores[u], v[u]) + _bdot(qb[u], state[u]) * xi_ref[h] for u, (b, h) in enumerate(units)]
    for u, (b, h) in enumerate(units):
        state_ref[u] = state[u] * cdec_ref[h] + _bdot_tn(k[u] * zeta_ref[h], v[u])
    yield
    for u, (b, h) in enumerate(units):
        mu = jnp.mean(out[u], axis=-1, keepdims=True)
        d = out[u] - mu
        var = jnp.mean(d * d, axis=-1, keepdims=True)
        g = z_ref[b, :, cols(3, h)]
        o_ref[b, :, cols(0, h)] = (_silu(g) * (d * lax.rsqrt(var + GN_EPS_RET))).astype(BF16)
        if u % 2:
            yield


def _retention_tables():
    h = RET_HEADS
    half = RET_DIM // 2
    inv_freq = ROPE_BASE ** (-jnp.arange(half, dtype=F32) / half)
    log_gamma = jnp.log(1.0 - 2.0 ** (-5.0 - jnp.arange(h, dtype=F32)))
    idx = jnp.arange(RET_CHUNK, dtype=F32)
    dist = idx[:, None] - idx[None, :]
    decay_intra = jnp.where(dist >= 0, jnp.exp(log_gamma[:, None, None] * jnp.maximum(dist, 0.0)), 0.0)
    zeta = jnp.exp(log_gamma[:, None] * (RET_CHUNK - 1.0 - idx))
    xi = jnp.exp(log_gamma[:, None] * (idx + 1.0))
    chunk_decay = jnp.exp(log_gamma * RET_CHUNK)
    wide = lambda t: jnp.broadcast_to(t[:, :, None], (h, RET_CHUNK, RET_DIM))
    cdec = jnp.broadcast_to(chunk_decay[:, None, None], (h, 1, RET_DIM))
    return inv_freq[None, :], decay_intra, wide(zeta), wide(xi), cdec


def _split2(x):
    hi = x.astype(BF16)
    return hi, (x - hi.astype(F32)).astype(BF16)


def _head_sums(xs, ones_ref):
    c = xs[0].shape[0]
    pieces = [x[:, g * GROUP_LANES:(g + 1) * GROUP_LANES].astype(BF16) for x in xs for g in range(RWKV_GROUPS)]
    res = jnp.dot(jnp.concatenate(pieces, axis=0), ones_ref[...], preferred_element_type=F32)
    return [jnp.concatenate([res[(i * RWKV_GROUPS + g) * c:(i * RWKV_GROUPS + g + 1) * c]
                             for g in range(RWKV_GROUPS)], axis=-1) for i in range(len(xs))]


def _rwkv_stages(zs_ref, mu_ref, par_ref, wd_ref, wa_ref, wg_ref, ltri_ref, ones_ref, mstrict_ref, mincl_ref,
                 o_ref, s_ref, carry_ref):
    c = RWKV_CHUNK
    w = RWKV_WIDTH
    n_blocks = BATCH * STEP_CHUNKS
    seq = lambda t, rb: t[rb * c:(rb + 1) * c]

    def shifted(cols):
        t = zs_ref[:, :, cols]
        first = lax.broadcasted_iota(jnp.int32, t.shape, 1) == 0
        prev = jnp.where(first, carry_ref[:, :, cols], pltpu.roll(t, 1, 1))
        t = t + (prev - t) * mu_ref[:, cols]
        return t.reshape(t.shape[0] * t.shape[1], t.shape[2])

    r, kw, vw = shifted(slice(0, w)), shifted(slice(w, 2 * w)), shifted(slice(2 * w, 3 * w))
    zw = shifted(slice(3 * w, 3 * w + LORA_PAD))
    za = shifted(slice(3 * w + LORA_PAD, 3 * w + 2 * LORA_PAD))
    zg = shifted(slice(3 * w + 2 * LORA_PAD, 3 * w + 2 * LORA_PAD + GATE_LORA))
    carry_ref[...] = zs_ref[:, RET_CHUNK - 1:RET_CHUNK, :]
    par = par_ref[...]
    w0, a0, k_k, k_a, r_k, lnx_w, lnx_b = (par[i:i + 1] for i in range(7))

    dec_pre = w0 + _bdot(jnp.tanh(zw), wd_ref[...])
    softplus = jnp.maximum(-dec_pre, 0.0) + jnp.log1p(jnp.exp(-jnp.abs(dec_pre)))
    logdec = -jnp.exp(-softplus - 0.5)
    iclr = jax.nn.sigmoid(a0 + _bdot(za, wa_ref[...]))

    kk = kw * k_k
    k_mod = kw * (1.0 + (iclr - 1.0) * k_a)
    (kk_sq,) = _head_sums([kk * kk], ones_ref)
    kk = kk * (1.0 / jnp.maximum(jnp.sqrt(kk_sq), 1e-12))
    a = -kk
    b = kk * iclr

    ld_hi = logdec.astype(BF16)
    ld_mid, ld_lo = _split2(logdec - ld_hi.astype(F32))
    ltri = ltri_ref[...]
    cum = (jnp.dot(ltri, ld_hi, preferred_element_type=F32) + jnp.dot(ltri, ld_mid, preferred_element_type=F32)
           + jnp.dot(ltri, ld_lo, preferred_element_type=F32))
    ends = [cum[(rb + 1) * c - 1:(rb + 1) * c, :] for rb in range(n_blocks)]
    cum_end = jnp.concatenate([jnp.broadcast_to(e, (c, w)) for e in ends], axis=0)
    p_inv = jnp.exp(-cum)
    a_t = a * jnp.exp(cum - logdec)
    r_t = r * jnp.exp(cum)
    b_t = (b * p_inv).astype(BF16)
    k_t = (k_mod * p_inv).astype(BF16)

    lane_head = lax.shift_right_logical(lax.broadcasted_iota(jnp.int32, (c, GROUP_LANES), 1), 6)
    head_masks = [lane_head == h for h in range(GROUP_HEADS)]

    def stack(x):
        return jnp.concatenate([jnp.where(m, x, 0.0).astype(BF16) for m in head_masks], axis=0)

    def rep(x):
        return jnp.concatenate([x] * GROUP_HEADS, axis=0)

    m_strict = mstrict_ref[...]
    m_incl = mincl_ref[...]
    eye = (lax.broadcasted_iota(jnp.int32, (GROUP_LANES, GROUP_LANES), 0)
           == lax.broadcasted_iota(jnp.int32, (GROUP_LANES, GROUP_LANES), 1)).astype(F32)
    n4 = GROUP_HEADS * c
    lane_groups = [slice(g * GROUP_LANES, (g + 1) * GROUP_LANES) for g in range(RWKV_GROUPS)]
    units = [(rb, sl) for rb in range(n_blocks) for sl in lane_groups]
    groups = range(len(units))
    unit = lambda t, u: seq(t, units[u][0])[:, units[u][1]]
    a_s = [stack(unit(a_t, u)) for u in groups]
    r_s = [stack(unit(r_t, u)) for u in groups]
    blk = [_bdot_nt(jnp.concatenate([a_s[g], r_s[g]], axis=0),
                    jnp.concatenate([rep(unit(b_t, g)), rep(unit(k_t, g))], axis=0)) for g in groups]
    a_ab = [blk[g][:n4, :n4] * m_strict for g in groups]
    a_ak = [blk[g][:n4, n4:] * m_strict for g in groups]
    a_rb = [blk[g][n4:, :n4] * m_incl for g in groups]
    a_rk = [blk[g][n4:, n4:] * m_incl for g in groups]

    t_inv = [eye + a_ab[g] for g in groups]
    a_pow = a_ab

    def inverse_round():
        nonlocal a_pow, t_inv
        a_pow = [_bdot(a_pow[g], a_pow[g]) for g in groups]
        t_inv = [t_inv[g] + _bdot(t_inv[g], a_pow[g]) for g in groups]

    yield
    inverse_round()
    g_rw = _bdot(jax.nn.sigmoid(zg), wg_ref[...])
    (rk_sum,) = _head_sums([r * k_mod * r_k], ones_ref)
    bonus = rk_sum * vw
    yield
    inverse_round()
    p_end = jnp.exp(cum_end - cum)
    b_e = b * p_end
    k_e = k_mod * p_end
    decay_end = [jnp.exp(e) for e in ends]
    yield
    inverse_round()
    v_s = [stack(unit(vw, u)) for u in groups]
    be_s = [stack(unit(b_e, u)) for u in groups]
    yield
    inverse_round()
    ke_s = [stack(unit(k_e, u)) for u in groups]
    yield
    inverse_round()
    yield

    a_eff = [_bdot(t_inv[g], a_s[g]) for g in groups]
    av = [_bdot(a_ak[g], v_s[g]) for g in groups]
    u_const = [_bdot(t_inv[g], av[g]) for g in groups]
    yield
    state, y_st = {}, {}
    for j in range(STEP_CHUNKS):
        now = [g for g in groups if units[g][0] % STEP_CHUNKS == j]
        slot = lambda g: (units[g][0] // STEP_CHUNKS) * RWKV_GROUPS + g % RWKV_GROUPS
        before = {g: s_ref[slot(g)] if j == 0 else state[g - RWKV_GROUPS] for g in now}
        sb = {g: before[g].astype(BF16) for g in now}
        u = {g: _bdot_nt(a_eff[g], sb[g]) + u_const[g] for g in now}
        for g in now:
            y_st[g] = _bdot_nt(r_s[g], sb[g]) + _bdot(a_rb[g], u[g]) + _bdot(a_rk[g], v_s[g])
        for g in now:
            rb, sl = units[g]
            state[g] = (before[g] * decay_end[rb][:, sl]
                        + _bdot_tn(jnp.concatenate([u[g].astype(BF16), v_s[g]], axis=0),
                                   jnp.concatenate([be_s[g], ke_s[g]], axis=0)))
            if j == STEP_CHUNKS - 1:
                s_ref[slot(g)] = state[g]
        yield
    heads_summed = [t[:c] + t[c:2 * c] + t[2 * c:3 * c] + t[3 * c:] for t in (y_st[g] for g in groups)]
    y = jnp.concatenate([jnp.concatenate(heads_summed[rb * RWKV_GROUPS:(rb + 1) * RWKV_GROUPS], axis=-1)
                         for rb in range(n_blocks)], axis=0)

    (y_sum,) = _head_sums([y], ones_ref)
    d = y - y_sum * (1.0 / RWKV_HEAD)
    (d_sq,) = _head_sums([d * d], ones_ref)
    y = d * lax.rsqrt(d_sq * (1.0 / RWKV_HEAD) + GN_EPS_RWKV) * lnx_w + lnx_b + bonus
    o_ref[...] = (y * g_rw).astype(BF16).reshape(o_ref.shape)


def _emit_interleaved(primary, secondary, pieces_after):
    for n in pieces_after:
        next(primary, None)
        for _ in range(n):
            next(secondary, None)
    for _ in primary:
        pass
    for _ in secondary:
        pass


def _mixer_kernel(pos_ref, zr_ref, zs_ref, invf_ref, dintra_ref, zeta_ref, xi_ref, cdec_ref, mu_ref, par_ref, wd_ref,
                  wa_ref, wg_ref, ltri_ref, ones_ref, mstrict_ref, mincl_ref, or_ref, ow_ref, rs_ref, ws_ref,
                  carry_ref):
    @pl.when(pl.program_id(0) == 0)
    def _():
        rs_ref[...] = jnp.zeros_like(rs_ref)
        ws_ref[...] = jnp.zeros_like(ws_ref)
        carry_ref[...] = jnp.zeros_like(carry_ref)

    _emit_interleaved(
        _rwkv_stages(zs_ref, mu_ref, par_ref, wd_ref, wa_ref, wg_ref, ltri_ref, ones_ref, mstrict_ref, mincl_ref,
                     ow_ref, ws_ref, carry_ref),
        _ret_stages(pos_ref, zr_ref, invf_ref, dintra_ref, zeta_ref, xi_ref, cdec_ref, or_ref, rs_ref),
        pieces_after=(1, 2, 1, 1, 2, 1))


def _rwkv_tables():
    c = RWKV_CHUNK
    tok = jnp.arange(BATCH * STEP_CHUNKS * c)
    ltri = ((tok[:, None] >= tok[None, :]) & (tok[:, None] // c == tok[None, :] // c)).astype(BF16)
    row = jnp.arange(GROUP_LANES)[:, None]
    col = jnp.arange(GROUP_LANES)[None, :]
    same_head = (row // RWKV_HEAD) == (col // RWKV_HEAD)
    ones = same_head.astype(BF16)
    m_strict = (same_head & (row > col)).astype(F32)
    m_incl = (same_head & (row >= col)).astype(F32)
    return ltri, ones, m_strict, m_incl


def _mixers(pos3, z3, mu_p, par, wd, wa, wg):
    assert RWKV_CHUNK == RWKV_HEAD and RET_CHUNK == STEP_CHUNKS * RWKV_CHUNK
    c = RET_CHUNK
    ret_tables = _retention_tables()
    rwkv_tables = _rwkv_tables()
    full = lambda a: pl.BlockSpec(a.shape, lambda n: (0,) * a.ndim)
    rows = lambda width, col: pl.BlockSpec((BATCH, c, width), lambda n: (0, n, col))
    return pl.pallas_call(
        _mixer_kernel,
        grid=(SEQ // c,),
        in_specs=[rows(1, 0), rows(SEC, 0), rows(SEC, 1), *map(full, ret_tables),
                  full(mu_p), full(par), full(wd), full(wa), full(wg), *map(full, rwkv_tables)],
        out_specs=[rows(RET_WIDTH, 0), rows(RWKV_WIDTH, 0)],
        out_shape=[jax.ShapeDtypeStruct((BATCH, SEQ, RET_WIDTH), BF16),
                   jax.ShapeDtypeStruct((BATCH, SEQ, RWKV_WIDTH), BF16)],
        scratch_shapes=[pltpu.VMEM((BATCH * RET_HEADS, RET_DIM, RET_DIM), F32),
                        pltpu.VMEM((BATCH * RWKV_GROUPS, GROUP_LANES, GROUP_LANES), F32),
                        pltpu.VMEM((BATCH, 1, SEC), F32)],
        compiler_params=_params("arbitrary"),
        name="mixers",
    )(pos3, z3, z3, *ret_tables, mu_p, par, wd, wa, wg, *rwkv_tables)


MG_TM = 1024
MG_TN = 1024


def _merge_kernel(yr_ref, yw_ref, wr_ref, ww_ref, gr_ref, gw_ref, br_ref, bw_ref, o_ref, wrb_ref, wwb_ref):
    @pl.when(pl.program_id(1) == 0)
    def _():
        wrb_ref[...] = wr_ref[...].astype(BF16)
        wwb_ref[...] = ww_ref[...].astype(BF16)

    pr = jnp.dot(yr_ref[...], wrb_ref[...], preferred_element_type=F32)
    pw = jnp.dot(yw_ref[...], wwb_ref[...], preferred_element_type=F32)
    o = jax.nn.sigmoid(gr_ref[...] + br_ref[...]) * pr + jax.nn.sigmoid(gw_ref[...] + bw_ref[...]) * pw
    o_ref[...] = o.astype(BF16)


def _merge(y_ret, y_rw, w_ret, w_rw, z, b_gate):
    g0 = 2 * SEC // MG_TN
    nd = D_MODEL // MG_TN
    return pl.pallas_call(
        _merge_kernel,
        grid=(nd, TOKENS // MG_TM),
        in_specs=[pl.BlockSpec((MG_TM, RET_WIDTH), lambda j, i: (i, 0)),
                  pl.BlockSpec((MG_TM, RWKV_WIDTH), lambda j, i: (i, 0)),
                  pl.BlockSpec((RET_WIDTH, MG_TN), lambda j, i: (0, j)),
                  pl.BlockSpec((RWKV_WIDTH, MG_TN), lambda j, i: (0, j)),
                  pl.BlockSpec((MG_TM, MG_TN), lambda j, i: (i, g0 + j)),
                  pl.BlockSpec((MG_TM, MG_TN), lambda j, i: (i, g0 + nd + j)),
                  pl.BlockSpec((1, MG_TN), lambda j, i: (0, j)),
                  pl.BlockSpec((1, MG_TN), lambda j, i: (0, nd + j))],
        out_specs=pl.BlockSpec((MG_TM, MG_TN), lambda j, i: (i, j)),
        out_shape=jax.ShapeDtypeStruct((TOKENS, D_MODEL), BF16),
        scratch_shapes=[pltpu.VMEM((RET_WIDTH, MG_TN), BF16), pltpu.VMEM((RWKV_WIDTH, MG_TN), BF16)],
        compiler_params=_params("arbitrary", "arbitrary"),
        name="merge",
    )(y_ret, y_rw, w_ret, w_rw, z, z, b_gate, b_gate)


OP_TM = 512


def _oproj_kernel(m_ref, w_ref, x_ref, mod_ref, gain_ref, h_ref, u_ref, wb_ref):
    @pl.when(pl.program_id(0) == 0)
    def _():
        wb_ref[...] = w_ref[...].astype(BF16)

    mod = mod_ref[0]
    h = x_ref[...] + mod[2:3] * jnp.dot(m_ref[...], wb_ref[...], preferred_element_type=F32)
    h_ref[...] = h
    u_ref[...] = _modulated_norm(h, gain_ref[...], mod[4:5], mod[3:4]).astype(BF16)


def _oproj(merged, w_o, x2, mod3, gain):
    blocks_per_batch = SEQ // OP_TM
    return pl.pallas_call(
        _oproj_kernel,
        grid=(TOKENS // OP_TM,),
        in_specs=[pl.BlockSpec((OP_TM, D_MODEL), lambda i: (i, 0)),
                  pl.BlockSpec((D_MODEL, D_MODEL), lambda i: (0, 0), pipeline_mode=pl.Buffered(1)),
                  pl.BlockSpec((OP_TM, D_MODEL), lambda i: (i, 0)),
                  pl.BlockSpec((1, 6, D_MODEL), lambda i: (i // blocks_per_batch, 0, 0)),
                  pl.BlockSpec((1, D_MODEL), lambda i: (0, 0))],
        out_specs=[pl.BlockSpec((OP_TM, D_MODEL), lambda i: (i, 0)),
                   pl.BlockSpec((OP_TM, D_MODEL), lambda i: (i, 0))],
        out_shape=[jax.ShapeDtypeStruct((TOKENS, D_MODEL), F32),
                   jax.ShapeDtypeStruct((TOKENS, D_MODEL), BF16)],
        scratch_shapes=[pltpu.VMEM((D_MODEL, D_MODEL), BF16)],
        compiler_params=_params("arbitrary"),
        name="oproj",
    )(merged, w_o, x2, mod3, gain)


FI_TM = 1024
FI_TN = 512


def _ffn_in_kernel(u_ref, wg_ref, wu_ref, o_ref, wgb_ref, wub_ref):
    @pl.when(pl.program_id(1) == 0)
    def _():
        wgb_ref[...] = wg_ref[...].astype(BF16)
        wub_ref[...] = wu_ref[...].astype(BF16)

    u = u_ref[...]
    gate = jnp.dot(u, wgb_ref[...], preferred_element_type=F32)
    up = jnp.dot(u, wub_ref[...], preferred_element_type=F32)
    o_ref[...] = (_silu(gate) * up).astype(BF16)


def _ffn_in(u2, w_in):
    nt = FFN_HIDDEN // FI_TN
    return pl.pallas_call(
        _ffn_in_kernel,
        grid=(nt, TOKENS // FI_TM),
        in_specs=[pl.BlockSpec((FI_TM, D_MODEL), lambda j, i: (i, 0)),
                  pl.BlockSpec((D_MODEL, FI_TN), lambda j, i: (0, j)),
                  pl.BlockSpec((D_MODEL, FI_TN), lambda j, i: (0, nt + j))],
        out_specs=pl.BlockSpec((FI_TM, FI_TN), lambda j, i: (i, j)),
        out_shape=jax.ShapeDtypeStruct((TOKENS, FFN_HIDDEN), BF16),
        scratch_shapes=[pltpu.VMEM((D_MODEL, FI_TN), BF16), pltpu.VMEM((D_MODEL, FI_TN), BF16)],
        compiler_params=_params("arbitrary", "arbitrary"),
        name="ffn_in",
    )(u2, w_in, w_in)


FO_TM = 1024
FO_TK = 512


def _ffn_out_kernel(a_ref, w_ref, h_ref, mod_ref, gain_ref, o_ref, acc_ref):
    k = pl.program_id(1)

    @pl.when(k == 0)
    def _():
        acc_ref[...] = jnp.dot(a_ref[...], w_ref[...].astype(BF16), preferred_element_type=F32)

    @pl.when(k > 0)
    def _():
        acc_ref[...] += jnp.dot(a_ref[...], w_ref[...].astype(BF16), preferred_element_type=F32)

    @pl.when(k == pl.num_programs(1) - 1)
    def _():
        h = h_ref[...] + mod_ref[0][5:6] * acc_ref[...]
        o_ref[...] = h * lax.rsqrt(jnp.mean(h * h, axis=-1, keepdims=True) + NORM_EPS) * gain_ref[...]


def _ffn_out(act, w_out, h1, mod3, gain):
    blocks_per_batch = SEQ // FO_TM
    return pl.pallas_call(
        _ffn_out_kernel,
        grid=(TOKENS // FO_TM, FFN_HIDDEN // FO_TK),
        in_specs=[pl.BlockSpec((FO_TM, FO_TK), lambda i, k: (i, k)),
                  pl.BlockSpec((FO_TK, D_MODEL), lambda i, k: (k, 0)),
                  pl.BlockSpec((FO_TM, D_MODEL), lambda i, k: (i, 0)),
                  pl.BlockSpec((1, 6, D_MODEL), lambda i, k: (i // blocks_per_batch, 0, 0)),
                  pl.BlockSpec((1, D_MODEL), lambda i, k: (0, 0))],
        out_specs=pl.BlockSpec((FO_TM, D_MODEL), lambda i, k: (i, 0)),
        out_shape=jax.ShapeDtypeStruct((TOKENS, D_MODEL), F32),
        scratch_shapes=[pltpu.VMEM((FO_TM, D_MODEL), F32)],
        compiler_params=_params("arbitrary", "arbitrary", vmem=VMEM_LIMIT_MAX),
        name="ffn_out",
    )(act, w_out, h1, mod3, gain)


def _pad_cols(t, width):
    return jnp.pad(t, ((0, 0), (0, width - t.shape[1])))


def _pad_rows(t, rows):
    return jnp.pad(t, ((0, rows - t.shape[0]), (0, 0)))


def _pack_lora_rows(w_t):
    o3 = HEAD_COLS
    o4 = o3 + DECAY_LORA
    o5 = o4 + ICLR_LORA
    lora = jnp.concatenate([_pad_rows(w_t[o3:o4], LORA_PAD), _pad_rows(w_t[o4:o5], LORA_PAD), w_t[o5:GATE_ROW0]],
                           axis=0)
    return _pad_rows(lora, 2 * SEC - HEAD_COLS)


def _pack_mu(mu):
    o3 = 3 * RWKV_WIDTH
    o4 = o3 + DECAY_LORA
    o5 = o4 + ICLR_LORA
    mu = mu[None, :]
    rw = jnp.concatenate([mu[:, :o3], _pad_cols(mu[:, o3:o4], LORA_PAD), _pad_cols(mu[:, o4:o5], LORA_PAD),
                          mu[:, o5:]], axis=1)
    return _pad_cols(rw, SEC)


def kernel(x, c, positions, w_ada, b_ada, norm_mix, norm_ffn, norm_final, w_in, b_gate, mu_shift, w0, w_decay_up, a0, w_iclr_up, w_gate_up, k_k, k_a, r_k, lnx_w, lnx_b, w_ret_out, w_rwkv_out, w_o, w_ffn_in, w_ffn_out):
    assert x.shape == (BATCH, SEQ, D_MODEL) and w_ada.shape[0] == 1
    x2 = x.reshape(TOKENS, D_MODEL)

    mod = _ada(_pad_rows(c, 8), w_ada[0], b_ada)
    mod3 = mod[:BATCH].reshape(BATCH, 6, D_MODEL)

    w_t = jnp.swapaxes(w_in, 1, 2)[0]
    u1 = _norm1(x2, mod3, norm_mix)
    z = _inproj(u1, w_t, _pack_lora_rows(w_t))

    par = jnp.concatenate([w0, a0, k_k, k_a, r_k.reshape(1, RWKV_WIDTH), lnx_w, lnx_b,
                           jnp.zeros((1, RWKV_WIDTH), F32)], axis=0)
    y_ret, y_rw = _mixers(positions.reshape(BATCH, SEQ, 1), z.reshape(BATCH, SEQ, Z_COLS), _pack_mu(mu_shift[0]), par,
                          _pad_rows(w_decay_up[0], LORA_PAD).astype(BF16),
                          _pad_rows(w_iclr_up[0], LORA_PAD).astype(BF16), w_gate_up[0].astype(BF16))
    y_ret = y_ret.reshape(TOKENS, RET_WIDTH)
    y_rw = y_rw.reshape(TOKENS, RWKV_WIDTH)

    merged = _merge(y_ret, y_rw, w_ret_out[0], w_rwkv_out[0], z, b_gate)
    h1, u2 = _oproj(merged, w_o[0], x2, mod3, norm_ffn)
    act = _ffn_in(u2, w_ffn_in[0])
    out = _ffn_out(act, w_ffn_out[0], h1, mod3, norm_final[None, :])
    return out.reshape(BATCH, SEQ, D_MODEL)
```

```python
import jax
import jax.numpy as jnp
from jax import lax
from jax.experimental import pallas as pl
from jax.experimental.pallas import tpu as pltpu

F32 = jnp.float32
BF16 = jnp.bfloat16

D_MODEL = 2048
BATCH = 2
SEQ = 4096
TOKENS = BATCH * SEQ

RET_HEADS = 4
RET_DIM = 256
RET_WIDTH = RET_HEADS * RET_DIM
RET_CHUNK = 128
ROPE_BASE = 10000.0

RWKV_HEAD = 64
RWKV_WIDTH = 1024
RWKV_HEADS = RWKV_WIDTH // RWKV_HEAD
DECAY_LORA = 96
ICLR_LORA = 96
GATE_LORA = 256
RWKV_CHUNK = 64
STEP_CHUNKS = RET_CHUNK // RWKV_CHUNK
GROUP_LANES = 256
GROUP_HEADS = GROUP_LANES // RWKV_HEAD
RWKV_GROUPS = RWKV_WIDTH // GROUP_LANES
LORA_PAD = 128

FFN_HIDDEN = ((8 * D_MODEL // 3 + 255) // 256) * 256
RET_COLS = 4 * RET_WIDTH
SHIFT_COLS = 3 * RWKV_WIDTH + DECAY_LORA + ICLR_LORA + GATE_LORA
GATE_COLS = 2 * D_MODEL
SEC = 4096
Z_COLS = 3 * SEC
HEAD_COLS = RET_COLS + 3 * RWKV_WIDTH
NORM_EPS = 1e-6
GN_EPS_RET = 1e-5
GN_EPS_RWKV = 64e-5

VMEM_LIMIT = 56 * 1024 * 1024
VMEM_LIMIT_MAX = 60 * 1024 * 1024
SUBLANES = 8


def _params(*sem, vmem=VMEM_LIMIT):
    return pltpu.CompilerParams(dimension_semantics=sem, vmem_limit_bytes=vmem)


def _bdot(a, b):
    return jnp.dot(a.astype(BF16), b.astype(BF16), preferred_element_type=F32)


def _bdot_nt(a, b):
    return lax.dot_general(a.astype(BF16), b.astype(BF16), (((1,), (1,)), ((), ())),
                           preferred_element_type=F32)


def _bdot_tn(a, b):
    return lax.dot_general(a.astype(BF16), b.astype(BF16), (((0,), (0,)), ((), ())),
                           preferred_element_type=F32)


def _silu(x):
    return x * jax.nn.sigmoid(x)


ADA_TN = 1024
ADA_EARLY_COLS = 2 * D_MODEL
ADA_LATE_COLS = 6 * D_MODEL - ADA_EARLY_COLS
ADA_LATE_TN = 256


def _ada_kernel(c_ref, w_ref, b_ref, o_ref):
    o_ref[...] = _bdot(_silu(c_ref[...]), w_ref[...]) + b_ref[...]


def _ada(c_pad, w_ada, b_ada):
    n = ADA_EARLY_COLS
    return pl.pallas_call(
        _ada_kernel,
        grid=(n // ADA_TN,),
        in_specs=[pl.BlockSpec((8, D_MODEL), lambda j: (0, 0)),
                  pl.BlockSpec((D_MODEL, ADA_TN), lambda j: (0, j)),
                  pl.BlockSpec((1, ADA_TN), lambda j: (0, j))],
        out_specs=pl.BlockSpec((8, ADA_TN), lambda j: (0, j)),
        out_shape=jax.ShapeDtypeStruct((8, n), F32),
        compiler_params=_params("arbitrary"),
        name="ada",
    )(c_pad, w_ada, b_ada)


IN_TM = 1024
IN_TN = 1024


def _modulated_norm(x, gain, scale, shift):
    y = x * lax.rsqrt(jnp.mean(x * x, axis=-1, keepdims=True) + NORM_EPS)
    return y * gain * (1.0 + scale) + shift


def _norm1_kernel(x_ref, mod_ref, gain_ref, u_ref):
    mod = mod_ref[0]
    u_ref[...] = _modulated_norm(x_ref[...], gain_ref[...], mod[1:2], mod[0:1]).astype(BF16)


def _norm1(x2, mod3, gain):
    blocks_per_batch = SEQ // IN_TM
    return pl.pallas_call(
        _norm1_kernel,
        grid=(TOKENS // IN_TM,),
        in_specs=[pl.BlockSpec((IN_TM, D_MODEL), lambda i: (i, 0)),
                  pl.BlockSpec((1, mod3.shape[1], D_MODEL), lambda i: (i // blocks_per_batch, 0, 0)),
                  pl.BlockSpec((1, D_MODEL), lambda i: (0, 0))],
        out_specs=pl.BlockSpec((IN_TM, D_MODEL), lambda i: (i, 0)),
        out_shape=jax.ShapeDtypeStruct((TOKENS, D_MODEL), BF16),
        compiler_params=_params("arbitrary"),
        name="norm1",
    )(x2, mod3, gain)


IN_LORA_BLOCK = HEAD_COLS // IN_TN
GATE_ROW0 = RET_COLS + SHIFT_COLS


def _inproj_kernel(u_ref, w_ref, wl_ref, o_ref, wb_ref):
    j = pl.program_id(0)
    i = pl.program_id(1)

    @pl.when((i == 0) & (j != IN_LORA_BLOCK))
    def _():
        wb_ref[...] = w_ref[...].astype(BF16)

    @pl.when((i == 0) & (j == IN_LORA_BLOCK))
    def _():
        wb_ref[...] = wl_ref[...].astype(BF16)

    o_ref[...] = _bdot_nt(u_ref[...], wb_ref[...]).astype(o_ref.dtype)


def _inproj(u, w_t, w_lora):
    assert HEAD_COLS % IN_TN == 0 and 2 * SEC - HEAD_COLS == IN_TN

    def w_row(j, i):
        tn, g0 = IN_TN // SUBLANES, GATE_ROW0 // SUBLANES
        head = jnp.minimum(j, IN_LORA_BLOCK - 1) * tn
        return (jnp.where(j > IN_LORA_BLOCK, g0 + (j - IN_LORA_BLOCK - 1) * tn, head) * SUBLANES, 0)

    return pl.pallas_call(
        _inproj_kernel,
        grid=(Z_COLS // IN_TN, TOKENS // IN_TM),
        in_specs=[pl.BlockSpec((IN_TM, D_MODEL), lambda j, i: (i, 0)),
                  pl.BlockSpec((pl.Element(IN_TN), pl.Element(D_MODEL)), w_row),
                  pl.BlockSpec((IN_TN, D_MODEL), lambda j, i: (0, 0), pipeline_mode=pl.Buffered(1))],
        out_specs=pl.BlockSpec((IN_TM, IN_TN), lambda j, i: (i, j)),
        out_shape=jax.ShapeDtypeStruct((TOKENS, Z_COLS), BF16),
        scratch_shapes=[pltpu.VMEM((IN_TN, D_MODEL), BF16)],
        compiler_params=_params("arbitrary", "arbitrary"),
        name="inproj",
    )(u, w_t, w_lora)


def _rows(ref, cols):
    t = ref[:, :, cols]
    return t.reshape(t.shape[0] * t.shape[1], t.shape[2])


def _ret_stages(pos_ref, z_ref, invf_ref, dintra_ref, zeta_ref, xi_ref, cdec_ref, o_ref, state_ref):
    c = RET_CHUNK
    ang = _rows(pos_ref, slice(None)).astype(F32) * invf_ref[...]
    cos = jnp.cos(ang)
    sin = jnp.sin(ang)
    half = RET_DIM // 2
    yield

    def rot(t):
        t1, t2 = t[:, :half], t[:, half:]
        return jnp.concatenate([t1 * cos - t2 * sin, t1 * sin + t2 * cos], axis=-1)

    heads = range(RET_HEADS)
    units = [(b, h) for b in range(BATCH) for h in heads]
    cols = lambda section, h: slice(section * RET_WIDTH + h * RET_DIM, section * RET_WIDTH + (h + 1) * RET_DIM)
    seq = lambda t, b: t[b * c:(b + 1) * c]
    q_all = [rot(_rows(z_ref, cols(0, h)).astype(F32)).astype(BF16) for h in heads]
    yield
    k_all = [rot(_rows(z_ref, cols(1, h)).astype(F32)) * (RET_DIM ** -0.5) for h in heads]
    v_all = [_rows(z_ref, cols(2, h)).astype(BF16) for h in heads]
    yield
    qb = [seq(q_all[h], b) for b, h in units]
    k = [seq(k_all[h], b) for b, h in units]
    v = [seq(v_all[h], b) for b, h in units]
    scores = [_bdot_nt(qb[u], k[u]) * dintra_ref[h] for u, (b, h) in enumerate(units)]
    state = [state_ref[u] for u in range(len(units))]
    yield
    out = [_bdot(scores[u], v[u]) + _bdot(qb[u], state[u]) * xi_ref[h] for u, (b, h) in enumerate(units)]
    for u, (b, h) in enumerate(units):
        state_ref[u] = state[u] * cdec_ref[h] + _bdot_tn(k[u] * zeta_ref[h], v[u])
    yield
    for u, (b, h) in enumerate(units):
        mu = jnp.mean(out[u], axis=-1, keepdims=True)
        d = out[u] - mu
        var = jnp.mean(d * d, axis=-1, keepdims=True)
        g = z_ref[b, :, cols(3, h)].astype(F32)
        o_ref[b, :, cols(0, h)] = (_silu(g) * (d * lax.rsqrt(var + GN_EPS_RET))).astype(BF16)
        if u % 2:
            yield


def _retention_tables():
    h = RET_HEADS
    half = RET_DIM // 2
    inv_freq = ROPE_BASE ** (-jnp.arange(half, dtype=F32) / half)
    log_gamma = jnp.log(1.0 - 2.0 ** (-5.0 - jnp.arange(h, dtype=F32)))
    idx = jnp.arange(RET_CHUNK, dtype=F32)
    dist = idx[:, None] - idx[None, :]
    decay_intra = jnp.where(dist >= 0, jnp.exp(log_gamma[:, None, None] * jnp.maximum(dist, 0.0)), 0.0)
    zeta = jnp.exp(log_gamma[:, None] * (RET_CHUNK - 1.0 - idx))
    xi = jnp.exp(log_gamma[:, None] * (idx + 1.0))
    chunk_decay = jnp.exp(log_gamma * RET_CHUNK)
    wide = lambda t: jnp.broadcast_to(t[:, :, None], (h, RET_CHUNK, RET_DIM))
    cdec = jnp.broadcast_to(chunk_decay[:, None, None], (h, 1, RET_DIM))
    return inv_freq[None, :], decay_intra, wide(zeta), wide(xi), cdec


def _split2(x):
    hi = x.astype(BF16)
    return hi, (x - hi.astype(F32)).astype(BF16)


def _head_sums(xs, ones_ref):
    c = xs[0].shape[0]
    pieces = [x[:, g * GROUP_LANES:(g + 1) * GROUP_LANES].astype(BF16) for x in xs for g in range(RWKV_GROUPS)]
    res = jnp.dot(jnp.concatenate(pieces, axis=0), ones_ref[...], preferred_element_type=F32)
    return [jnp.concatenate([res[(i * RWKV_GROUPS + g) * c:(i * RWKV_GROUPS + g + 1) * c]
                             for g in range(RWKV_GROUPS)], axis=-1) for i in range(len(xs))]


def _rwkv_stages(zs_ref, mu_ref, par_ref, wd_ref, wa_ref, wg_ref, ltri_ref, ones_ref, mstrict_ref, mincl_ref,
                 o_ref, s_ref, carry_ref):
    c = RWKV_CHUNK
    w = RWKV_WIDTH
    n_blocks = BATCH * STEP_CHUNKS
    seq = lambda t, rb: t[rb * c:(rb + 1) * c]

    def shifted(cols):
        t = zs_ref[:, :, cols].astype(F32)
        first = lax.broadcasted_iota(jnp.int32, t.shape, 1) == 0
        prev = jnp.where(first, carry_ref[:, :, cols], pltpu.roll(t, 1, 1))
        t = t + (prev - t) * mu_ref[:, cols]
        return t.reshape(t.shape[0] * t.shape[1], t.shape[2])

    r, kw, vw = shifted(slice(0, w)), shifted(slice(w, 2 * w)), shifted(slice(2 * w, 3 * w))
    zw = shifted(slice(3 * w, 3 * w + LORA_PAD))
    za = shifted(slice(3 * w + LORA_PAD, 3 * w + 2 * LORA_PAD))
    zg = shifted(slice(3 * w + 2 * LORA_PAD, 3 * w + 2 * LORA_PAD + GATE_LORA))
    carry_ref[...] = zs_ref[:, RET_CHUNK - 1:RET_CHUNK, :].astype(F32)
    par = par_ref[...]
    w0, a0, k_k, k_a, r_k, lnx_w, lnx_b = (par[i:i + 1] for i in range(7))

    dec_pre = w0 + _bdot(jnp.tanh(zw), wd_ref[...])
    softplus = jnp.maximum(-dec_pre, 0.0) + jnp.log1p(jnp.exp(-jnp.abs(dec_pre)))
    logdec = -jnp.exp(-softplus - 0.5)
    iclr = jax.nn.sigmoid(a0 + _bdot(za, wa_ref[...]))

    kk = kw * k_k
    k_mod = kw * (1.0 + (iclr - 1.0) * k_a)
    (kk_sq,) = _head_sums([kk * kk], ones_ref)
    kk = kk * (1.0 / jnp.maximum(jnp.sqrt(kk_sq), 1e-12))
    a = -kk
    b = kk * iclr

    ld_hi = logdec.astype(BF16)
    ld_mid, ld_lo = _split2(logdec - ld_hi.astype(F32))
    ltri = ltri_ref[...]
    cum = (jnp.dot(ltri, ld_hi, preferred_element_type=F32) + jnp.dot(ltri, ld_mid, preferred_element_type=F32)
           + jnp.dot(ltri, ld_lo, preferred_element_type=F32))
    ends = [cum[(rb + 1) * c - 1:(rb + 1) * c, :] for rb in range(n_blocks)]
    cum_end = jnp.concatenate([jnp.broadcast_to(e, (c, w)) for e in ends], axis=0)
    p_inv = jnp.exp(-cum)
    a_t = a * jnp.exp(cum - logdec)
    r_t = r * jnp.exp(cum)
    b_t = (b * p_inv).astype(BF16)
    k_t = (k_mod * p_inv).astype(BF16)

    lane_head = lax.shift_right_logical(lax.broadcasted_iota(jnp.int32, (c, GROUP_LANES), 1), 6)
    head_masks = [lane_head == h for h in range(GROUP_HEADS)]

    def stack(x):
        return jnp.concatenate([jnp.where(m, x, 0.0).astype(BF16) for m in head_masks], axis=0)

    def rep(x):
        return jnp.concatenate([x] * GROUP_HEADS, axis=0)

    m_strict = mstrict_ref[...]
    m_incl = mincl_ref[...]
    eye = (lax.broadcasted_iota(jnp.int32, (GROUP_LANES, GROUP_LANES), 0)
           == lax.broadcasted_iota(jnp.int32, (GROUP_LANES, GROUP_LANES), 1)).astype(F32)
    n4 = GROUP_HEADS * c
    lane_groups = [slice(g * GROUP_LANES, (g + 1) * GROUP_LANES) for g in range(RWKV_GROUPS)]
    units = [(rb, sl) for rb in range(n_blocks) for sl in lane_groups]
    groups = range(len(units))
    unit = lambda t, u: seq(t, units[u][0])[:, units[u][1]]
    a_s = [stack(unit(a_t, u)) for u in groups]
    r_s = [stack(unit(r_t, u)) for u in groups]
    blk = [_bdot_nt(jnp.concatenate([a_s[g], r_s[g]], axis=0),
                    jnp.concatenate([rep(unit(b_t, g)), rep(unit(k_t, g))], axis=0)) for g in groups]
    a_ab = [blk[g][:n4, :n4] * m_strict for g in groups]
    a_ak = [blk[g][:n4, n4:] * m_strict for g in groups]
    a_rb = [blk[g][n4:, :n4] * m_incl for g in groups]
    a_rk = [blk[g][n4:, n4:] * m_incl for g in groups]

    t_inv = [eye + a_ab[g] for g in groups]
    a_pow = a_ab

    def inverse_round():
        nonlocal a_pow, t_inv
        a_pow = [_bdot(a_pow[g], a_pow[g]) for g in groups]
        t_inv = [t_inv[g] + _bdot(t_inv[g], a_pow[g]) for g in groups]

    yield
    inverse_round()
    g_rw = _bdot(jax.nn.sigmoid(zg), wg_ref[...])
    (rk_sum,) = _head_sums([r * k_mod * r_k], ones_ref)
    bonus = rk_sum * vw
    yield
    inverse_round()
    p_end = jnp.exp(cum_end - cum)
    b_e = b * p_end
    k_e = k_mod * p_end
    decay_end = [jnp.exp(e) for e in ends]
    yield
    inverse_round()
    v_s = [stack(unit(vw, u)) for u in groups]
    be_s = [stack(unit(b_e, u)) for u in groups]
    yield
    inverse_round()
    ke_s = [stack(unit(k_e, u)) for u in groups]
    yield
    inverse_round()
    yield

    a_eff = [_bdot(t_inv[g], a_s[g]) for g in groups]
    av = [_bdot(a_ak[g], v_s[g]) for g in groups]
    u_const = [_bdot(t_inv[g], av[g]) for g in groups]
    yield
    state, y_st = {}, {}
    for j in range(STEP_CHUNKS):
        now = [g for g in groups if units[g][0] % STEP_CHUNKS == j]
        slot = lambda g: (units[g][0] // STEP_CHUNKS) * RWKV_GROUPS + g % RWKV_GROUPS
        before = {g: s_ref[slot(g)] if j == 0 else state[g - RWKV_GROUPS] for g in now}
        sb = {g: before[g].astype(BF16) for g in now}
        u = {g: _bdot_nt(a_eff[g], sb[g]) + u_const[g] for g in now}
        for g in now:
            y_st[g] = _bdot_nt(r_s[g], sb[g]) + _bdot(a_rb[g], u[g]) + _bdot(a_rk[g], v_s[g])
        for g in now:
            rb, sl = units[g]
            state[g] = (before[g] * decay_end[rb][:, sl]
                        + _bdot_tn(jnp.concatenate([u[g].astype(BF16), v_s[g]], axis=0),
                                   jnp.concatenate([be_s[g], ke_s[g]], axis=0)))
            if j == STEP_CHUNKS - 1:
                s_ref[slot(g)] = state[g]
        yield
    heads_summed = [t[:c] + t[c:2 * c] + t[2 * c:3 * c] + t[3 * c:] for t in (y_st[g] for g in groups)]
    y = jnp.concatenate([jnp.concatenate(heads_summed[rb * RWKV_GROUPS:(rb + 1) * RWKV_GROUPS], axis=-1)
                         for rb in range(n_blocks)], axis=0)

    (y_sum,) = _head_sums([y], ones_ref)
    d = y - y_sum * (1.0 / RWKV_HEAD)
    (d_sq,) = _head_sums([d * d], ones_ref)
    y = d * lax.rsqrt(d_sq * (1.0 / RWKV_HEAD) + GN_EPS_RWKV) * lnx_w + lnx_b + bonus
    o_ref[...] = (y * g_rw).astype(BF16).reshape(o_ref.shape)


def _emit_interleaved(primary, secondary, pieces_after):
    for n in pieces_after:
        next(primary, None)
        for _ in range(n):
            next(secondary, None)
    for _ in primary:
        pass
    for _ in secondary:
        pass


def _mixer_kernel(pos_ref, zr_ref, zs_ref, invf_ref, dintra_ref, zeta_ref, xi_ref, cdec_ref, mu_ref, par_ref, wd_ref,
                  wa_ref, wg_ref, ltri_ref, ones_ref, mstrict_ref, mincl_ref, crow_ref, wada_ref, bada_ref,
                  or_ref, ow_ref, mod_ref, rs_ref, ws_ref, carry_ref, cact_ref):
    @pl.when(pl.program_id(0) == 0)
    def _():
        rs_ref[...] = jnp.zeros_like(rs_ref)
        ws_ref[...] = jnp.zeros_like(ws_ref)
        carry_ref[...] = jnp.zeros_like(carry_ref)
        cact_ref[...] = _silu(crow_ref[...])

    lanes = crow_ref.shape[-1]
    for b in range(BATCH):
        cols = [jnp.sum(cact_ref[b] * wada_ref[:, k * lanes:(k + 1) * lanes], axis=0, keepdims=True)
                for k in range(ADA_LATE_TN // lanes)]
        mod_ref[b:b + 1, :] = jnp.concatenate(cols, axis=-1) + bada_ref[...]

    _emit_interleaved(
        _rwkv_stages(zs_ref, mu_ref, par_ref, wd_ref, wa_ref, wg_ref, ltri_ref, ones_ref, mstrict_ref, mincl_ref,
                     ow_ref, ws_ref, carry_ref),
        _ret_stages(pos_ref, zr_ref, invf_ref, dintra_ref, zeta_ref, xi_ref, cdec_ref, or_ref, rs_ref),
        pieces_after=(1, 2, 1, 1, 2, 1))


def _rwkv_tables():
    c = RWKV_CHUNK
    tok = jnp.arange(BATCH * STEP_CHUNKS * c)
    ltri = ((tok[:, None] >= tok[None, :]) & (tok[:, None] // c == tok[None, :] // c)).astype(BF16)
    row = jnp.arange(GROUP_LANES)[:, None]
    col = jnp.arange(GROUP_LANES)[None, :]
    same_head = (row // RWKV_HEAD) == (col // RWKV_HEAD)
    ones = same_head.astype(BF16)
    m_strict = (same_head & (row > col)).astype(F32)
    m_incl = (same_head & (row >= col)).astype(F32)
    return ltri, ones, m_strict, m_incl


def _mixers(pos3, z3, mu_p, par, wd, wa, wg, c_rows, w_ada, b_ada):
    assert RWKV_CHUNK == RWKV_HEAD and RET_CHUNK == STEP_CHUNKS * RWKV_CHUNK
    c = RET_CHUNK
    steps = SEQ // c
    assert ADA_LATE_COLS == steps * ADA_LATE_TN and ADA_LATE_TN % c_rows.shape[-1] == 0
    ret_tables = _retention_tables()
    rwkv_tables = _rwkv_tables()
    full = lambda a: pl.BlockSpec(a.shape, lambda n: (0,) * a.ndim)
    rows = lambda width, col: pl.BlockSpec((BATCH, c, width), lambda n: (0, n, col))
    late = lambda n: (0, ADA_EARLY_COLS // ADA_LATE_TN + n)
    return pl.pallas_call(
        _mixer_kernel,
        grid=(steps,),
        in_specs=[rows(1, 0), rows(SEC, 0), rows(SEC, 1), *map(full, ret_tables),
                  full(mu_p), full(par), full(wd), full(wa), full(wg), *map(full, rwkv_tables),
                  pl.BlockSpec(c_rows.shape, lambda n: (0, 0, 0), pipeline_mode=pl.Buffered(1)),
                  pl.BlockSpec((D_MODEL, ADA_LATE_TN), late), pl.BlockSpec((1, ADA_LATE_TN), late)],
        out_specs=[rows(RET_WIDTH, 0), rows(RWKV_WIDTH, 0), pl.BlockSpec((BATCH, ADA_LATE_TN), lambda n: (0, n))],
        out_shape=[jax.ShapeDtypeStruct((BATCH, SEQ, RET_WIDTH), BF16),
                   jax.ShapeDtypeStruct((BATCH, SEQ, RWKV_WIDTH), BF16),
                   jax.ShapeDtypeStruct((BATCH, ADA_LATE_COLS), F32)],
        scratch_shapes=[pltpu.VMEM((BATCH * RET_HEADS, RET_DIM, RET_DIM), F32),
                        pltpu.VMEM((BATCH * RWKV_GROUPS, GROUP_LANES, GROUP_LANES), F32),
                        pltpu.VMEM((BATCH, 1, SEC), F32),
                        pltpu.VMEM(c_rows.shape, F32)],
        compiler_params=_params("arbitrary"),
        name="mixers",
    )(pos3, z3, z3, *ret_tables, mu_p, par, wd, wa, wg, *rwkv_tables, c_rows, w_ada, b_ada)


MG_TM = 1024
MG_TN = 1024


def _merge_kernel(yr_ref, yw_ref, wr_ref, ww_ref, gr_ref, gw_ref, br_ref, bw_ref, o_ref, wrb_ref, wwb_ref):
    @pl.when(pl.program_id(1) == 0)
    def _():
        wrb_ref[...] = wr_ref[...].astype(BF16)
        wwb_ref[...] = ww_ref[...].astype(BF16)

    pr = jnp.dot(yr_ref[...], wrb_ref[...], preferred_element_type=F32)
    pw = jnp.dot(yw_ref[...], wwb_ref[...], preferred_element_type=F32)
    o = (jax.nn.sigmoid(gr_ref[...].astype(F32) + br_ref[...]) * pr
         + jax.nn.sigmoid(gw_ref[...].astype(F32) + bw_ref[...]) * pw)
    o_ref[...] = o.astype(BF16)


def _merge(y_ret, y_rw, w_ret, w_rw, z, b_gate):
    g0 = 2 * SEC // MG_TN
    nd = D_MODEL // MG_TN
    return pl.pallas_call(
        _merge_kernel,
        grid=(nd, TOKENS // MG_TM),
        in_specs=[pl.BlockSpec((MG_TM, RET_WIDTH), lambda j, i: (i, 0)),
                  pl.BlockSpec((MG_TM, RWKV_WIDTH), lambda j, i: (i, 0)),
                  pl.BlockSpec((RET_WIDTH, MG_TN), lambda j, i: (0, j)),
                  pl.BlockSpec((RWKV_WIDTH, MG_TN), lambda j, i: (0, j)),
                  pl.BlockSpec((MG_TM, MG_TN), lambda j, i: (i, g0 + j)),
                  pl.BlockSpec((MG_TM, MG_TN), lambda j, i: (i, g0 + nd + j)),
                  pl.BlockSpec((1, MG_TN), lambda j, i: (0, j)),
                  pl.BlockSpec((1, MG_TN), lambda j, i: (0, nd + j))],
        out_specs=pl.BlockSpec((MG_TM, MG_TN), lambda j, i: (i, j)),
        out_shape=jax.ShapeDtypeStruct((TOKENS, D_MODEL), BF16),
        scratch_shapes=[pltpu.VMEM((RET_WIDTH, MG_TN), BF16), pltpu.VMEM((RWKV_WIDTH, MG_TN), BF16)],
        compiler_params=_params("arbitrary", "arbitrary"),
        name="merge",
    )(y_ret, y_rw, w_ret, w_rw, z, z, b_gate, b_gate)


OP_TM = 512


def _oproj_kernel(m_ref, w_ref, x_ref, mod_ref, gain_ref, h_ref, u_ref, wb_ref):
    @pl.when(pl.program_id(0) == 0)
    def _():
        wb_ref[...] = w_ref[...].astype(BF16)

    mod = mod_ref[0]
    h = x_ref[...] + mod[2:3] * jnp.dot(m_ref[...], wb_ref[...], preferred_element_type=F32)
    h_ref[...] = h
    u_ref[...] = _modulated_norm(h, gain_ref[...], mod[4:5], mod[3:4]).astype(BF16)


def _oproj(merged, w_o, x2, mod3, gain):
    blocks_per_batch = SEQ // OP_TM
    return pl.pallas_call(
        _oproj_kernel,
        grid=(TOKENS // OP_TM,),
        in_specs=[pl.BlockSpec((OP_TM, D_MODEL), lambda i: (i, 0)),
                  pl.BlockSpec((D_MODEL, D_MODEL), lambda i: (0, 0), pipeline_mode=pl.Buffered(1)),
                  pl.BlockSpec((OP_TM, D_MODEL), lambda i: (i, 0)),
                  pl.BlockSpec((1, 6, D_MODEL), lambda i: (i // blocks_per_batch, 0, 0)),
                  pl.BlockSpec((1, D_MODEL), lambda i: (0, 0))],
        out_specs=[pl.BlockSpec((OP_TM, D_MODEL), lambda i: (i, 0)),
                   pl.BlockSpec((OP_TM, D_MODEL), lambda i: (i, 0))],
        out_shape=[jax.ShapeDtypeStruct((TOKENS, D_MODEL), F32),
                   jax.ShapeDtypeStruct((TOKENS, D_MODEL), BF16)],
        scratch_shapes=[pltpu.VMEM((D_MODEL, D_MODEL), BF16)],
        compiler_params=_params("arbitrary"),
        name="oproj",
    )(merged, w_o, x2, mod3, gain)


FI_TM = 1024
FI_TN = 512


def _ffn_in_kernel(u_ref, wg_ref, wu_ref, o_ref, wgb_ref, wub_ref):
    @pl.when(pl.program_id(1) == 0)
    def _():
        wgb_ref[...] = wg_ref[...].astype(BF16)
        wub_ref[...] = wu_ref[...].astype(BF16)

    u = u_ref[...]
    gate = jnp.dot(u, wgb_ref[...], preferred_element_type=F32)
    up = jnp.dot(u, wub_ref[...], preferred_element_type=F32)
    o_ref[...] = (_silu(gate) * up).astype(BF16)


def _ffn_in(u2, w_in):
    nt = FFN_HIDDEN // FI_TN
    return pl.pallas_call(
        _ffn_in_kernel,
        grid=(nt, TOKENS // FI_TM),
        in_specs=[pl.BlockSpec((FI_TM, D_MODEL), lambda j, i: (i, 0)),
                  pl.BlockSpec((D_MODEL, FI_TN), lambda j, i: (0, j)),
                  pl.BlockSpec((D_MODEL, FI_TN), lambda j, i: (0, nt + j))],
        out_specs=pl.BlockSpec((FI_TM, FI_TN), lambda j, i: (i, j)),
        out_shape=jax.ShapeDtypeStruct((TOKENS, FFN_HIDDEN), BF16),
        scratch_shapes=[pltpu.VMEM((D_MODEL, FI_TN), BF16), pltpu.VMEM((D_MODEL, FI_TN), BF16)],
        compiler_params=_params("arbitrary", "arbitrary"),
        name="ffn_in",
    )(u2, w_in, w_in)


FO_TM = 1024
FO_TK = 512


def _ffn_out_kernel(a_ref, w_ref, h_ref, mod_ref, gain_ref, o_ref, acc_ref):
    k = pl.program_id(1)

    @pl.when(k == 0)
    def _():
        acc_ref[...] = jnp.dot(a_ref[...], w_ref[...].astype(BF16), preferred_element_type=F32)

    @pl.when(k > 0)
    def _():
        acc_ref[...] += jnp.dot(a_ref[...], w_ref[...].astype(BF16), preferred_element_type=F32)

    @pl.when(k == pl.num_programs(1) - 1)
    def _():
        h = h_ref[...] + mod_ref[0][5:6] * acc_ref[...]
        o_ref[...] = h * lax.rsqrt(jnp.mean(h * h, axis=-1, keepdims=True) + NORM_EPS) * gain_ref[...]


def _ffn_out(act, w_out, h1, mod3, gain):
    blocks_per_batch = SEQ // FO_TM
    return pl.pallas_call(
        _ffn_out_kernel,
        grid=(TOKENS // FO_TM, FFN_HIDDEN // FO_TK),
        in_specs=[pl.BlockSpec((FO_TM, FO_TK), lambda i, k: (i, k)),
                  pl.BlockSpec((FO_TK, D_MODEL), lambda i, k: (k, 0)),
                  pl.BlockSpec((FO_TM, D_MODEL), lambda i, k: (i, 0)),
                  pl.BlockSpec((1, 6, D_MODEL), lambda i, k: (i // blocks_per_batch, 0, 0)),
                  pl.BlockSpec((1, D_MODEL), lambda i, k: (0, 0))],
        out_specs=pl.BlockSpec((FO_TM, D_MODEL), lambda i, k: (i, 0)),
        out_shape=jax.ShapeDtypeStruct((TOKENS, D_MODEL), F32),
        scratch_shapes=[pltpu.VMEM((FO_TM, D_MODEL), F32)],
        compiler_params=_params("arbitrary", "arbitrary", vmem=VMEM_LIMIT_MAX),
        name="ffn_out",
    )(act, w_out, h1, mod3, gain)


def _pad_cols(t, width):
    return jnp.pad(t, ((0, 0), (0, width - t.shape[1])))


def _pad_rows(t, rows):
    return jnp.pad(t, ((0, rows - t.shape[0]), (0, 0)))


def _pack_lora_rows(w_t):
    o3 = HEAD_COLS
    o4 = o3 + DECAY_LORA
    o5 = o4 + ICLR_LORA
    lora = jnp.concatenate([_pad_rows(w_t[o3:o4], LORA_PAD), _pad_rows(w_t[o4:o5], LORA_PAD), w_t[o5:GATE_ROW0]],
                           axis=0)
    return _pad_rows(lora, 2 * SEC - HEAD_COLS)


def _pack_mu(mu):
    o3 = 3 * RWKV_WIDTH
    o4 = o3 + DECAY_LORA
    o5 = o4 + ICLR_LORA
    mu = mu[None, :]
    rw = jnp.concatenate([mu[:, :o3], _pad_cols(mu[:, o3:o4], LORA_PAD), _pad_cols(mu[:, o4:o5], LORA_PAD),
                          mu[:, o5:]], axis=1)
    return _pad_cols(rw, SEC)


def kernel(x, c, positions, w_ada, b_ada, norm_mix, norm_ffn, norm_final, w_in, b_gate, mu_shift, w0, w_decay_up, a0, w_iclr_up, w_gate_up, k_k, k_a, r_k, lnx_w, lnx_b, w_ret_out, w_rwkv_out, w_o, w_ffn_in, w_ffn_out):
    assert x.shape == (BATCH, SEQ, D_MODEL) and w_ada.shape[0] == 1
    x2 = x.reshape(TOKENS, D_MODEL)

    mod_early = _ada(_pad_rows(c, 8), w_ada[0], b_ada)[:BATCH]

    w_t = jnp.swapaxes(w_in, 1, 2)[0]
    u1 = _norm1(x2, mod_early.reshape(BATCH, 2, D_MODEL), norm_mix)
    z = _inproj(u1, w_t, _pack_lora_rows(w_t))

    par = jnp.concatenate([w0, a0, k_k, k_a, r_k.reshape(1, RWKV_WIDTH), lnx_w, lnx_b,
                           jnp.zeros((1, RWKV_WIDTH), F32)], axis=0)
    c_rows = jnp.broadcast_to(c[:, :, None], (BATCH, D_MODEL, 128))
    y_ret, y_rw, mod_late = _mixers(
        positions.reshape(BATCH, SEQ, 1), z.reshape(BATCH, SEQ, Z_COLS), _pack_mu(mu_shift[0]), par,
        _pad_rows(w_decay_up[0], LORA_PAD).astype(BF16), _pad_rows(w_iclr_up[0], LORA_PAD).astype(BF16),
        w_gate_up[0].astype(BF16), c_rows, w_ada[0], b_ada)
    y_ret = y_ret.reshape(TOKENS, RET_WIDTH)
    y_rw = y_rw.reshape(TOKENS, RWKV_WIDTH)
    mod3 = jnp.concatenate([mod_early, mod_late], axis=1).reshape(BATCH, 6, D_MODEL)

    merged = _merge(y_ret, y_rw, w_ret_out[0], w_rwkv_out[0], z, b_gate)
    h1, u2 = _oproj(merged, w_o[0], x2, mod3, norm_ffn)
    act = _ffn_in(u2, w_ffn_in[0])
    out = _ffn_out(act, w_ffn_out[0], h1, mod3, norm_final[None, :])
    return out.reshape(BATCH, SEQ, D_MODEL)
```
